```python
import jax, jax.numpy as jnp
from jax import lax
import numpy as np

D_MODEL = 4096
BATCH = 4
SEQ = 4096
DEPTH = 4

CTX_LEN = 256
GRID_W = 64

MIXER_POOL = 0
MIXER_ATTN = 1
MIXER_LRU = 2
N_MIXERS = 3
LAYER_MIXER = tuple(i % N_MIXERS for i in range(DEPTH))
LAYER_IS_MOE = tuple(i % 2 == 1 for i in range(DEPTH))
MIXER_INDEX = tuple(LAYER_MIXER[:i].count(LAYER_MIXER[i]) for i in range(DEPTH))
FFN_INDEX = tuple(LAYER_IS_MOE[:i].count(LAYER_IS_MOE[i]) for i in range(DEPTH))
N_POOL_LAYERS = LAYER_MIXER.count(MIXER_POOL)
N_ATTN_LAYERS = LAYER_MIXER.count(MIXER_ATTN)
N_LRU_LAYERS = LAYER_MIXER.count(MIXER_LRU)
N_MOE_LAYERS = LAYER_IS_MOE.count(True)
N_DENSE_LAYERS = DEPTH - N_MOE_LAYERS
CTX_NEEDED_AFTER = tuple(any(LAYER_MIXER[j] != MIXER_POOL for j in range(i + 1, DEPTH)) for i in range(DEPTH))

ADA_RANK = 256
N_MOD = 6
POOL_WINDOWS = (2, 4, 8, 16)
POOL_GROUPS = 4
POOL_GW = D_MODEL // POOL_GROUPS
HEAD_DIM = 128
N_HEADS = D_MODEL // HEAD_DIM
N_KV_HEADS = 8
Q_GROUP = N_HEADS // N_KV_HEADS
Q_DIM = N_HEADS * HEAD_DIM
KV_DIM = N_KV_HEADS * HEAD_DIM
QKV_DIM = Q_DIM + 2 * KV_DIM
AXIS_DIM = HEAD_DIM // 2
ROPE_THETA = 10000.0
Q_BLOCK = 128
ATTN_SCALE = HEAD_DIM ** -0.5
D_RNN = D_MODEL
LRU_HEADS = 16
LRU_BW = D_RNN // LRU_HEADS
CONV_W = 4
CONV_LEFT = CONV_W // 2
LRU_C = 8.0
D_FF = 11008
N_EXPERTS = 8
TOP_K = 2
D_EXPERT = 2048
DEEPNORM_ALPHA = (2.0 * DEPTH) ** 0.25
DEEPNORM_BETA = (8.0 * DEPTH) ** -0.25
LN_EPS = 1e-6
RMS_EPS = 1e-6

kernel_name = 'hybrid_pool_gqa_rglru_moe_diffusion_block'

F32 = jnp.float32


def layer_norm(x, g, b):
    xf = x.astype(F32)
    mu = jnp.mean(xf, axis=-1, keepdims=True)
    var = jnp.mean(jnp.square(xf - mu), axis=-1, keepdims=True)
    return ((xf - mu) * lax.rsqrt(var + LN_EPS)).astype(x.dtype) * g + b


def rms_norm(x, g):
    xf = x.astype(F32)
    return (xf * lax.rsqrt(jnp.mean(jnp.square(xf), axis=-1, keepdims=True) + RMS_EPS)).astype(x.dtype) * g


def adaln(cvec, w_down, w_up, b):
    mod = (jax.nn.silu(cvec) @ w_down) @ w_up + b
    return jnp.split(mod, N_MOD, axis=-1)


def modulate(x, shift, scale):
    return x * (1.0 + scale) + shift


def deepnorm_residual(x, y, g, b):
    return layer_norm(DEEPNORM_ALPHA * x + y, g, b)


def axial_rope_tables(rows):
    r = jnp.repeat(jnp.arange(rows, dtype=F32), GRID_W)
    col = jnp.tile(jnp.arange(GRID_W, dtype=F32), rows)
    inv = ROPE_THETA ** (-jnp.arange(0, AXIS_DIM, 2, dtype=F32) / AXIS_DIM)
    ang_r = r[:, None] * inv
    ang_c = col[:, None] * inv
    return (jnp.cos(ang_r), jnp.sin(ang_r), jnp.cos(ang_c), jnp.sin(ang_c))


def rope2d(x, tabs):
    cos_r, sin_r, cos_c, sin_c = tabs
    xf = x.astype(F32)

    def rot(xa, cos, sin):
        x1, x2 = jnp.split(xa, 2, axis=-1)
        cos = cos[None, :, None, :]
        sin = sin[None, :, None, :]
        return jnp.concatenate([x1 * cos - x2 * sin, x1 * sin + x2 * cos], axis=-1)

    out = jnp.concatenate([rot(xf[..., :AXIS_DIM], cos_r, sin_r), rot(xf[..., AXIS_DIM:], cos_c, sin_c)], axis=-1)
    return out.astype(x.dtype)


def pool_mixer(h, w, scale):
    B, L, _ = h.shape
    cs = jnp.pad(jnp.cumsum(h.astype(F32), axis=1), ((0, 0), (1, 0), (0, 0)))
    t = jnp.arange(L)
    means = []
    for g, k in enumerate(POOL_WINDOWS):
        lo = jnp.clip(t - k // 2, 0, L)
        hi = jnp.clip(t + k - k // 2, 0, L)
        csg = cs[..., g * POOL_GW:(g + 1) * POOL_GW]
        win = jnp.take(csg, hi, axis=1) - jnp.take(csg, lo, axis=1)
        means.append(win / (hi - lo).astype(F32)[None, :, None])
    pooled = jnp.stack(means, axis=2)
    diff = (pooled - h.astype(F32).reshape(B, L, POOL_GROUPS, POOL_GW)).astype(h.dtype)
    y = jnp.einsum('blgi,gij->blgj', diff, w).reshape(B, L, D_MODEL)
    return y * scale


def attn_mixer(h_lat, h_ctx, w_qkv, q_gain, k_gain, w_o, rope, ctx_out):
    B, L, _ = h_lat.shape
    w_q = w_qkv[:, :Q_DIM]
    w_k = w_qkv[:, Q_DIM:Q_DIM + KV_DIM]
    w_v = w_qkv[:, Q_DIM + KV_DIM:]

    def proj_q(h):
        q = (h @ w_q).reshape(h.shape[0], h.shape[1], N_HEADS, HEAD_DIM)
        return rms_norm(q, q_gain) * ATTN_SCALE

    def proj_kv(h):
        k = rms_norm((h @ w_k).reshape(h.shape[0], h.shape[1], N_KV_HEADS, HEAD_DIM), k_gain)
        v = (h @ w_v).reshape(h.shape[0], h.shape[1], N_KV_HEADS, HEAD_DIM)
        return k, v

    q_l = rope2d(proj_q(h_lat), rope)
    k_l, v_l = proj_kv(h_lat)
    k_l = rope2d(k_l, rope)
    k_c, v_c = proj_kv(h_ctx)

    n_blocks = L // Q_BLOCK
    q_blocks = q_l.reshape(B, n_blocks, Q_BLOCK, N_KV_HEADS, Q_GROUP, HEAD_DIM).transpose(1, 0, 2, 3, 4, 5)

    def attend_block(qb):
        s_lat = jnp.einsum('bqkgd,bskd->bkgqs', qb, k_l, preferred_element_type=F32)
        s_ctx = jnp.einsum('bqkgd,bckd->bkgqc', qb, k_c, preferred_element_type=F32)
        p = jax.nn.softmax(jnp.concatenate([s_lat, s_ctx], axis=-1), axis=-1).astype(v_l.dtype)
        return (jnp.einsum('bkgqs,bskd->bqkgd', p[..., :L], v_l)
                + jnp.einsum('bkgqc,bckd->bqkgd', p[..., L:], v_c))

    o = lax.map(attend_block, q_blocks)
    y_lat = o.transpose(1, 0, 2, 3, 4, 5).reshape(B, L, D_MODEL) @ w_o
    y_ctx = None
    if ctx_out:
        C = h_ctx.shape[1]
        q_c = proj_q(h_ctx).reshape(B, C, N_KV_HEADS, Q_GROUP, HEAD_DIM)
        s = jnp.einsum('bqkgd,bckd->bkgqc', q_c, k_c, preferred_element_type=F32)
        p = jax.nn.softmax(s, axis=-1).astype(v_c.dtype)
        y_ctx = jnp.einsum('bkgqc,bckd->bqkgd', p, v_c).reshape(B, C, D_MODEL) @ w_o
    return y_lat, y_ctx


def conv_centred(x, w, b):
    L = x.shape[1]
    xp = jnp.pad(x, ((0, 0), (CONV_LEFT, CONV_W - 1 - CONV_LEFT), (0, 0)))
    out = b + xp[:, 0:L] * w[0]
    for k in range(1, CONV_W):
        out = out + xp[:, k:k + L] * w[k]
    return out


def rglru_coeffs(xr, gate_w, gate_b, lam):
    B, L, _ = xr.shape
    xh = xr.reshape(B, L, LRU_HEADS, LRU_BW)
    g = jnp.einsum('blhi,ghij->gblhj', xh, gate_w) + gate_b[:, None, None]
    r = jax.nn.sigmoid(g[0].astype(F32)).reshape(B, L, D_RNN)
    i = jax.nn.sigmoid(g[1].astype(F32)).reshape(B, L, D_RNN)
    log_a = -LRU_C * r * jax.nn.softplus(-lam.astype(F32))
    a = jnp.exp(log_a)
    b = jnp.sqrt(-jnp.expm1(2.0 * log_a)) * (i * xr.astype(F32))
    return a, b


def linear_scan(a, b, h0, reverse):
    edge = -1 if reverse else 0
    b = b.at[:, edge].add(a[:, edge] * h0)

    def combine(e1, e2):
        a1, b1 = e1
        a2, b2 = e2
        return a1 * a2, a2 * b1 + b2

    _, h = lax.associative_scan(combine, (a, b), reverse=reverse, axis=1)
    return h


def lru_mixer(h_lat, h_ctx, w_in, conv_w, conv_b, gate_w, gate_b, lam, w_out, ctx_out):
    w_y, w_x = w_in[:, :D_RNN], w_in[:, D_RNN:]
    B = h_lat.shape[0]
    xr_l = conv_centred(h_lat @ w_x, conv_w, conv_b)
    xr_c = conv_centred(h_ctx @ w_x, conv_w, conv_b)
    rec_l = []
    rec_c = []
    for d, rev in enumerate((False, True)):
        a_c, b_c = rglru_coeffs(xr_c, gate_w[d], gate_b[d], lam[d])
        s_c = linear_scan(a_c, b_c, jnp.zeros((B, D_RNN), F32), rev)
        h0 = s_c[:, 0] if rev else s_c[:, -1]
        a_l, b_l = rglru_coeffs(xr_l, gate_w[d], gate_b[d], lam[d])
        rec_l.append(linear_scan(a_l, b_l, h0, rev))
        if ctx_out:
            rec_c.append(s_c)
    y_lat = (jax.nn.gelu(h_lat @ w_y, approximate=True) * (rec_l[0] + rec_l[1]).astype(h_lat.dtype)) @ w_out
    y_ctx = None
    if ctx_out:
        y_ctx = (jax.nn.gelu(h_ctx @ w_y, approximate=True) * (rec_c[0] + rec_c[1]).astype(h_ctx.dtype)) @ w_out
    return y_lat, y_ctx


def swiglu(h, w_in, w_out):
    gate, up = jnp.split(h @ w_in, 2, axis=-1)
    return (jax.nn.silu(gate) * up) @ w_out


def moe_ffn(h, router, w_in, w_out):
    logits = jnp.einsum('bld,de->ble', h, router, preferred_element_type=F32)
    top_v, top_i = lax.top_k(logits, TOP_K)
    gates = jax.nn.softmax(top_v, axis=-1)
    combine = jnp.sum(jax.nn.one_hot(top_i, N_EXPERTS, dtype=F32) * gates[..., None], axis=-2)
    y = jnp.zeros_like(h)
    for e in range(N_EXPERTS):
        y = y + combine[..., e:e + 1].astype(h.dtype) * swiglu(h, w_in[e], w_out[e])
    return y


def setup_inputs(seed: int = 0) -> dict:
    key = jax.random.key(seed)
    ks = iter(jax.random.split(key, 40))

    def nrm(shape, scale):
        return jax.random.normal(next(ks), shape, F32) * scale

    inp = {}
    inp['x'] = nrm((BATCH, SEQ, D_MODEL), 1.0)
    inp['c'] = nrm((BATCH, D_MODEL), 1.0)
    inp['ctx'] = nrm((BATCH, CTX_LEN, D_MODEL), 1.0)
    inp['c_ctx'] = nrm((D_MODEL,), 1.0)
    inp['ada_down'] = nrm((DEPTH, D_MODEL, ADA_RANK), D_MODEL ** -0.5)
    inp['ada_up'] = nrm((DEPTH, ADA_RANK, N_MOD * D_MODEL), 0.5 * ADA_RANK ** -0.5)
    inp['ada_b'] = nrm((DEPTH, N_MOD * D_MODEL), 0.02)
    inp['ln_g'] = 1.0 + nrm((DEPTH, 2, D_MODEL), 0.02)
    inp['ln_b'] = nrm((DEPTH, 2, D_MODEL), 0.02)
    inp['pool_w'] = nrm((N_POOL_LAYERS, POOL_GROUPS, POOL_GW, POOL_GW), DEEPNORM_BETA * POOL_GW ** -0.5)
    inp['pool_scale'] = 1.0 + nrm((N_POOL_LAYERS, D_MODEL), 0.1)
    inp['attn_wqkv'] = nrm((N_ATTN_LAYERS, D_MODEL, QKV_DIM), D_MODEL ** -0.5)
    inp['attn_q_gain'] = 1.0 + nrm((N_ATTN_LAYERS, HEAD_DIM), 0.02)
    inp['attn_k_gain'] = 1.0 + nrm((N_ATTN_LAYERS, HEAD_DIM), 0.02)
    inp['attn_wo'] = nrm((N_ATTN_LAYERS, Q_DIM, D_MODEL), DEEPNORM_BETA * Q_DIM ** -0.5)
    inp['lru_w_in'] = nrm((N_LRU_LAYERS, D_MODEL, 2 * D_RNN), D_MODEL ** -0.5)
    inp['lru_conv_w'] = nrm((N_LRU_LAYERS, CONV_W, D_RNN), CONV_W ** -0.5)
    inp['lru_conv_b'] = nrm((N_LRU_LAYERS, D_RNN), 0.02)
    inp['lru_gate_w'] = nrm((N_LRU_LAYERS, 2, 2, LRU_HEADS, LRU_BW, LRU_BW), LRU_BW ** -0.5)
    inp['lru_gate_b'] = nrm((N_LRU_LAYERS, 2, 2, LRU_HEADS, LRU_BW), 0.02)
    a0 = jax.random.uniform(next(ks), (N_LRU_LAYERS, 2, D_RNN), F32, 0.9, 0.999)
    inp['lru_lambda'] = jnp.log(a0) - jnp.log1p(-a0)
    inp['lru_w_out'] = nrm((N_LRU_LAYERS, D_RNN, D_MODEL), DEEPNORM_BETA * D_RNN ** -0.5)
    inp['ffn_w_in'] = nrm((N_DENSE_LAYERS, D_MODEL, 2 * D_FF), D_MODEL ** -0.5)
    inp['ffn_w_out'] = nrm((N_DENSE_LAYERS, D_FF, D_MODEL), DEEPNORM_BETA * D_FF ** -0.5)
    inp['moe_router'] = nrm((N_MOE_LAYERS, D_MODEL, N_EXPERTS), D_MODEL ** -0.5)
    inp['moe_w_in'] = nrm((N_MOE_LAYERS, N_EXPERTS, D_MODEL, 2 * D_EXPERT), D_MODEL ** -0.5)
    inp['moe_w_out'] = nrm((N_MOE_LAYERS, N_EXPERTS, D_EXPERT, D_MODEL), DEEPNORM_BETA * D_EXPERT ** -0.5)
    return inp


def reference(x, c, ctx, c_ctx, ada_down, ada_up, ada_b, ln_g, ln_b,
              pool_w, pool_scale, attn_wqkv, attn_q_gain, attn_k_gain, attn_wo,
              lru_w_in, lru_conv_w, lru_conv_b, lru_gate_w, lru_gate_b, lru_lambda, lru_w_out,
              ffn_w_in, ffn_w_out, moe_router, moe_w_in, moe_w_out):
    rows = x.shape[1] // GRID_W
    rope = axial_rope_tables(rows)
    x_lat, x_ctx = x, ctx
    for i in range(DEPTH):
        mixer, mi, fi = LAYER_MIXER[i], MIXER_INDEX[i], FFN_INDEX[i]
        ctx_keep = CTX_NEEDED_AFTER[i]
        ctx_read = mixer != MIXER_POOL

        def channel_mixer(h):
            if LAYER_IS_MOE[i]:
                return moe_ffn(h, moe_router[fi], moe_w_in[fi], moe_w_out[fi])
            return swiglu(h, ffn_w_in[fi], ffn_w_out[fi])

        sh_m, sc_m, g_m, sh_f, sc_f, g_f = [m[:, None, :] for m in adaln(c, ada_down[i], ada_up[i], ada_b[i])]
        h_lat = modulate(x_lat, sh_m, sc_m)
        h_ctx = None
        if ctx_read or ctx_keep:
            csh_m, csc_m, cg_m, csh_f, csc_f, cg_f = adaln(c_ctx, ada_down[i], ada_up[i], ada_b[i])
            h_ctx = modulate(x_ctx, csh_m, csc_m)

        if mixer == MIXER_POOL:
            y_lat = pool_mixer(h_lat, pool_w[mi], pool_scale[mi])
            y_ctx = pool_mixer(h_ctx, pool_w[mi], pool_scale[mi]) if ctx_keep else None
        elif mixer == MIXER_ATTN:
            y_lat, y_ctx = attn_mixer(h_lat, h_ctx, attn_wqkv[mi], attn_q_gain[mi], attn_k_gain[mi],
                                      attn_wo[mi], rope, ctx_keep)
        else:
            y_lat, y_ctx = lru_mixer(h_lat, h_ctx, lru_w_in[mi], lru_conv_w[mi], lru_conv_b[mi],
                                     lru_gate_w[mi], lru_gate_b[mi], lru_lambda[mi], lru_w_out[mi], ctx_keep)

        x_lat = deepnorm_residual(x_lat, g_m * y_lat, ln_g[i, 0], ln_b[i, 0])
        x_lat = deepnorm_residual(x_lat, g_f * channel_mixer(modulate(x_lat, sh_f, sc_f)), ln_g[i, 1], ln_b[i, 1])
        if ctx_keep:
            x_ctx = deepnorm_residual(x_ctx, cg_m * y_ctx, ln_g[i, 0], ln_b[i, 0])
            x_ctx = deepnorm_residual(x_ctx, cg_f * channel_mixer(modulate(x_ctx, csh_f, csc_f)),
                                      ln_g[i, 1], ln_b[i, 1])
    return x_lat
```

```python
import functools

import jax
import jax.numpy as jnp
from jax import lax
from jax.experimental import pallas as pl
from jax.experimental.pallas import tpu as pltpu

F32 = jnp.float32
BF16 = jnp.bfloat16
HIGHEST = lax.Precision.HIGHEST

HEAD_DIM = 128
AXIS_DIM = HEAD_DIM // 2
GRID_W = 64
ROPE_THETA = 10000.0
LN_EPS = 1e-6
RMS_EPS = 1e-6
LRU_C = 8.0
CONV_W = 4
CONV_LEFT = CONV_W // 2
POOL_WINDOWS = (2, 4, 8, 16)
N_MOD = 6
N_MIXERS = 3
MOD_ROWS = 8

SUBLANES = 8
LANES = 128
VMEM_LIMIT_BYTES = 56 * 1024 * 1024
HALO = SUBLANES


def _params(*sem):
    return pltpu.CompilerParams(dimension_semantics=sem, vmem_limit_bytes=VMEM_LIMIT_BYTES)


def _pick(n, prefs):
    for p in prefs:
        if n % p == 0:
            return p
    return n


def _round_up(n, m):
    return (n + m - 1) // m * m


def _ln_rows(z, g, b):
    mu = jnp.mean(z, axis=-1, keepdims=True)
    zc = z - mu
    var = jnp.mean(zc * zc, axis=-1, keepdims=True)
    return zc * lax.rsqrt(var + LN_EPS) * g + b


class _Stream:
    def __init__(self, batch, seq, ctx_len, d):
        self.batch, self.seq, self.ctx_len, self.d = batch, seq, ctx_len, d
        self.n_lat = batch * seq
        self.n_all = self.n_lat + batch * ctx_len

    def group(self, row0):
        return jnp.minimum(row0 // self.seq, self.batch)

    def mod_spec(self, layer, which, tm):
        def imap(i, *_):
            return ((layer * MOD_ROWS + self.group(i * tm)) * N_MOD + which, 0, 0)
        return pl.BlockSpec((1, 1, self.d), imap)

    def seq_pos(self, row0):
        is_ctx = row0 >= self.n_lat
        pos = jnp.where(is_ctx, (row0 - self.n_lat) % self.ctx_len, row0 % self.seq)
        return pos, jnp.where(is_ctx, self.ctx_len, self.seq)


def _vec_spec(d):
    return pl.BlockSpec((1, d), lambda *_: (0, 0))


def _halo_specs(tm, width, n_rows, col_block=0):
    per = tm // HALO
    last = n_rows // HALO - 1
    prev = pl.BlockSpec((HALO, width), lambda i, *_: (jnp.maximum(i * per - 1, 0), col_block))
    cur = pl.BlockSpec((tm, width), lambda i, *_: (i, col_block))
    nxt = pl.BlockSpec((HALO, width), lambda i, *_: (jnp.minimum((i + 1) * per, last), col_block))
    return prev, cur, nxt


def _adaln_kernel(c_ref, down_ref, up_ref, b_ref, o_ref):
    c = c_ref[...]
    s = c * jax.nn.sigmoid(c)
    t = jnp.dot(s, down_ref[0], preferred_element_type=F32, precision=HIGHEST)
    o_ref[0] = jnp.dot(t, up_ref[0], preferred_element_type=F32, precision=HIGHEST) + b_ref[0]


def _adaln(cvec, ada_down, ada_up, ada_b):
    depth, d, rank = ada_down.shape
    n = ada_up.shape[-1]
    tn = _pick(n, (4096, 2048, 1024, 512, 256, 128))
    out = pl.pallas_call(
        _adaln_kernel,
        out_shape=jax.ShapeDtypeStruct((depth, MOD_ROWS, n), F32),
        grid=(depth, n // tn),
        in_specs=[
            pl.BlockSpec((MOD_ROWS, d), lambda l, j: (0, 0)),
            pl.BlockSpec((1, d, rank), lambda l, j: (l, 0, 0)),
            pl.BlockSpec((1, rank, tn), lambda l, j: (l, 0, j)),
            pl.BlockSpec((1, 1, tn), lambda l, j: (l, 0, j)),
        ],
        out_specs=pl.BlockSpec((1, MOD_ROWS, tn), lambda l, j: (l, 0, j)),
        compiler_params=_params("arbitrary", "arbitrary"),
        name="adaln",
    )(cvec, ada_down, ada_up, ada_b.reshape(depth, 1, n))
    return out.reshape(depth * MOD_ROWS * N_MOD, 1, d)


def _pool_kernel(st, tm, alpha, xp_ref, x_ref, xn_ref, sh_ref, sc_ref, gm_ref, shf_ref, scf_ref,
                 w_ref, ps_ref, lng_ref, lnb_ref, xo_ref, ho_ref, ext_ref, z_ref):
    row0 = pl.program_id(0) * tm
    pos0, seqlen = st.seq_pos(row0)
    first = pos0 == 0
    last = pos0 + tm == seqlen
    one_sc = 1.0 + sc_ref[0]
    sh = sh_ref[0]
    x = x_ref[...]
    ext_ref[pl.ds(HALO, tm), :] = x * one_sc + sh
    ext_ref[pl.ds(0, HALO), :] = jnp.where(first, 0.0, xp_ref[...] * one_sc + sh)
    ext_ref[pl.ds(HALO + tm, HALO), :] = jnp.where(last, 0.0, xn_ref[...] * one_sc + sh)

    pos = pos0 + lax.broadcasted_iota(jnp.int32, (tm, 1), 0)
    groups = len(POOL_WINDOWS)
    gw = st.d // groups
    for g, k in enumerate(POOL_WINDOWS):
        cols = pl.ds(g * gw, gw)
        lo = jnp.maximum(pos - k // 2, 0)
        hi = jnp.minimum(pos + (k - k // 2), seqlen)
        cnt = (hi - lo).astype(F32)
        win = ext_ref[pl.ds(HALO - k // 2, tm), cols]
        for j in range(1, k):
            win = win + ext_ref[pl.ds(HALO - k // 2 + j, tm), cols]
        diff = (win / cnt - ext_ref[pl.ds(HALO, tm), cols]).astype(BF16)
        y = jnp.dot(diff, w_ref[g], preferred_element_type=F32) * ps_ref[:, cols]
        z_ref[:, cols] = alpha * x_ref[:, cols] + gm_ref[0, :, cols] * y
    xn = _ln_rows(z_ref[...], lng_ref[...], lnb_ref[...])
    xo_ref[...] = xn
    ho_ref[...] = (xn * (1.0 + scf_ref[0]) + shf_ref[0]).astype(BF16)


def _pool_layer(st, n_rows, layer, alpha, x, mods, w, pscale, lng, lnb):
    d = st.d
    tm = min(256, st.ctx_len)
    prev, cur, nxt = _halo_specs(tm, d, n_rows)
    groups, gw, _ = w.shape
    return pl.pallas_call(
        functools.partial(_pool_kernel, st, tm, alpha),
        out_shape=(jax.ShapeDtypeStruct((n_rows, d), F32), jax.ShapeDtypeStruct((n_rows, d), BF16)),
        grid=(n_rows // tm,),
        in_specs=[prev, cur, nxt] + [st.mod_spec(layer, q, tm) for q in (0, 1, 2, 3, 4)] + [
            pl.BlockSpec((groups, gw, gw), lambda i: (0, 0, 0)),
            _vec_spec(d), _vec_spec(d), _vec_spec(d)],
        out_specs=(pl.BlockSpec((tm, d), lambda i: (i, 0)), pl.BlockSpec((tm, d), lambda i: (i, 0))),
        scratch_shapes=[pltpu.VMEM((tm + 2 * HALO, d), F32), pltpu.VMEM((tm, d), F32)],
        compiler_params=_params("arbitrary"),
        name="pool_mixer",
    )(x, x, x, mods, mods, mods, mods, mods, w, pscale.reshape(1, d), lng.reshape(1, d), lnb.reshape(1, d))


def _qkv_kernel(n_qk_tiles, a_ref, b_ref, gain_ref, c_ref, s1_ref, s2_ref, o_ref):
    j = pl.program_id(1)
    acc = jnp.dot(a_ref[...], b_ref[...], preferred_element_type=F32)
    tn = acc.shape[1]

    @pl.when(j < n_qk_tiles)
    def _():
        cos, s1, s2 = c_ref[...], s1_ref[...], s2_ref[...]
        for hh in range(tn // HEAD_DIM):
            cols = slice(hh * HEAD_DIM, (hh + 1) * HEAD_DIM)
            xh = acc[:, cols]
            n = xh * lax.rsqrt(jnp.mean(xh * xh, axis=-1, keepdims=True) + RMS_EPS) * gain_ref[:, cols]
            half = AXIS_DIM // 2
            rot = n * cos + pltpu.roll(n, HEAD_DIM - half, 1) * s1 + pltpu.roll(n, half, 1) * s2
            o_ref[:, cols] = rot.astype(o_ref.dtype)

    @pl.when(j >= n_qk_tiles)
    def _():
        o_ref[...] = acc.astype(o_ref.dtype)


def _rope_tables(seq, tm):
    rows = seq // GRID_W
    r = jnp.repeat(jnp.arange(rows, dtype=F32), GRID_W)
    col = jnp.tile(jnp.arange(GRID_W, dtype=F32), rows)
    inv = ROPE_THETA ** (-jnp.arange(0, AXIS_DIM, 2, dtype=F32) / AXIS_DIM)
    ang_r = r[:, None] * inv
    ang_c = col[:, None] * inv
    zero = jnp.zeros_like(ang_r)
    cos = jnp.concatenate([jnp.cos(ang_r), jnp.cos(ang_r), jnp.cos(ang_c), jnp.cos(ang_c)], axis=-1)
    s1 = jnp.concatenate([-jnp.sin(ang_r), zero, -jnp.sin(ang_c), zero], axis=-1)
    s2 = jnp.concatenate([zero, jnp.sin(ang_r), zero, jnp.sin(ang_c)], axis=-1)
    ident = jnp.ones((tm, HEAD_DIM), F32)
    nul = jnp.zeros((tm, HEAD_DIM), F32)
    return (jnp.concatenate([cos, ident]), jnp.concatenate([s1, nul]), jnp.concatenate([s2, nul]))


def _qkv_proj(st, h, w, q_gain, k_gain):
    n_rows, d = h.shape
    n = w.shape[1]
    kv = (n - d) // 2
    tm = next(t for t in (1024, 512, 256, 128, 64, 32, 16, 8) if st.seq % t == 0 and (st.batch * st.ctx_len) % t == 0)
    tn = _pick(kv, (1024, 512, 256, 128))
    scale = HEAD_DIM ** -0.5
    gain = jnp.concatenate([jnp.tile(q_gain * scale, d // HEAD_DIM), jnp.tile(k_gain, kv // HEAD_DIM),
                            jnp.ones((kv,), F32)]).reshape(1, n)
    cos, s1, s2 = _rope_tables(st.seq, tm)
    per_seq = st.seq // tm

    def tab_map(i, j):
        return (jnp.where(i * tm < st.n_lat, i % per_seq, per_seq), 0)

    tab = pl.BlockSpec((tm, HEAD_DIM), tab_map)
    return pl.pallas_call(
        functools.partial(_qkv_kernel, (d + kv) // tn),
        out_shape=jax.ShapeDtypeStruct((n_rows, n), BF16),
        grid=(n_rows // tm, n // tn),
        in_specs=[pl.BlockSpec((tm, d), lambda i, j: (i, 0)), pl.BlockSpec((d, tn), lambda i, j: (0, j)),
                  pl.BlockSpec((1, tn), lambda i, j: (0, j)), tab, tab, tab],
        out_specs=pl.BlockSpec((tm, tn), lambda i, j: (i, j)),
        compiler_params=_params("arbitrary", "arbitrary"),
        name="qkv_proj",
    )(h, w, gain, cos, s1, s2)


def _gelu_tanh(x):
    return 0.5 * x * (1.0 + jnp.tanh(0.7978845608028654 * (x + 0.044715 * (x * x * x))))


def _lru_in_kernel(n_gelu_tiles, a_ref, b_ref, o_ref):
    j = pl.program_id(1)
    acc = jnp.dot(a_ref[...], b_ref[...], preferred_element_type=F32)

    @pl.when(j < n_gelu_tiles)
    def _():
        o_ref[...] = _gelu_tanh(acc)

    @pl.when(j >= n_gelu_tiles)
    def _():
        o_ref[...] = acc


def _lru_in_proj(h, w):
    n_rows, d = h.shape
    n = w.shape[1]
    tm = _pick(n_rows, (1024, 512, 256, 128, 64))
    tn = _pick(n // 2, (1024, 512, 256, 128))
    return pl.pallas_call(
        functools.partial(_lru_in_kernel, (n // 2) // tn),
        out_shape=jax.ShapeDtypeStruct((n_rows, n), F32),
        grid=(n_rows // tm, n // tn),
        in_specs=[pl.BlockSpec((tm, d), lambda i, j: (i, 0)), pl.BlockSpec((d, tn), lambda i, j: (0, j))],
        out_specs=pl.BlockSpec((tm, tn), lambda i, j: (i, j)),
        compiler_params=_params("arbitrary", "arbitrary"),
        name="lru_in_proj",
    )(h, w)


def _swiglu_kernel(a_ref, wg_ref, wu_ref, o_ref):
    a = a_ref[...]
    gate = jnp.dot(a, wg_ref[0], preferred_element_type=F32)
    up = jnp.dot(a, wu_ref[0], preferred_element_type=F32)
    o_ref[...] = (gate * jax.nn.sigmoid(gate) * up).astype(o_ref.dtype)


def _swiglu_up(h, w, n_rows):
    e, d, f2 = w.shape
    f = f2 // 2
    tm = _pick(n_rows, (1024, 512, 256, 128, 64))
    tn = _pick(f, (512, 256, 128))
    nj = f // tn
    return pl.pallas_call(
        _swiglu_kernel,
        out_shape=jax.ShapeDtypeStruct((n_rows, e * f), BF16),
        grid=(n_rows // tm, e, nj),
        in_specs=[pl.BlockSpec((tm, d), lambda i, x, j: (i, 0)),
                  pl.BlockSpec((1, d, tn), lambda i, x, j: (x, 0, j)),
                  pl.BlockSpec((1, d, tn), lambda i, x, j: (x, 0, nj + j))],
        out_specs=pl.BlockSpec((tm, tn), lambda i, x, j: (i, x * nj + j)),
        compiler_params=_params("arbitrary", "arbitrary", "arbitrary"),
        name="swiglu_up",
    )(h, w, w)


def _mm_ln_kernel(alpha, nk, k_per_expert, has_next, *refs):
    refs = list(refs)
    a_ref, b_ref, x_ref, gate_ref, lng_ref, lnb_ref = refs[:6]
    refs = refs[6:]
    comb_ref = refs.pop(0) if k_per_expert else None
    if has_next:
        shn_ref, scn_ref = refs[:2]
        refs = refs[2:]
    xo_ref = refs.pop(0)
    ho_ref = refs.pop(0) if has_next else None
    acc_ref = refs.pop(0)
    k = pl.program_id(1)
    part = jnp.dot(a_ref[...], b_ref[...], preferred_element_type=F32)
    if k_per_expert:
        comb = comb_ref[...]
        e = k // k_per_expert
        lane = lax.broadcasted_iota(jnp.int32, comb.shape, 1)
        part = part * jnp.sum(jnp.where(lane == e, comb, 0.0), axis=1, keepdims=True)

    @pl.when(k == 0)
    def _():
        acc_ref[...] = part

    @pl.when(k > 0)
    def _():
        acc_ref[...] += part

    @pl.when(k == nk - 1)
    def _():
        z = alpha * x_ref[...] + gate_ref[0] * acc_ref[...]
        xn = _ln_rows(z, lng_ref[...], lnb_ref[...])
        xo_ref[...] = xn
        if has_next:
            ho_ref[...] = (xn * (1.0 + scn_ref[0]) + shn_ref[0]).astype(BF16)


def _mm_ln(st, n_rows, alpha, a, b, x, mods, layer, gate_idx, lng, lnb, next_mod=None, comb=None,
           k_per_expert_rows=0):
    kdim, d = b.shape
    tm = min(256, st.ctx_len)
    tk = _pick(k_per_expert_rows if comb is not None else kdim, (1024, 512, 256, 128))
    nk = kdim // tk
    kpe = k_per_expert_rows // tk if comb is not None else 0
    has_next = next_mod is not None
    in_specs = [pl.BlockSpec((tm, tk), lambda i, k: (i, k)), pl.BlockSpec((tk, d), lambda i, k: (k, 0)),
                pl.BlockSpec((tm, d), lambda i, k: (i, 0)), st.mod_spec(layer, gate_idx, tm),
                _vec_spec(d), _vec_spec(d)]
    args = [a, b, x, mods, lng.reshape(1, d), lnb.reshape(1, d)]
    if comb is not None:
        in_specs.append(pl.BlockSpec((tm, comb.shape[1]), lambda i, k: (i, 0)))
        args.append(comb)
    out_shape = [jax.ShapeDtypeStruct((n_rows, d), F32)]
    out_specs = [pl.BlockSpec((tm, d), lambda i, k: (i, 0))]
    if has_next:
        nl, n_sh, n_sc = next_mod
        in_specs += [st.mod_spec(nl, n_sh, tm), st.mod_spec(nl, n_sc, tm)]
        args += [mods, mods]
        out_shape.append(jax.ShapeDtypeStruct((n_rows, d), BF16))
        out_specs.append(pl.BlockSpec((tm, d), lambda i, k: (i, 0)))
    res = pl.pallas_call(
        functools.partial(_mm_ln_kernel, alpha, nk, kpe, has_next),
        out_shape=tuple(out_shape),
        grid=(n_rows // tm, nk),
        in_specs=in_specs,
        out_specs=tuple(out_specs),
        scratch_shapes=[pltpu.VMEM((tm, d), F32)],
        compiler_params=_params("arbitrary", "arbitrary"),
        name="matmul_deepnorm",
    )(*args)
    return res if has_next else (res[0], None)


def _attn_kernel(n_lat_tiles, group, q_ref, kl_ref, vl_ref, kc_ref, vc_ref, o_ref):
    qi = pl.program_id(2)
    nt = (((1,), (1,)), ((), ()))
    kc, vc = kc_ref[...], vc_ref[...]

    @pl.when(qi < n_lat_tiles)
    def _():
        kl, vl = kl_ref[...], vl_ref[...]
        for g in range(group):
            cols = slice(g * HEAD_DIM, (g + 1) * HEAD_DIM)
            q = q_ref[:, cols]
            s_l = lax.dot_general(q, kl, nt, preferred_element_type=F32)
            s_c = lax.dot_general(q, kc, nt, preferred_element_type=F32)
            m = jnp.maximum(jnp.max(s_l, axis=-1, keepdims=True), jnp.max(s_c, axis=-1, keepdims=True))
            p_l = jnp.exp(s_l - m)
            p_c = jnp.exp(s_c - m)
            den = jnp.sum(p_l, axis=-1, keepdims=True) + jnp.sum(p_c, axis=-1, keepdims=True)
            o = (jnp.dot(p_l.astype(BF16), vl, preferred_element_type=F32)
                 + jnp.dot(p_c.astype(BF16), vc, preferred_element_type=F32))
            o_ref[:, cols] = (o / den).astype(o_ref.dtype)

    @pl.when(qi >= n_lat_tiles)
    def _():
        for g in range(group):
            cols = slice(g * HEAD_DIM, (g + 1) * HEAD_DIM)
            s_c = lax.dot_general(q_ref[:, cols], kc, nt, preferred_element_type=F32)
            p_c = jnp.exp(s_c - jnp.max(s_c, axis=-1, keepdims=True))
            den = jnp.sum(p_c, axis=-1, keepdims=True)
            o = jnp.dot(p_c.astype(BF16), vc, preferred_element_type=F32)
            o_ref[:, cols] = (o / den).astype(o_ref.dtype)


def _attention(st, qkv, d):
    n_rows, n = qkv.shape
    kv = (n - d) // 2
    n_kv = kv // HEAD_DIM
    group = d // kv
    tq = min(256, st.ctx_len)
    ctx_tiles = st.ctx_len // tq
    lat_tiles = st.seq // tq
    qw = group * HEAD_DIM
    k_col0 = d // HEAD_DIM
    v_col0 = (d + kv) // HEAD_DIM

    def q_map(b, h, qi):
        row = jnp.where(qi < lat_tiles, b * lat_tiles + qi, st.n_lat // tq + b * ctx_tiles + (qi - lat_tiles))
        return (row, h)

    ctx_blk0 = st.n_lat // st.ctx_len
    return pl.pallas_call(
        functools.partial(_attn_kernel, lat_tiles, group),
        out_shape=jax.ShapeDtypeStruct((n_rows, d), BF16),
        grid=(st.batch, n_kv, lat_tiles + ctx_tiles),
        in_specs=[pl.BlockSpec((tq, qw), q_map),
                  pl.BlockSpec((st.seq, HEAD_DIM), lambda b, h, qi: (b, k_col0 + h)),
                  pl.BlockSpec((st.seq, HEAD_DIM), lambda b, h, qi: (b, v_col0 + h)),
                  pl.BlockSpec((st.ctx_len, HEAD_DIM), lambda b, h, qi: (ctx_blk0 + b, k_col0 + h)),
                  pl.BlockSpec((st.ctx_len, HEAD_DIM), lambda b, h, qi: (ctx_blk0 + b, v_col0 + h))],
        out_specs=pl.BlockSpec((tq, qw), q_map),
        compiler_params=_params("arbitrary", "arbitrary", "arbitrary"),
        name="gqa_attention",
    )(qkv, qkv, qkv, qkv, qkv)


def _scan8(a, b, reverse):
    row = lax.broadcasted_iota(jnp.int32, a.shape, 0)
    for s in (1, 2, 4):
        if reverse:
            keep = row < SUBLANES - s
            shift = SUBLANES - s
        else:
            keep = row >= s
            shift = s
        a_sh = jnp.where(keep, pltpu.roll(a, shift, 0), 1.0)
        b_sh = jnp.where(keep, pltpu.roll(b, shift, 0), 0.0)
        b = a * b_sh + b
        a = a * a_sh
    return a, b


def _lru_kernel(tm, lat_tiles, reverse, *refs):
    if reverse:
        (xp_ref, x_ref, xn_ref, cw_ref, cb_ref, gw_ref, gb_ref, lam_ref, recf_ref, gelu_ref,
         o_ref, ext_ref, a_ref, b_ref, carry_ref) = refs
    else:
        (xp_ref, x_ref, xn_ref, cw_ref, cb_ref, gw_ref, gb_ref, lam_ref,
         o_ref, ext_ref, a_ref, b_ref, carry_ref) = refs
    s = pl.program_id(1)
    is_ctx = s == 0
    tile = lat_tiles - s if reverse else s - 1
    first = jnp.logical_or(is_ctx, tile == 0)
    last = jnp.logical_or(is_ctx, tile == lat_tiles - 1)

    @pl.when(is_ctx)
    def _():
        carry_ref[...] = jnp.zeros_like(carry_ref)

    ext_ref[pl.ds(HALO, tm), :] = x_ref[...]
    ext_ref[pl.ds(0, HALO), :] = jnp.where(first, 0.0, xp_ref[...])
    ext_ref[pl.ds(HALO + tm, HALO), :] = jnp.where(last, 0.0, xn_ref[...])

    heads, bw, _ = gw_ref.shape
    lam = lam_ref[...]
    softplus = jnp.maximum(-lam, 0.0) + jnp.log1p(jnp.exp(-jnp.abs(lam)))
    for h in range(heads):
        cols = pl.ds(h * bw, bw)
        xr = cb_ref[:, cols] + ext_ref[pl.ds(HALO - CONV_LEFT, tm), cols] * cw_ref[0:1, cols]
        for k in range(1, CONV_W):
            xr = xr + ext_ref[pl.ds(HALO - CONV_LEFT + k, tm), cols] * cw_ref[k:k + 1, cols]
        g = jnp.dot(xr.astype(BF16), gw_ref[h], preferred_element_type=F32)
        r = jax.nn.sigmoid(g[:, :bw] + gb_ref[0:1, cols])
        i = jax.nn.sigmoid(g[:, bw:] + gb_ref[1:2, cols])
        log_a = -LRU_C * r * softplus[:, h * bw:(h + 1) * bw]
        a = jnp.exp(log_a)
        a_ref[:, cols] = a
        b_ref[:, cols] = jnp.sqrt(1.0 - a * a) * (i * xr)

    n_chunks = tm // SUBLANES

    def body(c, carry):
        c = n_chunks - 1 - c if reverse else c
        rows = pl.ds(pl.multiple_of(c * SUBLANES, SUBLANES), SUBLANES)
        a_cum, b_cum = _scan8(a_ref[rows, :], b_ref[rows, :], reverse)
        hs = a_cum * carry + b_cum
        if reverse:
            o_ref[rows, :] = (gelu_ref[rows, :] * (recf_ref[rows, :] + hs)).astype(o_ref.dtype)
            edge = hs[0:1, :]
        else:
            o_ref[rows, :] = hs
            edge = hs[SUBLANES - 1:SUBLANES, :]
        return jnp.broadcast_to(edge, carry.shape)

    carry_ref[...] = lax.fori_loop(0, n_chunks, body, carry_ref[...])


def _lru_scan(st, yx, conv_w, conv_b, gate_w, gate_b, lam, reverse, rec_f=None):
    n_rows = yx.shape[0]
    d = st.d
    tm = min(256, st.ctx_len)
    assert st.ctx_len == tm
    lat_tiles = st.seq // tm
    ctx_blk0 = st.n_lat // tm
    per = tm // HALO
    last_halo = n_rows // HALO - 1

    def blk(b, s):
        lat = b * lat_tiles + (lat_tiles - s if reverse else s - 1)
        return jnp.where(s == 0, ctx_blk0 + b, lat)

    heads = gate_w.shape[1]
    bw = d // heads
    gw = jnp.concatenate([gate_w[0], gate_w[1]], axis=-1).astype(BF16)
    in_specs = [pl.BlockSpec((HALO, d), lambda b, s: (jnp.maximum(blk(b, s) * per - 1, 0), 1)),
                pl.BlockSpec((tm, d), lambda b, s: (blk(b, s), 1)),
                pl.BlockSpec((HALO, d), lambda b, s: (jnp.minimum((blk(b, s) + 1) * per, last_halo), 1)),
                pl.BlockSpec((CONV_W, d), lambda b, s: (0, 0)), _vec_spec(d),
                pl.BlockSpec((heads, bw, 2 * bw), lambda b, s: (0, 0, 0)),
                pl.BlockSpec((2, d), lambda b, s: (0, 0)), _vec_spec(d)]
    args = [yx, yx, yx, conv_w, conv_b.reshape(1, d), gw, gate_b.reshape(2, d), lam.reshape(1, d)]
    if reverse:
        in_specs += [pl.BlockSpec((tm, d), lambda b, s: (blk(b, s), 0)),
                     pl.BlockSpec((tm, d), lambda b, s: (blk(b, s), 0))]
        args += [rec_f, yx]
    return pl.pallas_call(
        functools.partial(_lru_kernel, tm, lat_tiles, reverse),
        out_shape=jax.ShapeDtypeStruct((n_rows, d), BF16 if reverse else F32),
        grid=(st.batch, lat_tiles + 1),
        in_specs=in_specs,
        out_specs=pl.BlockSpec((tm, d), lambda b, s: (blk(b, s), 0)),
        scratch_shapes=[pltpu.VMEM((tm + 2 * HALO, d), F32), pltpu.VMEM((tm, d), F32),
                        pltpu.VMEM((tm, d), F32), pltpu.VMEM((SUBLANES, d), F32)],
        compiler_params=_params("arbitrary", "arbitrary"),
        name="rglru_reverse" if reverse else "rglru_forward",
    )(*args)


def _router_kernel(x_ref, sh_ref, sc_ref, w_ref, o_ref):
    h = x_ref[...] * (1.0 + sc_ref[0]) + sh_ref[0]
    logits = jnp.dot(h, w_ref[...], preferred_element_type=F32, precision=HIGHEST)
    n_exp = logits.shape[1]
    lane = lax.broadcasted_iota(jnp.int32, logits.shape, 1).astype(F32)
    m1 = jnp.max(logits, axis=-1, keepdims=True)
    i1 = jnp.min(jnp.where(logits == m1, lane, float(n_exp)), axis=-1, keepdims=True)
    pick1 = lane == i1
    rest = jnp.where(pick1, -jnp.inf, logits)
    m2 = jnp.max(rest, axis=-1, keepdims=True)
    i2 = jnp.min(jnp.where(rest == m2, lane, float(n_exp)), axis=-1, keepdims=True)
    pick2 = lane == i2
    e2 = jnp.exp(m2 - m1)
    den = 1.0 + e2
    o_ref[...] = jnp.where(pick1, 1.0 / den, 0.0) + jnp.where(pick2, e2 / den, 0.0)


def _router(st, n_rows, x, mods, layer, w):
    d, n_exp = w.shape
    tm = min(256, st.ctx_len)
    return pl.pallas_call(
        _router_kernel,
        out_shape=jax.ShapeDtypeStruct((n_rows, n_exp), F32),
        grid=(n_rows // tm,),
        in_specs=[pl.BlockSpec((tm, d), lambda i: (i, 0)), st.mod_spec(layer, 3, tm), st.mod_spec(layer, 4, tm),
                  pl.BlockSpec((d, n_exp), lambda i: (0, 0))],
        out_specs=pl.BlockSpec((tm, n_exp), lambda i: (i, 0)),
        compiler_params=_params("arbitrary"),
        name="moe_router",
    )(x, mods, mods, w)


def _pad_ffn(w_in, w_out, mult):
    d, f2 = w_in.shape
    f = f2 // 2
    fp = _round_up(f, mult)
    pad = ((0, 0), (0, fp - f))
    w_in_p = jnp.concatenate([jnp.pad(w_in[:, :f].astype(BF16), pad), jnp.pad(w_in[:, f:].astype(BF16), pad)], axis=1)
    w_out_p = jnp.pad(w_out.astype(BF16), ((0, fp - f), (0, 0)))
    return w_in_p.reshape(1, d, 2 * fp), w_out_p


def kernel(x, c, ctx, c_ctx, ada_down, ada_up, ada_b, ln_g, ln_b, pool_w, pool_scale, attn_wqkv, attn_q_gain,
           attn_k_gain, attn_wo, lru_w_in, lru_conv_w, lru_conv_b, lru_gate_w, lru_gate_b, lru_lambda, lru_w_out,
           ffn_w_in, ffn_w_out, moe_router, moe_w_in, moe_w_out):
    batch, seq, d = x.shape
    ctx_len = ctx.shape[1]
    depth = ada_down.shape[0]
    assert batch < MOD_ROWS and seq % ctx_len == 0 and ctx_len % SUBLANES == 0
    st = _Stream(batch, seq, ctx_len, d)
    alpha = (2.0 * depth) ** 0.25
    mixers = [i % N_MIXERS for i in range(depth)]
    is_moe = [i % 2 == 1 for i in range(depth)]
    ctx_needed_after = [any(mixers[j] != 0 for j in range(i + 1, depth)) for i in range(depth)]

    cvec = jnp.concatenate([c, c_ctx[None], jnp.zeros((MOD_ROWS - batch - 1, d), F32)], axis=0)
    mods = _adaln(cvec, ada_down, ada_up, ada_b)

    xs = jnp.concatenate([x.reshape(batch * seq, d), ctx.reshape(batch * ctx_len, d)], axis=0)
    n_rows = st.n_all
    h = None
    for i in range(depth):
        mixer = mixers[i]
        mi = mixers[:i].count(mixer)
        fi = is_moe[:i].count(is_moe[i])
        lng, lnb = ln_g[i], ln_b[i]
        if not ctx_needed_after[i] and mixer == 0 and n_rows != st.n_lat:
            n_rows = st.n_lat
            xs = xs[:n_rows]

        if mixer == 0:
            xs, hf = _pool_layer(st, n_rows, i, alpha, xs, mods, pool_w[mi].astype(BF16), pool_scale[mi],
                                 lng[0], lnb[0])
        elif mixer == 1:
            qkv = _qkv_proj(st, h, attn_wqkv[mi].astype(BF16), attn_q_gain[mi], attn_k_gain[mi])
            o = _attention(st, qkv, d)
            rows_out = n_rows if ctx_needed_after[i] else st.n_lat
            xs, hf = _mm_ln(st, rows_out, alpha, o, attn_wo[mi].astype(BF16), xs, mods, i, 2, lng[0], lnb[0],
                            next_mod=(i, 3, 4))
            n_rows = rows_out
        else:
            yx = _lru_in_proj(h, lru_w_in[mi].astype(BF16))
            rec_f = _lru_scan(st, yx, lru_conv_w[mi], lru_conv_b[mi], lru_gate_w[mi, 0], lru_gate_b[mi, 0],
                              lru_lambda[mi, 0], False)
            m = _lru_scan(st, yx, lru_conv_w[mi], lru_conv_b[mi], lru_gate_w[mi, 1], lru_gate_b[mi, 1],
                          lru_lambda[mi, 1], True, rec_f)
            rows_out = n_rows if ctx_needed_after[i] else st.n_lat
            xs, hf = _mm_ln(st, rows_out, alpha, m, lru_w_out[mi].astype(BF16), xs, mods, i, 2, lng[0], lnb[0],
                            next_mod=(i, 3, 4))
            n_rows = rows_out

        next_mod = (i + 1, 0, 1) if i + 1 < depth and mixers[i + 1] != 0 else None
        if is_moe[i]:
            n_exp, _, de2 = moe_w_in.shape[1:]
            comb = _router(st, n_rows, xs, mods, i, moe_router[fi])
            act = _swiglu_up(hf, moe_w_in[fi].astype(BF16), n_rows)
            w_out = moe_w_out[fi].astype(BF16).reshape(n_exp * (de2 // 2), d)
            xs, h = _mm_ln(st, n_rows, alpha, act, w_out, xs, mods, i, 5, lng[1], lnb[1], next_mod=next_mod,
                           comb=comb, k_per_expert_rows=de2 // 2)
        else:
            w_in_p, w_out_p = _pad_ffn(ffn_w_in[fi], ffn_w_out[fi], 1024)
            act = _swiglu_up(hf, w_in_p, n_rows)
            xs, h = _mm_ln(st, n_rows, alpha, act, w_out_p, xs, mods, i, 5, lng[1], lnb[1], next_mod=next_mod)
    return xs[:st.n_lat].reshape(batch, seq, d)
```

```python
import functools

import jax
import jax.numpy as jnp
from jax import lax
from jax.experimental import pallas as pl
from jax.experimental.pallas import tpu as pltpu

F32 = jnp.float32
BF16 = jnp.bfloat16
HIGHEST = lax.Precision.HIGHEST

HEAD_DIM = 128
AXIS_DIM = HEAD_DIM // 2
GRID_W = 64
ROPE_THETA = 10000.0
LN_EPS = 1e-6
RMS_EPS = 1e-6
LRU_C = 8.0
CONV_W = 4
CONV_LEFT = CONV_W // 2
POOL_WINDOWS = (2, 4, 8, 16)
N_MOD = 6
N_MIXERS = 3
MOD_ROWS = 8

SUBLANES = 8
LANES = 128
VMEM_LIMIT_BYTES = 56 * 1024 * 1024
HALO = SUBLANES


def _params(*sem):
    return pltpu.CompilerParams(dimension_semantics=sem, vmem_limit_bytes=VMEM_LIMIT_BYTES)


def _pick(n, prefs):
    for p in prefs:
        if n % p == 0:
            return p
    return n


def _round_up(n, m):
    return (n + m - 1) // m * m


def _ln_rows(z, g, b):
    mu = jnp.mean(z, axis=-1, keepdims=True)
    zc = z - mu
    var = jnp.mean(zc * zc, axis=-1, keepdims=True)
    return zc * lax.rsqrt(var + LN_EPS) * g + b


class _Stream:
    def __init__(self, batch, seq, ctx_len, d):
        self.batch, self.seq, self.ctx_len, self.d = batch, seq, ctx_len, d
        self.n_lat = batch * seq
        self.n_all = self.n_lat + batch * ctx_len

    def row_tile(self, largest):
        t = largest
        while self.seq % t or (self.batch * self.ctx_len) % t:
            t //= 2
        return t

    def group(self, row0):
        return jnp.minimum(row0 // self.seq, self.batch)

    def mod_spec(self, layer, which, tm):
        def imap(i, *_):
            return ((layer * MOD_ROWS + self.group(i * tm)) * N_MOD + which, 0, 0)
        return pl.BlockSpec((1, 1, self.d), imap)

    def seq_pos(self, row0):
        is_ctx = row0 >= self.n_lat
        pos = jnp.where(is_ctx, (row0 - self.n_lat) % self.ctx_len, row0 % self.seq)
        return pos, jnp.where(is_ctx, self.ctx_len, self.seq)


def _vec_spec(d):
    return pl.BlockSpec((1, d), lambda *_: (0, 0))


def _halo_specs(tm, width, n_rows, col_block=0):
    per = tm // HALO
    last = n_rows // HALO - 1
    prev = pl.BlockSpec((HALO, width), lambda i, *_: (jnp.maximum(i * per - 1, 0), col_block))
    cur = pl.BlockSpec((tm, width), lambda i, *_: (i, col_block))
    nxt = pl.BlockSpec((HALO, width), lambda i, *_: (jnp.minimum((i + 1) * per, last), col_block))
    return prev, cur, nxt


def _adaln_kernel(c_ref, down_ref, up_ref, b_ref, o_ref):
    c = c_ref[...]
    s = c * jax.nn.sigmoid(c)
    t = jnp.dot(s, down_ref[0], preferred_element_type=F32, precision=HIGHEST)
    o_ref[0] = jnp.dot(t, up_ref[0], preferred_element_type=F32, precision=HIGHEST) + b_ref[0]


def _adaln(cvec, ada_down, ada_up, ada_b):
    depth, d, rank = ada_down.shape
    n = ada_up.shape[-1]
    tn = _pick(n, (4096, 2048, 1024, 512, 256, 128))
    out = pl.pallas_call(
        _adaln_kernel,
        out_shape=jax.ShapeDtypeStruct((depth, MOD_ROWS, n), F32),
        grid=(depth, n // tn),
        in_specs=[
            pl.BlockSpec((MOD_ROWS, d), lambda l, j: (0, 0)),
            pl.BlockSpec((1, d, rank), lambda l, j: (l, 0, 0)),
            pl.BlockSpec((1, rank, tn), lambda l, j: (l, 0, j)),
            pl.BlockSpec((1, 1, tn), lambda l, j: (l, 0, j)),
        ],
        out_specs=pl.BlockSpec((1, MOD_ROWS, tn), lambda l, j: (l, 0, j)),
        compiler_params=_params("arbitrary", "arbitrary"),
        name="adaln",
    )(cvec, ada_down, ada_up, ada_b.reshape(depth, 1, n))
    return out.reshape(depth * MOD_ROWS * N_MOD, 1, d)


def _pool_kernel(st, tm, alpha, xp_ref, x_ref, xn_ref, sh_ref, sc_ref, gm_ref, shf_ref, scf_ref,
                 w_ref, ps_ref, lng_ref, lnb_ref, xo_ref, ho_ref, ext_ref, z_ref):
    row0 = pl.program_id(0) * tm
    pos0, seqlen = st.seq_pos(row0)
    first = pos0 == 0
    last = pos0 + tm == seqlen
    one_sc = 1.0 + sc_ref[0]
    sh = sh_ref[0]
    x = x_ref[...]
    ext_ref[pl.ds(HALO, tm), :] = x * one_sc + sh
    ext_ref[pl.ds(0, HALO), :] = jnp.where(first, 0.0, xp_ref[...] * one_sc + sh)
    ext_ref[pl.ds(HALO + tm, HALO), :] = jnp.where(last, 0.0, xn_ref[...] * one_sc + sh)

    pos = pos0 + lax.broadcasted_iota(jnp.int32, (tm, 1), 0)
    groups = len(POOL_WINDOWS)
    gw = st.d // groups
    for g, k in enumerate(POOL_WINDOWS):
        cols = pl.ds(g * gw, gw)
        lo = jnp.maximum(pos - k // 2, 0)
        hi = jnp.minimum(pos + (k - k // 2), seqlen)
        cnt = (hi - lo).astype(F32)
        win = ext_ref[pl.ds(HALO - k // 2, tm), cols]
        for j in range(1, k):
            win = win + ext_ref[pl.ds(HALO - k // 2 + j, tm), cols]
        diff = (win / cnt - ext_ref[pl.ds(HALO, tm), cols]).astype(BF16)
        y = jnp.dot(diff, w_ref[g], preferred_element_type=F32) * ps_ref[:, cols]
        z_ref[:, cols] = alpha * x_ref[:, cols] + gm_ref[0, :, cols] * y
    xn = _ln_rows(z_ref[...], lng_ref[...], lnb_ref[...])
    xo_ref[...] = xn
    ho_ref[...] = (xn * (1.0 + scf_ref[0]) + shf_ref[0]).astype(BF16)


def _pool_layer(st, n_rows, layer, alpha, x, mods, w, pscale, lng, lnb):
    d = st.d
    tm = min(256, st.ctx_len)
    prev, cur, nxt = _halo_specs(tm, d, n_rows)
    groups, gw, _ = w.shape
    return pl.pallas_call(
        functools.partial(_pool_kernel, st, tm, alpha),
        out_shape=(jax.ShapeDtypeStruct((n_rows, d), F32), jax.ShapeDtypeStruct((n_rows, d), BF16)),
        grid=(n_rows // tm,),
        in_specs=[prev, cur, nxt] + [st.mod_spec(layer, q, tm) for q in (0, 1, 2, 3, 4)] + [
            pl.BlockSpec((groups, gw, gw), lambda i: (0, 0, 0)),
            _vec_spec(d), _vec_spec(d), _vec_spec(d)],
        out_specs=(pl.BlockSpec((tm, d), lambda i: (i, 0)), pl.BlockSpec((tm, d), lambda i: (i, 0))),
        scratch_shapes=[pltpu.VMEM((tm + 2 * HALO, d), F32), pltpu.VMEM((tm, d), F32)],
        compiler_params=_params("arbitrary"),
        name="pool_mixer",
    )(x, x, x, mods, mods, mods, mods, mods, w, pscale.reshape(1, d), lng.reshape(1, d), lnb.reshape(1, d))


def _qkv_kernel(n_qk_tiles, a_ref, b_ref, gain_ref, c_ref, s1_ref, s2_ref, o_ref):
    j = pl.program_id(1)
    acc = jnp.dot(a_ref[...], b_ref[...], preferred_element_type=F32)
    tn = acc.shape[1]

    @pl.when(j < n_qk_tiles)
    def _():
        cos, s1, s2 = c_ref[...], s1_ref[...], s2_ref[...]
        for hh in range(tn // HEAD_DIM):
            cols = slice(hh * HEAD_DIM, (hh + 1) * HEAD_DIM)
            xh = acc[:, cols]
            n = xh * lax.rsqrt(jnp.mean(xh * xh, axis=-1, keepdims=True) + RMS_EPS) * gain_ref[:, cols]
            half = AXIS_DIM // 2
            rot = n * cos + pltpu.roll(n, HEAD_DIM - half, 1) * s1 + pltpu.roll(n, half, 1) * s2
            o_ref[:, cols] = rot.astype(o_ref.dtype)

    @pl.when(j >= n_qk_tiles)
    def _():
        o_ref[...] = acc.astype(o_ref.dtype)


def _rope_tables(seq, tm):
    rows = seq // GRID_W
    r = jnp.repeat(jnp.arange(rows, dtype=F32), GRID_W)
    col = jnp.tile(jnp.arange(GRID_W, dtype=F32), rows)
    inv = ROPE_THETA ** (-jnp.arange(0, AXIS_DIM, 2, dtype=F32) / AXIS_DIM)
    ang_r = r[:, None] * inv
    ang_c = col[:, None] * inv
    zero = jnp.zeros_like(ang_r)
    cos = jnp.concatenate([jnp.cos(ang_r), jnp.cos(ang_r), jnp.cos(ang_c), jnp.cos(ang_c)], axis=-1)
    s1 = jnp.concatenate([-jnp.sin(ang_r), zero, -jnp.sin(ang_c), zero], axis=-1)
    s2 = jnp.concatenate([zero, jnp.sin(ang_r), zero, jnp.sin(ang_c)], axis=-1)
    ident = jnp.ones((tm, HEAD_DIM), F32)
    nul = jnp.zeros((tm, HEAD_DIM), F32)
    return (jnp.concatenate([cos, ident]), jnp.concatenate([s1, nul]), jnp.concatenate([s2, nul]))


def _qkv_proj(st, h, w, q_gain, k_gain):
    n_rows, d = h.shape
    n = w.shape[1]
    kv = (n - d) // 2
    tm = st.row_tile(1024)
    tn = _pick(kv, (1024, 512, 256, 128))
    scale = HEAD_DIM ** -0.5
    gain = jnp.concatenate([jnp.tile(q_gain * scale, d // HEAD_DIM), jnp.tile(k_gain, kv // HEAD_DIM),
                            jnp.ones((kv,), F32)]).reshape(1, n)
    cos, s1, s2 = _rope_tables(st.seq, tm)
    per_seq = st.seq // tm

    def tab_map(i, j):
        return (jnp.where(i * tm < st.n_lat, i % per_seq, per_seq), 0)

    tab = pl.BlockSpec((tm, HEAD_DIM), tab_map)
    return pl.pallas_call(
        functools.partial(_qkv_kernel, (d + kv) // tn),
        out_shape=jax.ShapeDtypeStruct((n_rows, n), BF16),
        grid=(n_rows // tm, n // tn),
        in_specs=[pl.BlockSpec((tm, d), lambda i, j: (i, 0)), pl.BlockSpec((d, tn), lambda i, j: (0, j)),
                  pl.BlockSpec((1, tn), lambda i, j: (0, j)), tab, tab, tab],
        out_specs=pl.BlockSpec((tm, tn), lambda i, j: (i, j)),
        compiler_params=_params("arbitrary", "arbitrary"),
        name="qkv_proj",
    )(h, w, gain, cos, s1, s2)


def _gelu_tanh(x):
    return 0.5 * x * (1.0 + jnp.tanh(0.7978845608028654 * (x + 0.044715 * (x * x * x))))


def _lru_in_kernel(n_gelu_tiles, a_ref, b_ref, o_ref):
    j = pl.program_id(1)
    acc = jnp.dot(a_ref[...], b_ref[...], preferred_element_type=F32)

    @pl.when(j < n_gelu_tiles)
    def _():
        o_ref[...] = _gelu_tanh(acc)

    @pl.when(j >= n_gelu_tiles)
    def _():
        o_ref[...] = acc


def _lru_in_proj(h, w):
    n_rows, d = h.shape
    n = w.shape[1]
    tm = _pick(n_rows, (1024, 512, 256, 128, 64))
    tn = _pick(n // 2, (1024, 512, 256, 128))
    return pl.pallas_call(
        functools.partial(_lru_in_kernel, (n // 2) // tn),
        out_shape=jax.ShapeDtypeStruct((n_rows, n), F32),
        grid=(n_rows // tm, n // tn),
        in_specs=[pl.BlockSpec((tm, d), lambda i, j: (i, 0)), pl.BlockSpec((d, tn), lambda i, j: (0, j))],
        out_specs=pl.BlockSpec((tm, tn), lambda i, j: (i, j)),
        compiler_params=_params("arbitrary", "arbitrary"),
        name="lru_in_proj",
    )(h, w)


def _swiglu_kernel(a_ref, wg_ref, wu_ref, o_ref):
    a = a_ref[...]
    gate = jnp.dot(a, wg_ref[0], preferred_element_type=F32)
    up = jnp.dot(a, wu_ref[0], preferred_element_type=F32)
    o_ref[...] = (gate * jax.nn.sigmoid(gate) * up).astype(o_ref.dtype)


def _swiglu_up(h, w, n_rows):
    e, d, f2 = w.shape
    f = f2 // 2
    tm = _pick(n_rows, (1024, 512, 256, 128, 64))
    tn = _pick(f, (512, 256, 128))
    nj = f // tn
    return pl.pallas_call(
        _swiglu_kernel,
        out_shape=jax.ShapeDtypeStruct((n_rows, e * f), BF16),
        grid=(n_rows // tm, e, nj),
        in_specs=[pl.BlockSpec((tm, d), lambda i, x, j: (i, 0)),
                  pl.BlockSpec((1, d, tn), lambda i, x, j: (x, 0, j)),
                  pl.BlockSpec((1, d, tn), lambda i, x, j: (x, 0, nj + j))],
        out_specs=pl.BlockSpec((tm, tn), lambda i, x, j: (i, x * nj + j)),
        compiler_params=_params("arbitrary", "arbitrary", "arbitrary"),
        name="swiglu_up",
    )(h, w, w)


def _deepnorm_epilogue(alpha, y, x_ref, gate_ref, lng_ref, lnb_ref, next_refs, xo_ref, ho_ref, rows=slice(None)):
    xn = _ln_rows(alpha * x_ref[rows, :] + gate_ref[0] * y, lng_ref[...], lnb_ref[...])
    xo_ref[rows, :] = xn
    if ho_ref is not None:
        shn_ref, scn_ref = next_refs
        ho_ref[rows, :] = (xn * (1.0 + scn_ref[0]) + shn_ref[0]).astype(BF16)


MM_LN_COL_CHUNK = 1024
MM_LN_ROW_CHUNK = 128


def _mm_ln_kernel(alpha, nk, has_next, *refs):
    a_ref, b_ref, x_ref, gate_ref, lng_ref, lnb_ref = refs[:6]
    next_refs = refs[6:8] if has_next else None
    xo_ref = refs[8] if has_next else refs[6]
    ho_ref = refs[9] if has_next else None
    k = pl.program_id(1)
    tm, d = xo_ref.shape

    @pl.when(k == 0)
    def _():
        xo_ref[...] = jnp.zeros_like(xo_ref)

    a = a_ref[...]
    cw = min(MM_LN_COL_CHUNK, d)
    for c in range(d // cw):
        cols = slice(c * cw, (c + 1) * cw)
        xo_ref[:, cols] += jnp.dot(a, b_ref[:, cols], preferred_element_type=F32)

    @pl.when(k == nk - 1)
    def _():
        rc = min(MM_LN_ROW_CHUNK, tm)

        def body(r, carry):
            rows = pl.ds(pl.multiple_of(r * rc, rc), rc)
            _deepnorm_epilogue(alpha, xo_ref[rows, :], x_ref, gate_ref, lng_ref, lnb_ref, next_refs, xo_ref, ho_ref,
                               rows)
            return carry

        lax.fori_loop(0, tm // rc, body, 0)


def _mm_ln(st, n_rows, alpha, a, b, x, mods, layer, gate_idx, lng, lnb, next_mod=None):
    kdim, d = b.shape
    tm = st.row_tile(512)
    tk = _pick(kdim, (512, 256, 128))
    nk = kdim // tk
    has_next = next_mod is not None
    once = pl.Buffered(1)
    in_specs = [pl.BlockSpec((tm, tk), lambda i, k: (i, k)), pl.BlockSpec((tk, d), lambda i, k: (k, 0)),
                pl.BlockSpec((tm, d), lambda i, k: (i, 0), pipeline_mode=once), st.mod_spec(layer, gate_idx, tm),
                _vec_spec(d), _vec_spec(d)]
    args = [a, b, x, mods, lng.reshape(1, d), lnb.reshape(1, d)]
    out_shape = [jax.ShapeDtypeStruct((n_rows, d), F32)]
    out_specs = [pl.BlockSpec((tm, d), lambda i, k: (i, 0))]
    if has_next:
        nl, n_sh, n_sc = next_mod
        in_specs += [st.mod_spec(nl, n_sh, tm), st.mod_spec(nl, n_sc, tm)]
        args += [mods, mods]
        out_shape.append(jax.ShapeDtypeStruct((n_rows, d), BF16))
        out_specs.append(pl.BlockSpec((tm, d), lambda i, k: (i, 0), pipeline_mode=once))
    res = pl.pallas_call(
        functools.partial(_mm_ln_kernel, alpha, nk, has_next),
        out_shape=tuple(out_shape),
        grid=(n_rows // tm, nk),
        in_specs=in_specs,
        out_specs=tuple(out_specs),
        compiler_params=_params("arbitrary", "arbitrary"),
        name="matmul_deepnorm",
    )(*args)
    return res if has_next else (res[0], None)


def _attn_kernel(n_lat_tiles, group, q_ref, kl_ref, vl_ref, kc_ref, vc_ref, o_ref):
    qi = pl.program_id(2)
    nt = (((1,), (1,)), ((), ()))
    kc, vc = kc_ref[...], vc_ref[...]

    @pl.when(qi < n_lat_tiles)
    def _():
        kl, vl = kl_ref[...], vl_ref[...]
        for g in range(group):
            cols = slice(g * HEAD_DIM, (g + 1) * HEAD_DIM)
            q = q_ref[:, cols]
            s_l = lax.dot_general(q, kl, nt, preferred_element_type=F32)
            s_c = lax.dot_general(q, kc, nt, preferred_element_type=F32)
            m = jnp.maximum(jnp.max(s_l, axis=-1, keepdims=True), jnp.max(s_c, axis=-1, keepdims=True))
            p_l = jnp.exp(s_l - m)
            p_c = jnp.exp(s_c - m)
            den = jnp.sum(p_l, axis=-1, keepdims=True) + jnp.sum(p_c, axis=-1, keepdims=True)
            o = (jnp.dot(p_l.astype(BF16), vl, preferred_element_type=F32)
                 + jnp.dot(p_c.astype(BF16), vc, preferred_element_type=F32))
            o_ref[:, cols] = (o / den).astype(o_ref.dtype)

    @pl.when(qi >= n_lat_tiles)
    def _():
        for g in range(group):
            cols = slice(g * HEAD_DIM, (g + 1) * HEAD_DIM)
            s_c = lax.dot_general(q_ref[:, cols], kc, nt, preferred_element_type=F32)
            p_c = jnp.exp(s_c - jnp.max(s_c, axis=-1, keepdims=True))
            den = jnp.sum(p_c, axis=-1, keepdims=True)
            o = jnp.dot(p_c.astype(BF16), vc, preferred_element_type=F32)
            o_ref[:, cols] = (o / den).astype(o_ref.dtype)


def _attention(st, qkv, d):
    n_rows, n = qkv.shape
    kv = (n - d) // 2
    n_kv = kv // HEAD_DIM
    group = d // kv
    tq = min(256, st.ctx_len)
    ctx_tiles = st.ctx_len // tq
    lat_tiles = st.seq // tq
    qw = group * HEAD_DIM
    k_col0 = d // HEAD_DIM
    v_col0 = (d + kv) // HEAD_DIM

    def q_map(b, h, qi):
        row = jnp.where(qi < lat_tiles, b * lat_tiles + qi, st.n_lat // tq + b * ctx_tiles + (qi - lat_tiles))
        return (row, h)

    ctx_blk0 = st.n_lat // st.ctx_len
    return pl.pallas_call(
        functools.partial(_attn_kernel, lat_tiles, group),
        out_shape=jax.ShapeDtypeStruct((n_rows, d), BF16),
        grid=(st.batch, n_kv, lat_tiles + ctx_tiles),
        in_specs=[pl.BlockSpec((tq, qw), q_map),
                  pl.BlockSpec((st.seq, HEAD_DIM), lambda b, h, qi: (b, k_col0 + h)),
                  pl.BlockSpec((st.seq, HEAD_DIM), lambda b, h, qi: (b, v_col0 + h)),
                  pl.BlockSpec((st.ctx_len, HEAD_DIM), lambda b, h, qi: (ctx_blk0 + b, k_col0 + h)),
                  pl.BlockSpec((st.ctx_len, HEAD_DIM), lambda b, h, qi: (ctx_blk0 + b, v_col0 + h))],
        out_specs=pl.BlockSpec((tq, qw), q_map),
        compiler_params=_params("arbitrary", "arbitrary", "arbitrary"),
        name="gqa_attention",
    )(qkv, qkv, qkv, qkv, qkv)


def _scan8(a, b, reverse):
    row = lax.broadcasted_iota(jnp.int32, a.shape, 0)
    for s in (1, 2, 4):
        if reverse:
            keep = row < SUBLANES - s
            shift = SUBLANES - s
        else:
            keep = row >= s
            shift = s
        a_sh = jnp.where(keep, pltpu.roll(a, shift, 0), 1.0)
        b_sh = jnp.where(keep, pltpu.roll(b, shift, 0), 0.0)
        b = a * b_sh + b
        a = a * a_sh
    return a, b


def _lru_kernel(tm, lat_tiles, reverse, *refs):
    if reverse:
        (xp_ref, x_ref, xn_ref, cw_ref, cb_ref, gw_ref, gb_ref, lam_ref, recf_ref, gelu_ref,
         o_ref, ext_ref, a_ref, b_ref, carry_ref) = refs
    else:
        (xp_ref, x_ref, xn_ref, cw_ref, cb_ref, gw_ref, gb_ref, lam_ref,
         o_ref, ext_ref, a_ref, b_ref, carry_ref) = refs
    s = pl.program_id(1)
    is_ctx = s == 0
    tile = lat_tiles - s if reverse else s - 1
    first = jnp.logical_or(is_ctx, tile == 0)
    last = jnp.logical_or(is_ctx, tile == lat_tiles - 1)

    @pl.when(is_ctx)
    def _():
        carry_ref[...] = jnp.zeros_like(carry_ref)

    ext_ref[pl.ds(HALO, tm), :] = x_ref[...]
    ext_ref[pl.ds(0, HALO), :] = jnp.where(first, 0.0, xp_ref[...])
    ext_ref[pl.ds(HALO + tm, HALO), :] = jnp.where(last, 0.0, xn_ref[...])

    heads, bw, _ = gw_ref.shape
    lam = lam_ref[...]
    softplus = jnp.maximum(-lam, 0.0) + jnp.log1p(jnp.exp(-jnp.abs(lam)))
    for h in range(heads):
        cols = pl.ds(h * bw, bw)
        xr = cb_ref[:, cols] + ext_ref[pl.ds(HALO - CONV_LEFT, tm), cols] * cw_ref[0:1, cols]
        for k in range(1, CONV_W):
            xr = xr + ext_ref[pl.ds(HALO - CONV_LEFT + k, tm), cols] * cw_ref[k:k + 1, cols]
        g = jnp.dot(xr.astype(BF16), gw_ref[h], preferred_element_type=F32)
        r = jax.nn.sigmoid(g[:, :bw] + gb_ref[0:1, cols])
        i = jax.nn.sigmoid(g[:, bw:] + gb_ref[1:2, cols])
        log_a = -LRU_C * r * softplus[:, h * bw:(h + 1) * bw]
        a = jnp.exp(log_a)
        a_ref[:, cols] = a
        b_ref[:, cols] = jnp.sqrt(1.0 - a * a) * (i * xr)

    n_chunks = tm // SUBLANES

    def body(c, carry):
        c = n_chunks - 1 - c if reverse else c
        rows = pl.ds(pl.multiple_of(c * SUBLANES, SUBLANES), SUBLANES)
        a_cum, b_cum = _scan8(a_ref[rows, :], b_ref[rows, :], reverse)
        hs = a_cum * carry + b_cum
        if reverse:
            o_ref[rows, :] = (gelu_ref[rows, :] * (recf_ref[rows, :] + hs)).astype(o_ref.dtype)
            edge = hs[0:1, :]
        else:
            o_ref[rows, :] = hs
            edge = hs[SUBLANES - 1:SUBLANES, :]
        return jnp.broadcast_to(edge, carry.shape)

    carry_ref[...] = lax.fori_loop(0, n_chunks, body, carry_ref[...])


def _lru_scan(st, yx, conv_w, conv_b, gate_w, gate_b, lam, reverse, rec_f=None):
    n_rows = yx.shape[0]
    d = st.d
    tm = min(256, st.ctx_len)
    assert st.ctx_len == tm
    lat_tiles = st.seq // tm
    ctx_blk0 = st.n_lat // tm
    per = tm // HALO
    last_halo = n_rows // HALO - 1

    def blk(b, s):
        lat = b * lat_tiles + (lat_tiles - s if reverse else s - 1)
        return jnp.where(s == 0, ctx_blk0 + b, lat)

    heads = gate_w.shape[1]
    bw = d // heads
    gw = jnp.concatenate([gate_w[0], gate_w[1]], axis=-1).astype(BF16)
    in_specs = [pl.BlockSpec((HALO, d), lambda b, s: (jnp.maximum(blk(b, s) * per - 1, 0), 1)),
                pl.BlockSpec((tm, d), lambda b, s: (blk(b, s), 1)),
                pl.BlockSpec((HALO, d), lambda b, s: (jnp.minimum((blk(b, s) + 1) * per, last_halo), 1)),
                pl.BlockSpec((CONV_W, d), lambda b, s: (0, 0)), _vec_spec(d),
                pl.BlockSpec((heads, bw, 2 * bw), lambda b, s: (0, 0, 0)),
                pl.BlockSpec((2, d), lambda b, s: (0, 0)), _vec_spec(d)]
    args = [yx, yx, yx, conv_w, conv_b.reshape(1, d), gw, gate_b.reshape(2, d), lam.reshape(1, d)]
    if reverse:
        in_specs += [pl.BlockSpec((tm, d), lambda b, s: (blk(b, s), 0)),
                     pl.BlockSpec((tm, d), lambda b, s: (blk(b, s), 0))]
        args += [rec_f, yx]
    return pl.pallas_call(
        functools.partial(_lru_kernel, tm, lat_tiles, reverse),
        out_shape=jax.ShapeDtypeStruct((n_rows, d), BF16 if reverse else F32),
        grid=(st.batch, lat_tiles + 1),
        in_specs=in_specs,
        out_specs=pl.BlockSpec((tm, d), lambda b, s: (blk(b, s), 0)),
        scratch_shapes=[pltpu.VMEM((tm + 2 * HALO, d), F32), pltpu.VMEM((tm, d), F32),
                        pltpu.VMEM((tm, d), F32), pltpu.VMEM((SUBLANES, d), F32)],
        compiler_params=_params("arbitrary", "arbitrary"),
        name="rglru_reverse" if reverse else "rglru_forward",
    )(*args)


ROUTE_LANES = 8
MOE_ROW_TILE = 512


def _lane_pick(rec, k):
    lane = lax.broadcasted_iota(jnp.int32, rec.shape, 1)
    return jnp.sum(jnp.where(lane == k, rec, 0.0), axis=1, keepdims=True)


def _router_kernel(x_ref, sh_ref, sc_ref, w_ref, route_ref, cnt_ref, run_ref):
    @pl.when(pl.program_id(0) == 0)
    def _():
        run_ref[...] = jnp.zeros_like(run_ref)

    h = x_ref[...] * (1.0 + sc_ref[0]) + sh_ref[0]
    logits = jnp.dot(h, w_ref[...], preferred_element_type=F32, precision=HIGHEST)
    tm, n_exp = logits.shape
    lane = lax.broadcasted_iota(jnp.int32, logits.shape, 1).astype(F32)
    m1 = jnp.max(logits, axis=-1, keepdims=True)
    i1 = jnp.min(jnp.where(logits == m1, lane, float(n_exp)), axis=-1, keepdims=True)
    pick1 = lane == i1
    rest = jnp.where(pick1, -jnp.inf, logits)
    m2 = jnp.max(rest, axis=-1, keepdims=True)
    i2 = jnp.min(jnp.where(rest == m2, lane, float(n_exp)), axis=-1, keepdims=True)
    pick2 = lane == i2
    e2 = jnp.exp(m2 - m1)
    den = 1.0 + e2
    onehot = jnp.where(pick1, 1.0, jnp.where(pick2, 1.0, 0.0))
    earlier = (lax.broadcasted_iota(jnp.int32, (tm, tm), 0) > lax.broadcasted_iota(jnp.int32, (tm, tm), 1))
    before = jnp.dot(jnp.where(earlier, 1.0, 0.0).astype(BF16), onehot.astype(BF16),
                     preferred_element_type=F32) + run_ref[...]
    rank1 = jnp.sum(jnp.where(pick1, before, 0.0), axis=-1, keepdims=True)
    rank2 = jnp.sum(jnp.where(pick2, before, 0.0), axis=-1, keepdims=True)
    rec = jnp.zeros((tm, ROUTE_LANES), F32)
    for k, v in enumerate((i1, i2, 1.0 / den, e2 / den, rank1, rank2)):
        rec = jnp.where(lax.broadcasted_iota(jnp.int32, rec.shape, 1) == k, v, rec)
    route_ref[...] = rec
    run_ref[...] += jnp.sum(onehot, axis=0, keepdims=True)
    cnt_ref[...] = run_ref[...]


def _router(st, n_rows, x, mods, layer, w):
    d, n_exp = w.shape
    tm = min(256, st.ctx_len)
    return pl.pallas_call(
        _router_kernel,
        out_shape=(jax.ShapeDtypeStruct((n_rows, ROUTE_LANES), F32), jax.ShapeDtypeStruct((1, n_exp), F32)),
        grid=(n_rows // tm,),
        in_specs=[pl.BlockSpec((tm, d), lambda i: (i, 0)), st.mod_spec(layer, 3, tm), st.mod_spec(layer, 4, tm),
                  pl.BlockSpec((d, n_exp), lambda i: (0, 0))],
        out_specs=(pl.BlockSpec((tm, ROUTE_LANES), lambda i: (i, 0)), pl.BlockSpec((1, n_exp), lambda i: (0, 0))),
        scratch_shapes=[pltpu.VMEM((1, n_exp), F32)],
        compiler_params=_params("arbitrary"),
        name="moe_router",
    )(x, mods, mods, w)


def _dispatch_plan(route, counts, tm):
    n_tok = route.shape[0]
    n_exp = counts.shape[1]
    expert = route[:, 0:2].astype(jnp.int32)
    rank = route[:, 4:6].astype(jnp.int32)
    cnt = counts[0].astype(jnp.int32)
    padded = (cnt + tm - 1) // tm * tm
    ends = jnp.cumsum(padded)
    starts = ends - padded
    dest = (starts[expert] + rank).T.reshape(2 * n_tok)
    n_tiles = (2 * n_tok + n_exp * tm) // tm
    tile_row0 = jnp.arange(n_tiles, dtype=jnp.int32) * tm
    tile_expert = jnp.minimum(jnp.sum(tile_row0[:, None] >= ends[None, :], axis=1), n_exp - 1).astype(jnp.int32)
    n_used = (ends[-1:] // tm).astype(jnp.int32)
    return dest, tile_expert, n_used, n_tiles


def _dispatch_kernel(tm, n_tok, dest_ref, x_ref, sh_ref, sc_ref, init_ref, o_ref, h_ref, sem):
    del init_ref
    base = pl.program_id(0) * tm
    h_ref[...] = x_ref[...] * (1.0 + sc_ref[0]) + sh_ref[0]

    def row_copy(r, d):
        return pltpu.make_async_copy(h_ref.at[pl.ds(r, 1), :], o_ref.at[pl.ds(d, 1), :], sem)

    def issue(r, carry):
        row_copy(r, dest_ref[base + r]).start()
        row_copy(r, dest_ref[n_tok + base + r]).start()
        return carry

    def drain(r, carry):
        row_copy(r, 0).wait()
        row_copy(r, 0).wait()
        return carry

    lax.fori_loop(0, tm, issue, 0)
    lax.fori_loop(0, tm, drain, 0)


def _dispatch(st, n_rows, x, mods, layer, dest, n_out):
    d = st.d
    tm = min(256, st.ctx_len)
    grid_spec = pltpu.PrefetchScalarGridSpec(
        num_scalar_prefetch=1,
        grid=(n_rows // tm,),
        in_specs=[pl.BlockSpec((tm, d), lambda i, dest: (i, 0)), st.mod_spec(layer, 3, tm), st.mod_spec(layer, 4, tm),
                  pl.BlockSpec(memory_space=pl.ANY)],
        out_specs=pl.BlockSpec(memory_space=pl.ANY),
        scratch_shapes=[pltpu.VMEM((tm, d), F32), pltpu.SemaphoreType.DMA],
    )
    return pl.pallas_call(
        functools.partial(_dispatch_kernel, tm, n_rows),
        out_shape=jax.ShapeDtypeStruct((n_out, d), F32),
        grid_spec=grid_spec,
        input_output_aliases={4: 0},
        compiler_params=_params("arbitrary"),
        name="moe_dispatch",
    )(dest, x, mods, mods, jnp.zeros((n_out, d), F32))


def _moe_up_kernel(te_ref, nu_ref, a_ref, wg_ref, wu_ref, o_ref, abf_ref):
    i, j = pl.program_id(0), pl.program_id(1)

    @pl.when(i < nu_ref[0])
    def _():
        @pl.when(j == 0)
        def _():
            abf_ref[...] = a_ref[...].astype(BF16)

        a = abf_ref[...]
        gate = jnp.dot(a, wg_ref[0], preferred_element_type=F32)
        up = jnp.dot(a, wu_ref[0], preferred_element_type=F32)
        o_ref[...] = (gate * jax.nn.sigmoid(gate) * up).astype(o_ref.dtype)

    @pl.when(i >= nu_ref[0])
    def _():
        o_ref[...] = jnp.zeros_like(o_ref)


def _moe_up(xg, w, tile_expert, n_used, tm):
    n_out, d = xg.shape
    f = w.shape[2] // 2
    tn = _pick(f, (512, 256, 128))
    nj = f // tn
    grid_spec = pltpu.PrefetchScalarGridSpec(
        num_scalar_prefetch=2,
        grid=(n_out // tm, nj),
        in_specs=[pl.BlockSpec((tm, d), lambda i, j, te, nu: (i, 0)),
                  pl.BlockSpec((1, d, tn), lambda i, j, te, nu: (te[i], 0, j)),
                  pl.BlockSpec((1, d, tn), lambda i, j, te, nu: (te[i], 0, nj + j))],
        out_specs=pl.BlockSpec((tm, tn), lambda i, j, te, nu: (i, j)),
        scratch_shapes=[pltpu.VMEM((tm, d), BF16)],
    )
    return pl.pallas_call(
        _moe_up_kernel,
        out_shape=jax.ShapeDtypeStruct((n_out, f), BF16),
        grid_spec=grid_spec,
        compiler_params=_params("arbitrary", "arbitrary"),
        name="moe_up",
    )(tile_expert, n_used, xg, w, w)


def _moe_down_kernel(te_ref, nu_ref, a_ref, w_ref, o_ref):
    i = pl.program_id(0)

    @pl.when(i < nu_ref[0])
    def _():
        o_ref[...] = jnp.dot(a_ref[...], w_ref[0], preferred_element_type=F32)

    @pl.when(i >= nu_ref[0])
    def _():
        o_ref[...] = jnp.zeros_like(o_ref)


def _moe_down(act, w, tile_expert, n_used, tm):
    n_out, f = act.shape
    d = w.shape[2]
    tn = _pick(d, (2048, 1024, 512, 256, 128))
    grid_spec = pltpu.PrefetchScalarGridSpec(
        num_scalar_prefetch=2,
        grid=(n_out // tm, d // tn),
        in_specs=[pl.BlockSpec((tm, f), lambda i, j, te, nu: (i, 0)),
                  pl.BlockSpec((1, f, tn), lambda i, j, te, nu: (te[i], 0, j))],
        out_specs=pl.BlockSpec((tm, tn), lambda i, j, te, nu: (i, j)),
    )
    return pl.pallas_call(
        _moe_down_kernel,
        out_shape=jax.ShapeDtypeStruct((n_out, d), F32),
        grid_spec=grid_spec,
        compiler_params=_params("arbitrary", "arbitrary"),
        name="moe_down",
    )(tile_expert, n_used, act, w)


def _combine_kernel(tm, n_tok, alpha, has_next, dest_ref, *refs):
    route_ref, x_ref, gate_ref, lng_ref, lnb_ref = refs[:5]
    next_refs = refs[5:7] if has_next else None
    refs = refs[7:] if has_next else refs[5:]
    y_ref, xo_ref = refs[:2]
    ho_ref = refs[2] if has_next else None
    buf_ref, sem = refs[-2:]
    base = pl.program_id(0) * tm

    def row_copy(slot, r, src):
        return pltpu.make_async_copy(y_ref.at[pl.ds(src, 1), :], buf_ref.at[slot, pl.ds(r, 1), :], sem)

    def issue(r, carry):
        row_copy(0, r, dest_ref[base + r]).start()
        row_copy(1, r, dest_ref[n_tok + base + r]).start()
        return carry

    def drain(r, carry):
        row_copy(0, r, 0).wait()
        row_copy(1, r, 0).wait()
        return carry

    lax.fori_loop(0, tm, issue, 0)
    lax.fori_loop(0, tm, drain, 0)
    rec = route_ref[...]
    y = _lane_pick(rec, 2) * buf_ref[0] + _lane_pick(rec, 3) * buf_ref[1]
    _deepnorm_epilogue(alpha, y, x_ref, gate_ref, lng_ref, lnb_ref, next_refs, xo_ref, ho_ref)


def _combine(st, n_rows, alpha, yg, route, dest, x, mods, layer, lng, lnb, next_mod=None):
    d = st.d
    tm = min(256, st.ctx_len)
    has_next = next_mod is not None
    row = lambda i, dest: (i, 0)
    in_specs = [pl.BlockSpec((tm, ROUTE_LANES), row), pl.BlockSpec((tm, d), row), st.mod_spec(layer, 5, tm),
                _vec_spec(d), _vec_spec(d)]
    args = [route, x, mods, lng.reshape(1, d), lnb.reshape(1, d)]
    out_shape = [jax.ShapeDtypeStruct((n_rows, d), F32)]
    out_specs = [pl.BlockSpec((tm, d), row)]
    if has_next:
        nl, n_sh, n_sc = next_mod
        in_specs += [st.mod_spec(nl, n_sh, tm), st.mod_spec(nl, n_sc, tm)]
        args += [mods, mods]
        out_shape.append(jax.ShapeDtypeStruct((n_rows, d), BF16))
        out_specs.append(pl.BlockSpec((tm, d), row))
    in_specs.append(pl.BlockSpec(memory_space=pl.ANY))
    args.append(yg)
    grid_spec = pltpu.PrefetchScalarGridSpec(
        num_scalar_prefetch=1,
        grid=(n_rows // tm,),
        in_specs=in_specs,
        out_specs=tuple(out_specs),
        scratch_shapes=[pltpu.VMEM((2, tm, d), F32), pltpu.SemaphoreType.DMA],
    )
    res = pl.pallas_call(
        functools.partial(_combine_kernel, tm, n_rows, alpha, has_next),
        out_shape=tuple(out_shape),
        grid_spec=grid_spec,
        compiler_params=_params("arbitrary"),
        name="moe_combine",
    )(dest, *args)
    return res if has_next else (res[0], None)


def _moe_layer(st, n_rows, alpha, x, mods, layer, router_w, w_in, w_out, lng, lnb, next_mod):
    route, counts = _router(st, n_rows, x, mods, layer, router_w)
    dest, tile_expert, n_used, n_tiles = _dispatch_plan(route, counts, MOE_ROW_TILE)
    xg = _dispatch(st, n_rows, x, mods, layer, dest, n_tiles * MOE_ROW_TILE)
    act = _moe_up(xg, w_in, tile_expert, n_used, MOE_ROW_TILE)
    yg = _moe_down(act, w_out, tile_expert, n_used, MOE_ROW_TILE)
    return _combine(st, n_rows, alpha, yg, route, dest, x, mods, layer, lng, lnb, next_mod)


def _pad_ffn(w_in, w_out, mult):
    d, f2 = w_in.shape
    f = f2 // 2
    fp = _round_up(f, mult)
    pad = ((0, 0), (0, fp - f))
    w_in_p = jnp.concatenate([jnp.pad(w_in[:, :f].astype(BF16), pad), jnp.pad(w_in[:, f:].astype(BF16), pad)], axis=1)
    w_out_p = jnp.pad(w_out.astype(BF16), ((0, fp - f), (0, 0)))
    return w_in_p.reshape(1, d, 2 * fp), w_out_p


def kernel(x, c, ctx, c_ctx, ada_down, ada_up, ada_b, ln_g, ln_b, pool_w, pool_scale, attn_wqkv, attn_q_gain,
           attn_k_gain, attn_wo, lru_w_in, lru_conv_w, lru_conv_b, lru_gate_w, lru_gate_b, lru_lambda, lru_w_out,
           ffn_w_in, ffn_w_out, moe_router, moe_w_in, moe_w_out):
    batch, seq, d = x.shape
    ctx_len = ctx.shape[1]
    depth = ada_down.shape[0]
    assert batch < MOD_ROWS and seq % ctx_len == 0 and ctx_len % SUBLANES == 0
    st = _Stream(batch, seq, ctx_len, d)
    alpha = (2.0 * depth) ** 0.25
    mixers = [i % N_MIXERS for i in range(depth)]
    is_moe = [i % 2 == 1 for i in range(depth)]
    ctx_needed_after = [any(mixers[j] != 0 for j in range(i + 1, depth)) for i in range(depth)]

    cvec = jnp.concatenate([c, c_ctx[None], jnp.zeros((MOD_ROWS - batch - 1, d), F32)], axis=0)
    mods = _adaln(cvec, ada_down, ada_up, ada_b)

    xs = jnp.concatenate([x.reshape(batch * seq, d), ctx.reshape(batch * ctx_len, d)], axis=0)
    n_rows = st.n_all
    h = None
    for i in range(depth):
        mixer = mixers[i]
        mi = mixers[:i].count(mixer)
        fi = is_moe[:i].count(is_moe[i])
        lng, lnb = ln_g[i], ln_b[i]
        ffn_mod = None if is_moe[i] else (i, 3, 4)
        if not ctx_needed_after[i] and mixer == 0 and n_rows != st.n_lat:
            n_rows = st.n_lat
            xs = xs[:n_rows]

        if mixer == 0:
            xs, hf = _pool_layer(st, n_rows, i, alpha, xs, mods, pool_w[mi].astype(BF16), pool_scale[mi],
                                 lng[0], lnb[0])
        elif mixer == 1:
            qkv = _qkv_proj(st, h, attn_wqkv[mi].astype(BF16), attn_q_gain[mi], attn_k_gain[mi])
            o = _attention(st, qkv, d)
            rows_out = n_rows if ctx_needed_after[i] else st.n_lat
            xs, hf = _mm_ln(st, rows_out, alpha, o, attn_wo[mi].astype(BF16), xs, mods, i, 2, lng[0], lnb[0],
                            next_mod=ffn_mod)
            n_rows = rows_out
        else:
            yx = _lru_in_proj(h, lru_w_in[mi].astype(BF16))
            rec_f = _lru_scan(st, yx, lru_conv_w[mi], lru_conv_b[mi], lru_gate_w[mi, 0], lru_gate_b[mi, 0],
                              lru_lambda[mi, 0], False)
            m = _lru_scan(st, yx, lru_conv_w[mi], lru_conv_b[mi], lru_gate_w[mi, 1], lru_gate_b[mi, 1],
                          lru_lambda[mi, 1], True, rec_f)
            rows_out = n_rows if ctx_needed_after[i] else st.n_lat
            xs, hf = _mm_ln(st, rows_out, alpha, m, lru_w_out[mi].astype(BF16), xs, mods, i, 2, lng[0], lnb[0],
                            next_mod=ffn_mod)
            n_rows = rows_out

        next_mod = (i + 1, 0, 1) if i + 1 < depth and mixers[i + 1] != 0 else None
        if is_moe[i]:
            xs, h = _moe_layer(st, n_rows, alpha, xs, mods, i, moe_router[fi], moe_w_in[fi].astype(BF16),
                               moe_w_out[fi].astype(BF16), lng[1], lnb[1], next_mod)
        else:
            w_in_p, w_out_p = _pad_ffn(ffn_w_in[fi], ffn_w_out[fi], 1024)
            act = _swiglu_up(hf, w_in_p, n_rows)
            xs, h = _mm_ln(st, n_rows, alpha, act, w_out_p, xs, mods, i, 5, lng[1], lnb[1], next_mod=next_mod)
    return xs[:st.n_lat].reshape(batch, seq, d)
```

```python
import functools

import jax
import jax.numpy as jnp
from jax import lax
from jax.experimental import pallas as pl
from jax.experimental.pallas import tpu as pltpu

F32 = jnp.float32
BF16 = jnp.bfloat16
HIGHEST = lax.Precision.HIGHEST

HEAD_DIM = 128
AXIS_DIM = HEAD_DIM // 2
GRID_W = 64
ROPE_THETA = 10000.0
LN_EPS = 1e-6
RMS_EPS = 1e-6
LRU_C = 8.0
CONV_W = 4
CONV_LEFT = CONV_W // 2
POOL_WINDOWS = (2, 4, 8, 16)
N_MOD = 6
N_MIXERS = 3
MOD_ROWS = 8

SUBLANES = 8
LANES = 128
VMEM_LIMIT_BYTES = 56 * 1024 * 1024
HALO = SUBLANES


def _params(*sem):
    return pltpu.CompilerParams(dimension_semantics=sem, vmem_limit_bytes=VMEM_LIMIT_BYTES)


def _pick(n, prefs):
    for p in prefs:
        if n % p == 0:
            return p
    return n


def _round_up(n, m):
    return (n + m - 1) // m * m


def _ln_rows(z, g, b):
    mu = jnp.mean(z, axis=-1, keepdims=True)
    zc = z - mu
    var = jnp.mean(zc * zc, axis=-1, keepdims=True)
    return zc * lax.rsqrt(var + LN_EPS) * g + b


class _Stream:
    def __init__(self, batch, seq, ctx_len, d):
        self.batch, self.seq, self.ctx_len, self.d = batch, seq, ctx_len, d
        self.n_lat = batch * seq
        self.n_all = self.n_lat + batch * ctx_len

    def row_tile(self, largest):
        t = largest
        while self.seq % t or (self.batch * self.ctx_len) % t:
            t //= 2
        return t

    def group(self, row0):
        return jnp.minimum(row0 // self.seq, self.batch)

    def mod_spec(self, layer, which, tm):
        def imap(i, *_):
            return ((layer * MOD_ROWS + self.group(i * tm)) * N_MOD + which, 0, 0)
        return pl.BlockSpec((1, 1, self.d), imap)

    def seq_pos(self, row0):
        is_ctx = row0 >= self.n_lat
        pos = jnp.where(is_ctx, (row0 - self.n_lat) % self.ctx_len, row0 % self.seq)
        return pos, jnp.where(is_ctx, self.ctx_len, self.seq)


def _vec_spec(d):
    return pl.BlockSpec((1, d), lambda *_: (0, 0))


def _halo_specs(tm, width, n_rows, col_block=0):
    per = tm // HALO
    last = n_rows // HALO - 1
    prev = pl.BlockSpec((HALO, width), lambda i, *_: (jnp.maximum(i * per - 1, 0), col_block))
    cur = pl.BlockSpec((tm, width), lambda i, *_: (i, col_block))
    nxt = pl.BlockSpec((HALO, width), lambda i, *_: (jnp.minimum((i + 1) * per, last), col_block))
    return prev, cur, nxt


def _adaln_kernel(c_ref, down_ref, up_ref, b_ref, o_ref):
    c = c_ref[...]
    s = c * jax.nn.sigmoid(c)
    t = jnp.dot(s, down_ref[0], preferred_element_type=F32, precision=HIGHEST)
    o_ref[0] = jnp.dot(t, up_ref[0], preferred_element_type=F32, precision=HIGHEST) + b_ref[0]


def _adaln(cvec, ada_down, ada_up, ada_b):
    depth, d, rank = ada_down.shape
    n = ada_up.shape[-1]
    tn = _pick(n, (4096, 2048, 1024, 512, 256, 128))
    out = pl.pallas_call(
        _adaln_kernel,
        out_shape=jax.ShapeDtypeStruct((depth, MOD_ROWS, n), F32),
        grid=(depth, n // tn),
        in_specs=[
            pl.BlockSpec((MOD_ROWS, d), lambda l, j: (0, 0)),
            pl.BlockSpec((1, d, rank), lambda l, j: (l, 0, 0)),
            pl.BlockSpec((1, rank, tn), lambda l, j: (l, 0, j)),
            pl.BlockSpec((1, 1, tn), lambda l, j: (l, 0, j)),
        ],
        out_specs=pl.BlockSpec((1, MOD_ROWS, tn), lambda l, j: (l, 0, j)),
        compiler_params=_params("arbitrary", "arbitrary"),
        name="adaln",
    )(cvec, ada_down, ada_up, ada_b.reshape(depth, 1, n))
    return out.reshape(depth * MOD_ROWS * N_MOD, 1, d)


def _pool_kernel(st, tm, alpha, xp_ref, x_ref, xn_ref, sh_ref, sc_ref, gm_ref, shf_ref, scf_ref,
                 w_ref, ps_ref, lng_ref, lnb_ref, xo_ref, ho_ref, ext_ref, z_ref):
    row0 = pl.program_id(0) * tm
    pos0, seqlen = st.seq_pos(row0)
    first = pos0 == 0
    last = pos0 + tm == seqlen
    one_sc = 1.0 + sc_ref[0]
    sh = sh_ref[0]
    x = x_ref[...]
    ext_ref[pl.ds(HALO, tm), :] = x * one_sc + sh
    ext_ref[pl.ds(0, HALO), :] = jnp.where(first, 0.0, xp_ref[...] * one_sc + sh)
    ext_ref[pl.ds(HALO + tm, HALO), :] = jnp.where(last, 0.0, xn_ref[...] * one_sc + sh)

    pos = pos0 + lax.broadcasted_iota(jnp.int32, (tm, 1), 0)
    groups = len(POOL_WINDOWS)
    gw = st.d // groups
    for g, k in enumerate(POOL_WINDOWS):
        cols = pl.ds(g * gw, gw)
        lo = jnp.maximum(pos - k // 2, 0)
        hi = jnp.minimum(pos + (k - k // 2), seqlen)
        cnt = (hi - lo).astype(F32)
        win = ext_ref[pl.ds(HALO - k // 2, tm), cols]
        for j in range(1, k):
            win = win + ext_ref[pl.ds(HALO - k // 2 + j, tm), cols]
        diff = (win / cnt - ext_ref[pl.ds(HALO, tm), cols]).astype(BF16)
        y = jnp.dot(diff, w_ref[g], preferred_element_type=F32) * ps_ref[:, cols]
        z_ref[:, cols] = alpha * x_ref[:, cols] + gm_ref[0, :, cols] * y
    xn = _ln_rows(z_ref[...], lng_ref[...], lnb_ref[...])
    xo_ref[...] = xn
    ho_ref[...] = (xn * (1.0 + scf_ref[0]) + shf_ref[0]).astype(BF16)


def _pool_layer(st, n_rows, layer, alpha, x, mods, w, pscale, lng, lnb):
    d = st.d
    tm = min(256, st.ctx_len)
    prev, cur, nxt = _halo_specs(tm, d, n_rows)
    groups, gw, _ = w.shape
    return pl.pallas_call(
        functools.partial(_pool_kernel, st, tm, alpha),
        out_shape=(jax.ShapeDtypeStruct((n_rows, d), F32), jax.ShapeDtypeStruct((n_rows, d), BF16)),
        grid=(n_rows // tm,),
        in_specs=[prev, cur, nxt] + [st.mod_spec(layer, q, tm) for q in (0, 1, 2, 3, 4)] + [
            pl.BlockSpec((groups, gw, gw), lambda i: (0, 0, 0)),
            _vec_spec(d), _vec_spec(d), _vec_spec(d)],
        out_specs=(pl.BlockSpec((tm, d), lambda i: (i, 0)), pl.BlockSpec((tm, d), lambda i: (i, 0))),
        scratch_shapes=[pltpu.VMEM((tm + 2 * HALO, d), F32), pltpu.VMEM((tm, d), F32)],
        compiler_params=_params("arbitrary"),
        name="pool_mixer",
    )(x, x, x, mods, mods, mods, mods, mods, w, pscale.reshape(1, d), lng.reshape(1, d), lnb.reshape(1, d))


def _qkv_kernel(n_qk_tiles, a_ref, b_ref, gain_ref, c_ref, s1_ref, s2_ref, o_ref):
    j = pl.program_id(1)
    acc = jnp.dot(a_ref[...], b_ref[...], preferred_element_type=F32)
    tn = acc.shape[1]

    @pl.when(j < n_qk_tiles)
    def _():
        cos, s1, s2 = c_ref[...], s1_ref[...], s2_ref[...]
        for hh in range(tn // HEAD_DIM):
            cols = slice(hh * HEAD_DIM, (hh + 1) * HEAD_DIM)
            xh = acc[:, cols]
            n = xh * lax.rsqrt(jnp.mean(xh * xh, axis=-1, keepdims=True) + RMS_EPS) * gain_ref[:, cols]
            half = AXIS_DIM // 2
            rot = n * cos + pltpu.roll(n, HEAD_DIM - half, 1) * s1 + pltpu.roll(n, half, 1) * s2
            o_ref[:, cols] = rot.astype(o_ref.dtype)

    @pl.when(j >= n_qk_tiles)
    def _():
        o_ref[...] = acc.astype(o_ref.dtype)


def _rope_tables(seq, tm):
    rows = seq // GRID_W
    r = jnp.repeat(jnp.arange(rows, dtype=F32), GRID_W)
    col = jnp.tile(jnp.arange(GRID_W, dtype=F32), rows)
    inv = ROPE_THETA ** (-jnp.arange(0, AXIS_DIM, 2, dtype=F32) / AXIS_DIM)
    ang_r = r[:, None] * inv
    ang_c = col[:, None] * inv
    zero = jnp.zeros_like(ang_r)
    cos = jnp.concatenate([jnp.cos(ang_r), jnp.cos(ang_r), jnp.cos(ang_c), jnp.cos(ang_c)], axis=-1)
    s1 = jnp.concatenate([-jnp.sin(ang_r), zero, -jnp.sin(ang_c), zero], axis=-1)
    s2 = jnp.concatenate([zero, jnp.sin(ang_r), zero, jnp.sin(ang_c)], axis=-1)
    ident = jnp.ones((tm, HEAD_DIM), F32)
    nul = jnp.zeros((tm, HEAD_DIM), F32)
    return (jnp.concatenate([cos, ident]), jnp.concatenate([s1, nul]), jnp.concatenate([s2, nul]))


def _qkv_proj(st, h, w, q_gain, k_gain):
    n_rows, d = h.shape
    n = w.shape[1]
    kv = (n - d) // 2
    tm = st.row_tile(1024)
    tn = _pick(kv, (1024, 512, 256, 128))
    scale = HEAD_DIM ** -0.5
    gain = jnp.concatenate([jnp.tile(q_gain * scale, d // HEAD_DIM), jnp.tile(k_gain, kv // HEAD_DIM),
                            jnp.ones((kv,), F32)]).reshape(1, n)
    cos, s1, s2 = _rope_tables(st.seq, tm)
    per_seq = st.seq // tm

    def tab_map(i, j):
        return (jnp.where(i * tm < st.n_lat, i % per_seq, per_seq), 0)

    tab = pl.BlockSpec((tm, HEAD_DIM), tab_map)
    return pl.pallas_call(
        functools.partial(_qkv_kernel, (d + kv) // tn),
        out_shape=jax.ShapeDtypeStruct((n_rows, n), BF16),
        grid=(n_rows // tm, n // tn),
        in_specs=[pl.BlockSpec((tm, d), lambda i, j: (i, 0)), pl.BlockSpec((d, tn), lambda i, j: (0, j)),
                  pl.BlockSpec((1, tn), lambda i, j: (0, j)), tab, tab, tab],
        out_specs=pl.BlockSpec((tm, tn), lambda i, j: (i, j)),
        compiler_params=_params("arbitrary", "arbitrary"),
        name="qkv_proj",
    )(h, w, gain, cos, s1, s2)


def _gelu_tanh(x):
    return 0.5 * x * (1.0 + jnp.tanh(0.7978845608028654 * (x + 0.044715 * (x * x * x))))


def _lru_in_kernel(n_gelu_tiles, a_ref, b_ref, o_ref):
    j = pl.program_id(1)
    acc = jnp.dot(a_ref[...], b_ref[...], preferred_element_type=F32)

    @pl.when(j < n_gelu_tiles)
    def _():
        o_ref[...] = _gelu_tanh(acc)

    @pl.when(j >= n_gelu_tiles)
    def _():
        o_ref[...] = acc


def _lru_in_proj(h, w):
    n_rows, d = h.shape
    n = w.shape[1]
    tm = _pick(n_rows, (1024, 512, 256, 128, 64))
    tn = _pick(n // 2, (1024, 512, 256, 128))
    return pl.pallas_call(
        functools.partial(_lru_in_kernel, (n // 2) // tn),
        out_shape=jax.ShapeDtypeStruct((n_rows, n), F32),
        grid=(n_rows // tm, n // tn),
        in_specs=[pl.BlockSpec((tm, d), lambda i, j: (i, 0)), pl.BlockSpec((d, tn), lambda i, j: (0, j))],
        out_specs=pl.BlockSpec((tm, tn), lambda i, j: (i, j)),
        compiler_params=_params("arbitrary", "arbitrary"),
        name="lru_in_proj",
    )(h, w)


def _swiglu_kernel(a_ref, wg_ref, wu_ref, o_ref):
    a = a_ref[...]
    gate = jnp.dot(a, wg_ref[0], preferred_element_type=F32)
    up = jnp.dot(a, wu_ref[0], preferred_element_type=F32)
    o_ref[...] = (gate * jax.nn.sigmoid(gate) * up).astype(o_ref.dtype)


def _swiglu_up(h, w, n_rows):
    e, d, f2 = w.shape
    f = f2 // 2
    tm = _pick(n_rows, (1024, 512, 256, 128, 64))
    tn = _pick(f, (512, 256, 128))
    nj = f // tn
    return pl.pallas_call(
        _swiglu_kernel,
        out_shape=jax.ShapeDtypeStruct((n_rows, e * f), BF16),
        grid=(n_rows // tm, e, nj),
        in_specs=[pl.BlockSpec((tm, d), lambda i, x, j: (i, 0)),
                  pl.BlockSpec((1, d, tn), lambda i, x, j: (x, 0, j)),
                  pl.BlockSpec((1, d, tn), lambda i, x, j: (x, 0, nj + j))],
        out_specs=pl.BlockSpec((tm, tn), lambda i, x, j: (i, x * nj + j)),
        compiler_params=_params("arbitrary", "arbitrary", "arbitrary"),
        name="swiglu_up",
    )(h, w, w)


def _deepnorm_epilogue(alpha, y, x_ref, gate_ref, lng_ref, lnb_ref, next_refs, xo_ref, ho_ref, rows=slice(None)):
    xn = _ln_rows(alpha * x_ref[rows, :] + gate_ref[0] * y, lng_ref[...], lnb_ref[...])
    xo_ref[rows, :] = xn
    if ho_ref is not None:
        shn_ref, scn_ref = next_refs
        ho_ref[rows, :] = (xn * (1.0 + scn_ref[0]) + shn_ref[0]).astype(BF16)


MM_LN_COL_CHUNK = 1024
MM_LN_ROW_CHUNK = 128


def _mm_ln_kernel(alpha, nk, has_next, *refs):
    a_ref, b_ref, x_ref, gate_ref, lng_ref, lnb_ref = refs[:6]
    next_refs = refs[6:8] if has_next else None
    xo_ref = refs[8] if has_next else refs[6]
    ho_ref = refs[9] if has_next else None
    k = pl.program_id(1)
    tm, d = xo_ref.shape

    @pl.when(k == 0)
    def _():
        xo_ref[...] = jnp.zeros_like(xo_ref)

    a = a_ref[...]
    cw = min(MM_LN_COL_CHUNK, d)
    for c in range(d // cw):
        cols = slice(c * cw, (c + 1) * cw)
        xo_ref[:, cols] += jnp.dot(a, b_ref[:, cols], preferred_element_type=F32)

    @pl.when(k == nk - 1)
    def _():
        rc = min(MM_LN_ROW_CHUNK, tm)

        def body(r, carry):
            rows = pl.ds(pl.multiple_of(r * rc, rc), rc)
            _deepnorm_epilogue(alpha, xo_ref[rows, :], x_ref, gate_ref, lng_ref, lnb_ref, next_refs, xo_ref, ho_ref,
                               rows)
            return carry

        lax.fori_loop(0, tm // rc, body, 0)


def _mm_ln(st, n_rows, alpha, a, b, x, mods, layer, gate_idx, lng, lnb, next_mod=None):
    kdim, d = b.shape
    tm = st.row_tile(512)
    tk = _pick(kdim, (1024, 512, 256, 128))
    nk = kdim // tk
    has_next = next_mod is not None
    once = pl.Buffered(1)
    in_specs = [pl.BlockSpec((tm, tk), lambda i, k: (i, k)), pl.BlockSpec((tk, d), lambda i, k: (k, 0)),
                pl.BlockSpec((tm, d), lambda i, k: (i, 0), pipeline_mode=once), st.mod_spec(layer, gate_idx, tm),
                _vec_spec(d), _vec_spec(d)]
    args = [a, b, x, mods, lng.reshape(1, d), lnb.reshape(1, d)]
    out_shape = [jax.ShapeDtypeStruct((n_rows, d), F32)]
    out_specs = [pl.BlockSpec((tm, d), lambda i, k: (i, 0))]
    if has_next:
        nl, n_sh, n_sc = next_mod
        in_specs += [st.mod_spec(nl, n_sh, tm), st.mod_spec(nl, n_sc, tm)]
        args += [mods, mods]
        out_shape.append(jax.ShapeDtypeStruct((n_rows, d), BF16))
        out_specs.append(pl.BlockSpec((tm, d), lambda i, k: (i, 0), pipeline_mode=once))
    res = pl.pallas_call(
        functools.partial(_mm_ln_kernel, alpha, nk, has_next),
        out_shape=tuple(out_shape),
        grid=(n_rows // tm, nk),
        in_specs=in_specs,
        out_specs=tuple(out_specs),
        compiler_params=_params("arbitrary", "arbitrary"),
        name="matmul_deepnorm",
    )(*args)
    return res if has_next else (res[0], None)


def _attn_kernel(n_lat_tiles, group, q_ref, kl_ref, vl_ref, kc_ref, vc_ref, o_ref, k_all, v_all):
    qi = pl.program_id(2)
    seq, ctx_len = kl_ref.shape[0], kc_ref.shape[0]

    @pl.when(qi == 0)
    def _():
        k_all[pl.ds(0, seq), :] = kl_ref[...]
        k_all[pl.ds(seq, ctx_len), :] = kc_ref[...]
        v_all[pl.ds(0, seq), pl.ds(0, HEAD_DIM)] = vl_ref[...]
        v_all[pl.ds(seq, ctx_len), pl.ds(0, HEAD_DIM)] = vc_ref[...]
        v_all[:, pl.ds(HEAD_DIM, HEAD_DIM)] = jnp.ones((seq + ctx_len, HEAD_DIM), v_all.dtype)

    def attend(k, v):
        for g in range(group):
            cols = slice(g * HEAD_DIM, (g + 1) * HEAD_DIM)
            s = lax.dot_general(q_ref[:, cols], k, (((1,), (1,)), ((), ())), preferred_element_type=F32)
            p = jnp.exp(s - jnp.max(s, axis=-1, keepdims=True)).astype(v.dtype)
            o = jnp.dot(p, v, preferred_element_type=F32)
            o_ref[:, cols] = (o[:, :HEAD_DIM] / o[:, HEAD_DIM:]).astype(o_ref.dtype)

    @pl.when(qi < n_lat_tiles)
    def _():
        attend(k_all[...], v_all[...])

    @pl.when(qi >= n_lat_tiles)
    def _():
        attend(k_all[pl.ds(seq, ctx_len), :], v_all[pl.ds(seq, ctx_len), :])


def _attention(st, qkv, d):
    n_rows, n = qkv.shape
    kv = (n - d) // 2
    n_kv = kv // HEAD_DIM
    group = d // kv
    tq = min(256, st.ctx_len)
    ctx_tiles = st.ctx_len // tq
    lat_tiles = st.seq // tq
    qw = group * HEAD_DIM
    k_col0 = d // HEAD_DIM
    v_col0 = (d + kv) // HEAD_DIM

    def q_map(b, h, qi):
        row = jnp.where(qi < lat_tiles, b * lat_tiles + qi, st.n_lat // tq + b * ctx_tiles + (qi - lat_tiles))
        return (row, h)

    ctx_blk0 = st.n_lat // st.ctx_len
    return pl.pallas_call(
        functools.partial(_attn_kernel, lat_tiles, group),
        out_shape=jax.ShapeDtypeStruct((n_rows, d), BF16),
        grid=(st.batch, n_kv, lat_tiles + ctx_tiles),
        in_specs=[pl.BlockSpec((tq, qw), q_map),
                  pl.BlockSpec((st.seq, HEAD_DIM), lambda b, h, qi: (b, k_col0 + h)),
                  pl.BlockSpec((st.seq, HEAD_DIM), lambda b, h, qi: (b, v_col0 + h)),
                  pl.BlockSpec((st.ctx_len, HEAD_DIM), lambda b, h, qi: (ctx_blk0 + b, k_col0 + h)),
                  pl.BlockSpec((st.ctx_len, HEAD_DIM), lambda b, h, qi: (ctx_blk0 + b, v_col0 + h))],
        out_specs=pl.BlockSpec((tq, qw), q_map),
        scratch_shapes=[pltpu.VMEM((st.seq + st.ctx_len, HEAD_DIM), qkv.dtype),
                        pltpu.VMEM((st.seq + st.ctx_len, 2 * HEAD_DIM), qkv.dtype)],
        compiler_params=_params("arbitrary", "arbitrary", "arbitrary"),
        name="gqa_attention",
    )(qkv, qkv, qkv, qkv, qkv)


def _scan8(a, b, reverse):
    row = lax.broadcasted_iota(jnp.int32, a.shape, 0)
    for s in (1, 2, 4):
        if reverse:
            keep = row < SUBLANES - s
            shift = SUBLANES - s
        else:
            keep = row >= s
            shift = s
        a_sh = jnp.where(keep, pltpu.roll(a, shift, 0), 1.0)
        b_sh = jnp.where(keep, pltpu.roll(b, shift, 0), 0.0)
        b = a * b_sh + b
        a = a * a_sh
    return a, b


def _lru_kernel(tm, lat_tiles, reverse, *refs):
    if reverse:
        (xp_ref, x_ref, xn_ref, cw_ref, cb_ref, gw_ref, gb_ref, lam_ref, recf_ref, gelu_ref,
         o_ref, ext_ref, a_ref, b_ref, carry_ref) = refs
    else:
        (xp_ref, x_ref, xn_ref, cw_ref, cb_ref, gw_ref, gb_ref, lam_ref,
         o_ref, ext_ref, a_ref, b_ref, carry_ref) = refs
    s = pl.program_id(1)
    is_ctx = s == 0
    tile = lat_tiles - s if reverse else s - 1
    first = jnp.logical_or(is_ctx, tile == 0)
    last = jnp.logical_or(is_ctx, tile == lat_tiles - 1)

    @pl.when(is_ctx)
    def _():
        carry_ref[...] = jnp.zeros_like(carry_ref)

    ext_ref[pl.ds(HALO, tm), :] = x_ref[...]
    ext_ref[pl.ds(0, HALO), :] = jnp.where(first, 0.0, xp_ref[...])
    ext_ref[pl.ds(HALO + tm, HALO), :] = jnp.where(last, 0.0, xn_ref[...])

    heads, bw, _ = gw_ref.shape
    lam = lam_ref[...]
    softplus = jnp.maximum(-lam, 0.0) + jnp.log1p(jnp.exp(-jnp.abs(lam)))
    for h in range(heads):
        cols = pl.ds(h * bw, bw)
        xr = cb_ref[:, cols] + ext_ref[pl.ds(HALO - CONV_LEFT, tm), cols] * cw_ref[0:1, cols]
        for k in range(1, CONV_W):
            xr = xr + ext_ref[pl.ds(HALO - CONV_LEFT + k, tm), cols] * cw_ref[k:k + 1, cols]
        g = jnp.dot(xr.astype(BF16), gw_ref[h], preferred_element_type=F32)
        r = jax.nn.sigmoid(g[:, :bw] + gb_ref[0:1, cols])
        i = jax.nn.sigmoid(g[:, bw:] + gb_ref[1:2, cols])
        log_a = -LRU_C * r * softplus[:, h * bw:(h + 1) * bw]
        a = jnp.exp(log_a)
        a_ref[:, cols] = a
        b_ref[:, cols] = jnp.sqrt(1.0 - a * a) * (i * xr)

    n_chunks = tm // SUBLANES

    def body(c, carry):
        c = n_chunks - 1 - c if reverse else c
        rows = pl.ds(pl.multiple_of(c * SUBLANES, SUBLANES), SUBLANES)
        a_cum, b_cum = _scan8(a_ref[rows, :], b_ref[rows, :], reverse)
        hs = a_cum * carry + b_cum
        if reverse:
            o_ref[rows, :] = (gelu_ref[rows, :] * (recf_ref[rows, :] + hs)).astype(o_ref.dtype)
            edge = hs[0:1, :]
        else:
            o_ref[rows, :] = hs
            edge = hs[SUBLANES - 1:SUBLANES, :]
        return jnp.broadcast_to(edge, carry.shape)

    carry_ref[...] = lax.fori_loop(0, n_chunks, body, carry_ref[...])


def _lru_scan(st, yx, conv_w, conv_b, gate_w, gate_b, lam, reverse, rec_f=None):
    n_rows = yx.shape[0]
    d = st.d
    tm = min(256, st.ctx_len)
    assert st.ctx_len == tm
    lat_tiles = st.seq // tm
    ctx_blk0 = st.n_lat // tm
    per = tm // HALO
    last_halo = n_rows // HALO - 1

    def blk(b, s):
        lat = b * lat_tiles + (lat_tiles - s if reverse else s - 1)
        return jnp.where(s == 0, ctx_blk0 + b, lat)

    heads = gate_w.shape[1]
    bw = d // heads
    gw = jnp.concatenate([gate_w[0], gate_w[1]], axis=-1).astype(BF16)
    in_specs = [pl.BlockSpec((HALO, d), lambda b, s: (jnp.maximum(blk(b, s) * per - 1, 0), 1)),
                pl.BlockSpec((tm, d), lambda b, s: (blk(b, s), 1)),
                pl.BlockSpec((HALO, d), lambda b, s: (jnp.minimum((blk(b, s) + 1) * per, last_halo), 1)),
                pl.BlockSpec((CONV_W, d), lambda b, s: (0, 0)), _vec_spec(d),
                pl.BlockSpec((heads, bw, 2 * bw), lambda b, s: (0, 0, 0)),
                pl.BlockSpec((2, d), lambda b, s: (0, 0)), _vec_spec(d)]
    args = [yx, yx, yx, conv_w, conv_b.reshape(1, d), gw, gate_b.reshape(2, d), lam.reshape(1, d)]
    if reverse:
        in_specs += [pl.BlockSpec((tm, d), lambda b, s: (blk(b, s), 0)),
                     pl.BlockSpec((tm, d), lambda b, s: (blk(b, s), 0))]
        args += [rec_f, yx]
    return pl.pallas_call(
        functools.partial(_lru_kernel, tm, lat_tiles, reverse),
        out_shape=jax.ShapeDtypeStruct((n_rows, d), BF16 if reverse else F32),
        grid=(st.batch, lat_tiles + 1),
        in_specs=in_specs,
        out_specs=pl.BlockSpec((tm, d), lambda b, s: (blk(b, s), 0)),
        scratch_shapes=[pltpu.VMEM((tm + 2 * HALO, d), F32), pltpu.VMEM((tm, d), F32),
                        pltpu.VMEM((tm, d), F32), pltpu.VMEM((SUBLANES, d), F32)],
        compiler_params=_params("arbitrary", "arbitrary"),
        name="rglru_reverse" if reverse else "rglru_forward",
    )(*args)


ROUTE_LANES = 8
MOE_ROW_TILE = 512


def _lane_pick(rec, k):
    lane = lax.broadcasted_iota(jnp.int32, rec.shape, 1)
    return jnp.sum(jnp.where(lane == k, rec, 0.0), axis=1, keepdims=True)


def _router_kernel(x_ref, sh_ref, sc_ref, w_ref, route_ref, cnt_ref, run_ref):
    @pl.when(pl.program_id(0) == 0)
    def _():
        run_ref[...] = jnp.zeros_like(run_ref)

    h = x_ref[...] * (1.0 + sc_ref[0]) + sh_ref[0]
    logits = jnp.dot(h, w_ref[...], preferred_element_type=F32, precision=HIGHEST)
    tm, n_exp = logits.shape
    lane = lax.broadcasted_iota(jnp.int32, logits.shape, 1).astype(F32)
    m1 = jnp.max(logits, axis=-1, keepdims=True)
    i1 = jnp.min(jnp.where(logits == m1, lane, float(n_exp)), axis=-1, keepdims=True)
    pick1 = lane == i1
    rest = jnp.where(pick1, -jnp.inf, logits)
    m2 = jnp.max(rest, axis=-1, keepdims=True)
    i2 = jnp.min(jnp.where(rest == m2, lane, float(n_exp)), axis=-1, keepdims=True)
    pick2 = lane == i2
    e2 = jnp.exp(m2 - m1)
    den = 1.0 + e2
    onehot = jnp.where(pick1, 1.0, jnp.where(pick2, 1.0, 0.0))
    earlier = (lax.broadcasted_iota(jnp.int32, (tm, tm), 0) > lax.broadcasted_iota(jnp.int32, (tm, tm), 1))
    before = jnp.dot(jnp.where(earlier, 1.0, 0.0).astype(BF16), onehot.astype(BF16),
                     preferred_element_type=F32) + run_ref[...]
    rank1 = jnp.sum(jnp.where(pick1, before, 0.0), axis=-1, keepdims=True)
    rank2 = jnp.sum(jnp.where(pick2, before, 0.0), axis=-1, keepdims=True)
    rec = jnp.zeros((tm, ROUTE_LANES), F32)
    for k, v in enumerate((i1, i2, 1.0 / den, e2 / den, rank1, rank2)):
        rec = jnp.where(lax.broadcasted_iota(jnp.int32, rec.shape, 1) == k, v, rec)
    route_ref[...] = rec
    run_ref[...] += jnp.sum(onehot, axis=0, keepdims=True)
    cnt_ref[...] = run_ref[...]


def _router(st, n_rows, x, mods, layer, w):
    d, n_exp = w.shape
    tm = min(256, st.ctx_len)
    return pl.pallas_call(
        _router_kernel,
        out_shape=(jax.ShapeDtypeStruct((n_rows, ROUTE_LANES), F32), jax.ShapeDtypeStruct((1, n_exp), F32)),
        grid=(n_rows // tm,),
        in_specs=[pl.BlockSpec((tm, d), lambda i: (i, 0)), st.mod_spec(layer, 3, tm), st.mod_spec(layer, 4, tm),
                  pl.BlockSpec((d, n_exp), lambda i: (0, 0))],
        out_specs=(pl.BlockSpec((tm, ROUTE_LANES), lambda i: (i, 0)), pl.BlockSpec((1, n_exp), lambda i: (0, 0))),
        scratch_shapes=[pltpu.VMEM((1, n_exp), F32)],
        compiler_params=_params("arbitrary"),
        name="moe_router",
    )(x, mods, mods, w)


def _dispatch_plan(route, counts, tm):
    n_tok = route.shape[0]
    n_exp = counts.shape[1]
    expert = route[:, 0:2].astype(jnp.int32)
    rank = route[:, 4:6].astype(jnp.int32)
    cnt = counts[0].astype(jnp.int32)
    padded = (cnt + tm - 1) // tm * tm
    ends = jnp.cumsum(padded)
    starts = ends - padded
    dest = (starts[expert] + rank).T.reshape(2 * n_tok)
    n_tiles = (2 * n_tok + n_exp * tm) // tm
    tile_row0 = jnp.arange(n_tiles, dtype=jnp.int32) * tm
    tile_expert = jnp.minimum(jnp.sum(tile_row0[:, None] >= ends[None, :], axis=1), n_exp - 1).astype(jnp.int32)
    n_used = (ends[-1:] // tm).astype(jnp.int32)
    return dest, tile_expert, n_used, n_tiles


def _dispatch_kernel(tm, n_tok, dest_ref, x_ref, sh_ref, sc_ref, init_ref, o_ref, h_ref, sem):
    del init_ref
    base = pl.program_id(0) * tm
    h_ref[...] = x_ref[...] * (1.0 + sc_ref[0]) + sh_ref[0]

    def row_copy(r, d):
        return pltpu.make_async_copy(h_ref.at[pl.ds(r, 1), :], o_ref.at[pl.ds(d, 1), :], sem)

    def issue(r, carry):
        row_copy(r, dest_ref[base + r]).start()
        row_copy(r, dest_ref[n_tok + base + r]).start()
        return carry

    def drain(r, carry):
        row_copy(r, 0).wait()
        row_copy(r, 0).wait()
        return carry

    lax.fori_loop(0, tm, issue, 0)
    lax.fori_loop(0, tm, drain, 0)


def _dispatch(st, n_rows, x, mods, layer, dest, n_out):
    d = st.d
    tm = min(256, st.ctx_len)
    grid_spec = pltpu.PrefetchScalarGridSpec(
        num_scalar_prefetch=1,
        grid=(n_rows // tm,),
        in_specs=[pl.BlockSpec((tm, d), lambda i, dest: (i, 0)), st.mod_spec(layer, 3, tm), st.mod_spec(layer, 4, tm),
                  pl.BlockSpec(memory_space=pl.ANY)],
        out_specs=pl.BlockSpec(memory_space=pl.ANY),
        scratch_shapes=[pltpu.VMEM((tm, d), F32), pltpu.SemaphoreType.DMA],
    )
    return pl.pallas_call(
        functools.partial(_dispatch_kernel, tm, n_rows),
        out_shape=jax.ShapeDtypeStruct((n_out, d), F32),
        grid_spec=grid_spec,
        input_output_aliases={4: 0},
        compiler_params=_params("arbitrary"),
        name="moe_dispatch",
    )(dest, x, mods, mods, jnp.zeros((n_out, d), F32))


def _moe_up_kernel(te_ref, nu_ref, a_ref, wg_ref, wu_ref, o_ref, abf_ref):
    i, j = pl.program_id(0), pl.program_id(1)

    @pl.when(i < nu_ref[0])
    def _():
        @pl.when(j == 0)
        def _():
            abf_ref[...] = a_ref[...].astype(BF16)

        a = abf_ref[...]
        gate = jnp.dot(a, wg_ref[0], preferred_element_type=F32)
        up = jnp.dot(a, wu_ref[0], preferred_element_type=F32)
        o_ref[...] = (gate * jax.nn.sigmoid(gate) * up).astype(o_ref.dtype)

    @pl.when(i >= nu_ref[0])
    def _():
        o_ref[...] = jnp.zeros_like(o_ref)


def _moe_up(xg, w, tile_expert, n_used, tm):
    n_out, d = xg.shape
    f = w.shape[2] // 2
    tn = _pick(f, (512, 256, 128))
    nj = f // tn
    grid_spec = pltpu.PrefetchScalarGridSpec(
        num_scalar_prefetch=2,
        grid=(n_out // tm, nj),
        in_specs=[pl.BlockSpec((tm, d), lambda i, j, te, nu: (i, 0)),
                  pl.BlockSpec((1, d, tn), lambda i, j, te, nu: (te[i], 0, j)),
                  pl.BlockSpec((1, d, tn), lambda i, j, te, nu: (te[i], 0, nj + j))],
        out_specs=pl.BlockSpec((tm, tn), lambda i, j, te, nu: (i, j)),
        scratch_shapes=[pltpu.VMEM((tm, d), BF16)],
    )
    return pl.pallas_call(
        _moe_up_kernel,
        out_shape=jax.ShapeDtypeStruct((n_out, f), BF16),
        grid_spec=grid_spec,
        compiler_params=_params("arbitrary", "arbitrary"),
        name="moe_up",
    )(tile_expert, n_used, xg, w, w)


def _moe_down_kernel(te_ref, nu_ref, a_ref, w_ref, o_ref):
    i = pl.program_id(0)

    @pl.when(i < nu_ref[0])
    def _():
        o_ref[...] = jnp.dot(a_ref[...], w_ref[0], preferred_element_type=F32)

    @pl.when(i >= nu_ref[0])
    def _():
        o_ref[...] = jnp.zeros_like(o_ref)


def _moe_down(act, w, tile_expert, n_used, tm):
    n_out, f = act.shape
    d = w.shape[2]
    tn = _pick(d, (2048, 1024, 512, 256, 128))
    grid_spec = pltpu.PrefetchScalarGridSpec(
        num_scalar_prefetch=2,
        grid=(n_out // tm, d // tn),
        in_specs=[pl.BlockSpec((tm, f), lambda i, j, te, nu: (i, 0)),
                  pl.BlockSpec((1, f, tn), lambda i, j, te, nu: (te[i], 0, j))],
        out_specs=pl.BlockSpec((tm, tn), lambda i, j, te, nu: (i, j)),
    )
    return pl.pallas_call(
        _moe_down_kernel,
        out_shape=jax.ShapeDtypeStruct((n_out, d), F32),
        grid_spec=grid_spec,
        compiler_params=_params("arbitrary", "arbitrary"),
        name="moe_down",
    )(tile_expert, n_used, act, w)


def _combine_kernel(tm, n_tok, alpha, has_next, dest_ref, *refs):
    route_ref, x_ref, gate_ref, lng_ref, lnb_ref = refs[:5]
    next_refs = refs[5:7] if has_next else None
    refs = refs[7:] if has_next else refs[5:]
    y_ref, xo_ref = refs[:2]
    ho_ref = refs[2] if has_next else None
    buf_ref, sem = refs[-2:]
    base = pl.program_id(0) * tm

    def row_copy(slot, r, src):
        return pltpu.make_async_copy(y_ref.at[pl.ds(src, 1), :], buf_ref.at[slot, pl.ds(r, 1), :], sem)

    def issue(r, carry):
        row_copy(0, r, dest_ref[base + r]).start()
        row_copy(1, r, dest_ref[n_tok + base + r]).start()
        return carry

    def drain(r, carry):
        row_copy(0, r, 0).wait()
        row_copy(1, r, 0).wait()
        return carry

    lax.fori_loop(0, tm, issue, 0)
    lax.fori_loop(0, tm, drain, 0)
    rec = route_ref[...]
    y = _lane_pick(rec, 2) * buf_ref[0] + _lane_pick(rec, 3) * buf_ref[1]
    _deepnorm_epilogue(alpha, y, x_ref, gate_ref, lng_ref, lnb_ref, next_refs, xo_ref, ho_ref)


def _combine(st, n_rows, alpha, yg, route, dest, x, mods, layer, lng, lnb, next_mod=None):
    d = st.d
    tm = min(256, st.ctx_len)
    has_next = next_mod is not None
    row = lambda i, dest: (i, 0)
    in_specs = [pl.BlockSpec((tm, ROUTE_LANES), row), pl.BlockSpec((tm, d), row), st.mod_spec(layer, 5, tm),
                _vec_spec(d), _vec_spec(d)]
    args = [route, x, mods, lng.reshape(1, d), lnb.reshape(1, d)]
    out_shape = [jax.ShapeDtypeStruct((n_rows, d), F32)]
    out_specs = [pl.BlockSpec((tm, d), row)]
    if has_next:
        nl, n_sh, n_sc = next_mod
        in_specs += [st.mod_spec(nl, n_sh, tm), st.mod_spec(nl, n_sc, tm)]
        args += [mods, mods]
        out_shape.append(jax.ShapeDtypeStruct((n_rows, d), BF16))
        out_specs.append(pl.BlockSpec((tm, d), row))
    in_specs.append(pl.BlockSpec(memory_space=pl.ANY))
    args.append(yg)
    grid_spec = pltpu.PrefetchScalarGridSpec(
        num_scalar_prefetch=1,
        grid=(n_rows // tm,),
        in_specs=in_specs,
        out_specs=tuple(out_specs),
        scratch_shapes=[pltpu.VMEM((2, tm, d), F32), pltpu.SemaphoreType.DMA],
    )
    res = pl.pallas_call(
        functools.partial(_combine_kernel, tm, n_rows, alpha, has_next),
        out_shape=tuple(out_shape),
        grid_spec=grid_spec,
        compiler_params=_params("arbitrary"),
        name="moe_combine",
    )(dest, *args)
    return res if has_next else (res[0], None)


def _moe_layer(st, n_rows, alpha, x, mods, layer, router_w, w_in, w_out, lng, lnb, next_mod):
    route, counts = _router(st, n_rows, x, mods, layer, router_w)
    dest, tile_expert, n_used, n_tiles = _dispatch_plan(route, counts, MOE_ROW_TILE)
    xg = _dispatch(st, n_rows, x, mods, layer, dest, n_tiles * MOE_ROW_TILE)
    act = _moe_up(xg, w_in, tile_expert, n_used, MOE_ROW_TILE)
    yg = _moe_down(act, w_out, tile_expert, n_used, MOE_ROW_TILE)
    return _combine(st, n_rows, alpha, yg, route, dest, x, mods, layer, lng, lnb, next_mod)


def _cast_kernel(is_pad, x_ref, o_ref):
    pad = is_pad(pl.program_id(0), pl.program_id(1))

    @pl.when(jnp.logical_not(pad))
    def _():
        o_ref[...] = x_ref[...].astype(o_ref.dtype)

    @pl.when(pad)
    def _():
        o_ref[...] = jnp.zeros_like(o_ref)


def _cast_blocks(w, layer, block, out_blocks, src_of, n_src):
    def in_map(i, j):
        bi, bj = src_of(i, j)
        return (layer, jnp.minimum(bi, n_src[0] - 1), jnp.minimum(bj, n_src[1] - 1))

    def is_pad(i, j):
        bi, bj = src_of(i, j)
        return jnp.logical_or(bi >= n_src[0], bj >= n_src[1])

    return pl.pallas_call(
        functools.partial(_cast_kernel, is_pad),
        out_shape=jax.ShapeDtypeStruct((out_blocks[0] * block[0], out_blocks[1] * block[1]), BF16),
        grid=out_blocks,
        in_specs=[pl.BlockSpec((None,) + block, in_map)],
        out_specs=pl.BlockSpec(block, lambda i, j: (i, j)),
        compiler_params=_params("arbitrary", "arbitrary"),
        name="cast_weights",
    )(w)


def _cast(w, layer):
    w3 = w.reshape(w.shape[0], -1, w.shape[-1])
    rows, cols = w3.shape[1:]
    block = (_pick(rows, (512, 256, 128, 64, 32, 16)), _pick(cols, (4096, 2048, 1024, 512, 256, 128)))
    n = (rows // block[0], cols // block[1])
    return _cast_blocks(w3, layer, block, n, lambda i, j: (i, j), n).reshape(w.shape[1:])


FFN_PAD_BLOCK = 256


def _cast_pad_ffn(w_in, w_out, layer, mult):
    d, f2 = w_in.shape[1:]
    f = f2 // 2
    blk = _pick(f, (FFN_PAD_BLOCK, 128))
    fp = _round_up(f, mult)
    nb, nbp = f // blk, fp // blk
    tr = _pick(d, (2048, 1024, 512, 256, 128))

    def in_src(i, j):
        up = j - nbp
        return (i, jnp.where(j < nbp, jnp.where(j < nb, j, 2 * nb), jnp.where(up < nb, up + nb, 2 * nb)))

    w_in_p = _cast_blocks(w_in, layer, (tr, blk), (d // tr, 2 * nbp), in_src, (d // tr, 2 * nb))
    dc = _pick(d, (4096, 2048, 1024, 512, 256, 128))
    w_out_p = _cast_blocks(w_out, layer, (blk, dc), (nbp, d // dc), lambda i, j: (i, j), (nb, d // dc))
    return w_in_p.reshape(1, d, 2 * fp), w_out_p


def kernel(x, c, ctx, c_ctx, ada_down, ada_up, ada_b, ln_g, ln_b, pool_w, pool_scale, attn_wqkv, attn_q_gain,
           attn_k_gain, attn_wo, lru_w_in, lru_conv_w, lru_conv_b, lru_gate_w, lru_gate_b, lru_lambda, lru_w_out,
           ffn_w_in, ffn_w_out, moe_router, moe_w_in, moe_w_out):
    batch, seq, d = x.shape
    ctx_len = ctx.shape[1]
    depth = ada_down.shape[0]
    assert batch < MOD_ROWS and seq % ctx_len == 0 and ctx_len % SUBLANES == 0
    st = _Stream(batch, seq, ctx_len, d)
    alpha = (2.0 * depth) ** 0.25
    mixers = [i % N_MIXERS for i in range(depth)]
    is_moe = [i % 2 == 1 for i in range(depth)]
    ctx_needed_after = [any(mixers[j] != 0 for j in range(i + 1, depth)) for i in range(depth)]

    cvec = jnp.concatenate([c, c_ctx[None], jnp.zeros((MOD_ROWS - batch - 1, d), F32)], axis=0)
    mods = _adaln(cvec, ada_down, ada_up, ada_b)

    xs = jnp.concatenate([x.reshape(batch * seq, d), ctx.reshape(batch * ctx_len, d)], axis=0)
    n_rows = st.n_all
    h = None
    for i in range(depth):
        mixer = mixers[i]
        mi = mixers[:i].count(mixer)
        fi = is_moe[:i].count(is_moe[i])
        lng, lnb = ln_g[i], ln_b[i]
        ffn_mod = None if is_moe[i] else (i, 3, 4)
        if not ctx_needed_after[i] and mixer == 0 and n_rows != st.n_lat:
            n_rows = st.n_lat
            xs = xs[:n_rows]

        if mixer == 0:
            xs, hf = _pool_layer(st, n_rows, i, alpha, xs, mods, _cast(pool_w, mi), pool_scale[mi],
                                 lng[0], lnb[0])
        elif mixer == 1:
            qkv = _qkv_proj(st, h, _cast(attn_wqkv, mi), attn_q_gain[mi], attn_k_gain[mi])
            o = _attention(st, qkv, d)
            rows_out = n_rows if ctx_needed_after[i] else st.n_lat
            xs, hf = _mm_ln(st, rows_out, alpha, o, _cast(attn_wo, mi), xs, mods, i, 2, lng[0], lnb[0],
                            next_mod=ffn_mod)
            n_rows = rows_out
        else:
            yx = _lru_in_proj(h, _cast(lru_w_in, mi))
            rec_f = _lru_scan(st, yx, lru_conv_w[mi], lru_conv_b[mi], lru_gate_w[mi, 0], lru_gate_b[mi, 0],
                              lru_lambda[mi, 0], False)
            m = _lru_scan(st, yx, lru_conv_w[mi], lru_conv_b[mi], lru_gate_w[mi, 1], lru_gate_b[mi, 1],
                          lru_lambda[mi, 1], True, rec_f)
            rows_out = n_rows if ctx_needed_after[i] else st.n_lat
            xs, hf = _mm_ln(st, rows_out, alpha, m, _cast(lru_w_out, mi), xs, mods, i, 2, lng[0], lnb[0],
                            next_mod=ffn_mod)
            n_rows = rows_out

        next_mod = (i + 1, 0, 1) if i + 1 < depth and mixers[i + 1] != 0 else None
        if is_moe[i]:
            xs, h = _moe_layer(st, n_rows, alpha, xs, mods, i, moe_router[fi], _cast(moe_w_in, fi),
                               _cast(moe_w_out, fi), lng[1], lnb[1], next_mod)
        else:
            w_in_p, w_out_p = _cast_pad_ffn(ffn_w_in, ffn_w_out, fi, 1024)
            act = _swiglu_up(hf, w_in_p, n_rows)
            xs, h = _mm_ln(st, n_rows, alpha, act, w_out_p, xs, mods, i, 5, lng[1], lnb[1], next_mod=next_mod)
    return xs[:st.n_lat].reshape(batch, seq, d)
```

```python
import functools

import jax
import jax.numpy as jnp
from jax import lax
from jax.experimental import pallas as pl
from jax.experimental.pallas import tpu as pltpu

F32 = jnp.float32
BF16 = jnp.bfloat16
HIGHEST = lax.Precision.HIGHEST

HEAD_DIM = 128
AXIS_DIM = HEAD_DIM // 2
GRID_W = 64
ROPE_THETA = 10000.0
LN_EPS = 1e-6
RMS_EPS = 1e-6
LRU_C = 8.0
CONV_W = 4
CONV_LEFT = CONV_W // 2
POOL_WINDOWS = (2, 4, 8, 16)
N_MOD = 6
N_MIXERS = 3
MOD_ROWS = 8

SUBLANES = 8
LANES = 128
VMEM_LIMIT_BYTES = 56 * 1024 * 1024
HALO = SUBLANES


def _params(*sem):
    return pltpu.CompilerParams(dimension_semantics=sem, vmem_limit_bytes=VMEM_LIMIT_BYTES)


def _pick(n, prefs):
    for p in prefs:
        if n % p == 0:
            return p
    return n


def _round_up(n, m):
    return (n + m - 1) // m * m


def _ln_rows(z, g, b):
    mu = jnp.mean(z, axis=-1, keepdims=True)
    zc = z - mu
    var = jnp.mean(zc * zc, axis=-1, keepdims=True)
    return zc * lax.rsqrt(var + LN_EPS) * g + b


class _Stream:
    def __init__(self, batch, seq, ctx_len, d):
        self.batch, self.seq, self.ctx_len, self.d = batch, seq, ctx_len, d
        self.n_lat = batch * seq
        self.n_all = self.n_lat + batch * ctx_len

    def row_tile(self, largest):
        t = largest
        while self.seq % t or (self.batch * self.ctx_len) % t:
            t //= 2
        return t

    def group(self, row0):
        return jnp.minimum(row0 // self.seq, self.batch)

    def mod_spec(self, layer, which, tm):
        def imap(i, *_):
            return ((layer * MOD_ROWS + self.group(i * tm)) * N_MOD + which, 0, 0)
        return pl.BlockSpec((1, 1, self.d), imap)

    def seq_pos(self, row0):
        is_ctx = row0 >= self.n_lat
        pos = jnp.where(is_ctx, (row0 - self.n_lat) % self.ctx_len, row0 % self.seq)
        return pos, jnp.where(is_ctx, self.ctx_len, self.seq)


def _vec_spec(d):
    return pl.BlockSpec((1, d), lambda *_: (0, 0))


def _halo_specs(tm, width, n_rows, col_block=0):
    per = tm // HALO
    last = n_rows // HALO - 1
    prev = pl.BlockSpec((HALO, width), lambda i, *_: (jnp.maximum(i * per - 1, 0), col_block))
    cur = pl.BlockSpec((tm, width), lambda i, *_: (i, col_block))
    nxt = pl.BlockSpec((HALO, width), lambda i, *_: (jnp.minimum((i + 1) * per, last), col_block))
    return prev, cur, nxt


def _adaln_kernel(c_ref, down_ref, up_ref, b_ref, o_ref):
    c = c_ref[...]
    s = c * jax.nn.sigmoid(c)
    t = jnp.dot(s, down_ref[0], preferred_element_type=F32, precision=HIGHEST)
    o_ref[0] = jnp.dot(t, up_ref[0], preferred_element_type=F32, precision=HIGHEST) + b_ref[0]


def _adaln(cvec, ada_down, ada_up, ada_b):
    depth, d, rank = ada_down.shape
    n = ada_up.shape[-1]
    tn = _pick(n, (4096, 2048, 1024, 512, 256, 128))
    out = pl.pallas_call(
        _adaln_kernel,
        out_shape=jax.ShapeDtypeStruct((depth, MOD_ROWS, n), F32),
        grid=(depth, n // tn),
        in_specs=[
            pl.BlockSpec((MOD_ROWS, d), lambda l, j: (0, 0)),
            pl.BlockSpec((1, d, rank), lambda l, j: (l, 0, 0)),
            pl.BlockSpec((1, rank, tn), lambda l, j: (l, 0, j)),
            pl.BlockSpec((1, 1, tn), lambda l, j: (l, 0, j)),
        ],
        out_specs=pl.BlockSpec((1, MOD_ROWS, tn), lambda l, j: (l, 0, j)),
        compiler_params=_params("arbitrary", "arbitrary"),
        name="adaln",
    )(cvec, ada_down, ada_up, ada_b.reshape(depth, 1, n))
    return out.reshape(depth * MOD_ROWS * N_MOD, 1, d)


def _pool_kernel(st, tm, alpha, xp_ref, x_ref, xn_ref, sh_ref, sc_ref, gm_ref, shf_ref, scf_ref,
                 w_ref, ps_ref, lng_ref, lnb_ref, xo_ref, ho_ref, ext_ref, z_ref):
    row0 = pl.program_id(0) * tm
    pos0, seqlen = st.seq_pos(row0)
    first = pos0 == 0
    last = pos0 + tm == seqlen
    one_sc = 1.0 + sc_ref[0]
    sh = sh_ref[0]
    x = x_ref[...]
    ext_ref[pl.ds(HALO, tm), :] = x * one_sc + sh
    ext_ref[pl.ds(0, HALO), :] = jnp.where(first, 0.0, xp_ref[...] * one_sc + sh)
    ext_ref[pl.ds(HALO + tm, HALO), :] = jnp.where(last, 0.0, xn_ref[...] * one_sc + sh)

    pos = pos0 + lax.broadcasted_iota(jnp.int32, (tm, 1), 0)
    groups = len(POOL_WINDOWS)
    gw = st.d // groups
    for g, k in enumerate(POOL_WINDOWS):
        cols = pl.ds(g * gw, gw)
        lo = jnp.maximum(pos - k // 2, 0)
        hi = jnp.minimum(pos + (k - k // 2), seqlen)
        cnt = (hi - lo).astype(F32)
        win = ext_ref[pl.ds(HALO - k // 2, tm), cols]
        for j in range(1, k):
            win = win + ext_ref[pl.ds(HALO - k // 2 + j, tm), cols]
        diff = (win / cnt - ext_ref[pl.ds(HALO, tm), cols]).astype(BF16)
        y = jnp.dot(diff, w_ref[g], preferred_element_type=F32) * ps_ref[:, cols]
        z_ref[:, cols] = alpha * x_ref[:, cols] + gm_ref[0, :, cols] * y
    xn = _ln_rows(z_ref[...], lng_ref[...], lnb_ref[...])
    xo_ref[...] = xn
    ho_ref[...] = (xn * (1.0 + scf_ref[0]) + shf_ref[0]).astype(BF16)


def _pool_layer(st, n_rows, layer, alpha, x, mods, w, pscale, lng, lnb):
    d = st.d
    tm = min(256, st.ctx_len)
    prev, cur, nxt = _halo_specs(tm, d, n_rows)
    groups, gw, _ = w.shape
    return pl.pallas_call(
        functools.partial(_pool_kernel, st, tm, alpha),
        out_shape=(jax.ShapeDtypeStruct((n_rows, d), F32), jax.ShapeDtypeStruct((n_rows, d), BF16)),
        grid=(n_rows // tm,),
        in_specs=[prev, cur, nxt] + [st.mod_spec(layer, q, tm) for q in (0, 1, 2, 3, 4)] + [
            pl.BlockSpec((groups, gw, gw), lambda i: (0, 0, 0)),
            _vec_spec(d), _vec_spec(d), _vec_spec(d)],
        out_specs=(pl.BlockSpec((tm, d), lambda i: (i, 0)), pl.BlockSpec((tm, d), lambda i: (i, 0))),
        scratch_shapes=[pltpu.VMEM((tm + 2 * HALO, d), F32), pltpu.VMEM((tm, d), F32)],
        compiler_params=_params("arbitrary"),
        name="pool_mixer",
    )(x, x, x, mods, mods, mods, mods, mods, w, pscale.reshape(1, d), lng.reshape(1, d), lnb.reshape(1, d))


def _qkv_kernel(n_qk_tiles, a_ref, b_ref, gain_ref, c_ref, s1_ref, s2_ref, o_ref):
    j = pl.program_id(1)
    acc = jnp.dot(a_ref[...], b_ref[...], preferred_element_type=F32)
    tn = acc.shape[1]

    @pl.when(j < n_qk_tiles)
    def _():
        cos, s1, s2 = c_ref[...], s1_ref[...], s2_ref[...]
        for hh in range(tn // HEAD_DIM):
            cols = slice(hh * HEAD_DIM, (hh + 1) * HEAD_DIM)
            xh = acc[:, cols]
            n = xh * lax.rsqrt(jnp.mean(xh * xh, axis=-1, keepdims=True) + RMS_EPS) * gain_ref[:, cols]
            half = AXIS_DIM // 2
            rot = n * cos + pltpu.roll(n, HEAD_DIM - half, 1) * s1 + pltpu.roll(n, half, 1) * s2
            o_ref[:, cols] = rot.astype(o_ref.dtype)

    @pl.when(j >= n_qk_tiles)
    def _():
        o_ref[...] = acc.astype(o_ref.dtype)


def _rope_tables(seq, tm):
    rows = seq // GRID_W
    r = jnp.repeat(jnp.arange(rows, dtype=F32), GRID_W)
    col = jnp.tile(jnp.arange(GRID_W, dtype=F32), rows)
    inv = ROPE_THETA ** (-jnp.arange(0, AXIS_DIM, 2, dtype=F32) / AXIS_DIM)
    ang_r = r[:, None] * inv
    ang_c = col[:, None] * inv
    zero = jnp.zeros_like(ang_r)
    cos = jnp.concatenate([jnp.cos(ang_r), jnp.cos(ang_r), jnp.cos(ang_c), jnp.cos(ang_c)], axis=-1)
    s1 = jnp.concatenate([-jnp.sin(ang_r), zero, -jnp.sin(ang_c), zero], axis=-1)
    s2 = jnp.concatenate([zero, jnp.sin(ang_r), zero, jnp.sin(ang_c)], axis=-1)
    ident = jnp.ones((tm, HEAD_DIM), F32)
    nul = jnp.zeros((tm, HEAD_DIM), F32)
    return (jnp.concatenate([cos, ident]), jnp.concatenate([s1, nul]), jnp.concatenate([s2, nul]))


def _qkv_proj(st, h, w, q_gain, k_gain):
    n_rows, d = h.shape
    n = w.shape[1]
    kv = (n - d) // 2
    tm = st.row_tile(1024)
    tn = _pick(kv, (1024, 512, 256, 128))
    scale = HEAD_DIM ** -0.5
    gain = jnp.concatenate([jnp.tile(q_gain * scale, d // HEAD_DIM), jnp.tile(k_gain, kv // HEAD_DIM),
                            jnp.ones((kv,), F32)]).reshape(1, n)
    cos, s1, s2 = _rope_tables(st.seq, tm)
    per_seq = st.seq // tm

    def tab_map(i, j):
        return (jnp.where(i * tm < st.n_lat, i % per_seq, per_seq), 0)

    tab = pl.BlockSpec((tm, HEAD_DIM), tab_map)
    return pl.pallas_call(
        functools.partial(_qkv_kernel, (d + kv) // tn),
        out_shape=jax.ShapeDtypeStruct((n_rows, n), BF16),
        grid=(n_rows // tm, n // tn),
        in_specs=[pl.BlockSpec((tm, d), lambda i, j: (i, 0)), pl.BlockSpec((d, tn), lambda i, j: (0, j)),
                  pl.BlockSpec((1, tn), lambda i, j: (0, j)), tab, tab, tab],
        out_specs=pl.BlockSpec((tm, tn), lambda i, j: (i, j)),
        compiler_params=_params("arbitrary", "arbitrary"),
        name="qkv_proj",
    )(h, w, gain, cos, s1, s2)


def _gelu_tanh(x):
    return 0.5 * x * (1.0 + jnp.tanh(0.7978845608028654 * (x + 0.044715 * (x * x * x))))


def _lru_in_kernel(n_gelu_tiles, a_ref, b_ref, o_ref):
    j = pl.program_id(1)
    acc = jnp.dot(a_ref[...], b_ref[...], preferred_element_type=F32)

    @pl.when(j < n_gelu_tiles)
    def _():
        o_ref[...] = _gelu_tanh(acc)

    @pl.when(j >= n_gelu_tiles)
    def _():
        o_ref[...] = acc


def _lru_in_proj(h, w):
    n_rows, d = h.shape
    n = w.shape[1]
    tm = _pick(n_rows, (1024, 512, 256, 128, 64))
    tn = _pick(n // 2, (1024, 512, 256, 128))
    return pl.pallas_call(
        functools.partial(_lru_in_kernel, (n // 2) // tn),
        out_shape=jax.ShapeDtypeStruct((n_rows, n), F32),
        grid=(n_rows // tm, n // tn),
        in_specs=[pl.BlockSpec((tm, d), lambda i, j: (i, 0)), pl.BlockSpec((d, tn), lambda i, j: (0, j))],
        out_specs=pl.BlockSpec((tm, tn), lambda i, j: (i, j)),
        compiler_params=_params("arbitrary", "arbitrary"),
        name="lru_in_proj",
    )(h, w)


def _swiglu_kernel(a_ref, wg_ref, wu_ref, o_ref):
    a = a_ref[...]
    gate = jnp.dot(a, wg_ref[0], preferred_element_type=F32)
    up = jnp.dot(a, wu_ref[0], preferred_element_type=F32)
    o_ref[...] = (gate * jax.nn.sigmoid(gate) * up).astype(o_ref.dtype)


def _swiglu_up(h, w, n_rows):
    e, d, f2 = w.shape
    f = f2 // 2
    tm = _pick(n_rows, (1024, 512, 256, 128, 64))
    tn = _pick(f, (512, 256, 128))
    nj = f // tn
    return pl.pallas_call(
        _swiglu_kernel,
        out_shape=jax.ShapeDtypeStruct((n_rows, e * f), BF16),
        grid=(n_rows // tm, e, nj),
        in_specs=[pl.BlockSpec((tm, d), lambda i, x, j: (i, 0)),
                  pl.BlockSpec((1, d, tn), lambda i, x, j: (x, 0, j)),
                  pl.BlockSpec((1, d, tn), lambda i, x, j: (x, 0, nj + j))],
        out_specs=pl.BlockSpec((tm, tn), lambda i, x, j: (i, x * nj + j)),
        compiler_params=_params("arbitrary", "arbitrary", "arbitrary"),
        name="swiglu_up",
    )(h, w, w)


def _deepnorm_epilogue(alpha, y, x_ref, gate_ref, lng_ref, lnb_ref, next_refs, xo_ref, ho_ref, rows=slice(None)):
    xn = _ln_rows(alpha * x_ref[rows, :] + gate_ref[0] * y, lng_ref[...], lnb_ref[...])
    xo_ref[rows, :] = xn
    if ho_ref is not None:
        shn_ref, scn_ref = next_refs
        ho_ref[rows, :] = (xn * (1.0 + scn_ref[0]) + shn_ref[0]).astype(BF16)


MM_LN_COL_CHUNK = 1024
MM_LN_ROW_CHUNK = 128


def _mm_ln_kernel(alpha, nk, has_next, *refs):
    a_ref, b_ref, gate_ref, lng_ref, lnb_ref = refs[:5]
    next_refs = refs[5:7] if has_next else None
    refs = refs[7:] if has_next else refs[5:]
    if has_next:
        x_hbm, xo_hbm, ho_hbm, acc_ref, xbuf, obuf, hbuf, xsem, osem, hsem = refs
    else:
        x_hbm, xo_hbm, acc_ref, xbuf, obuf, xsem, osem = refs
        ho_hbm = hbuf = hsem = None
    i, k = pl.program_id(0), pl.program_id(1)
    tm, d = acc_ref.shape

    @pl.when(k == 0)
    def _():
        acc_ref[...] = jnp.zeros_like(acc_ref)

    a = a_ref[...]
    cw = min(MM_LN_COL_CHUNK, d)
    for c in range(d // cw):
        cols = slice(c * cw, (c + 1) * cw)
        acc_ref[:, cols] += jnp.dot(a, b_ref[:, cols], preferred_element_type=F32)

    @pl.when(k == nk - 1)
    def _():
        rc = xbuf.shape[1]
        n_chunks = tm // rc

        def hbm_rows(ref, c):
            return ref.at[pl.ds(pl.multiple_of(i * tm + c * rc, rc), rc), :]

        def x_copy(c, slot):
            return pltpu.make_async_copy(hbm_rows(x_hbm, c), xbuf.at[slot], xsem.at[slot])

        def o_copy(c, slot):
            return pltpu.make_async_copy(obuf.at[slot], hbm_rows(xo_hbm, c), osem.at[slot])

        def h_copy(c, slot):
            return pltpu.make_async_copy(hbuf.at[slot], hbm_rows(ho_hbm, c), hsem.at[slot])

        def wait_out(c, slot):
            o_copy(c, slot).wait()
            if has_next:
                h_copy(c, slot).wait()

        x_copy(0, 0).start()

        def body(c, carry):
            slot = c % 2

            @pl.when(c + 1 < n_chunks)
            def _():
                x_copy(c + 1, 1 - slot).start()

            x_copy(c, slot).wait()

            @pl.when(c >= 2)
            def _():
                wait_out(c - 2, slot)

            y = acc_ref[pl.ds(pl.multiple_of(c * rc, rc), rc), :]
            _deepnorm_epilogue(alpha, y, xbuf.at[slot], gate_ref, lng_ref, lnb_ref, next_refs, obuf.at[slot],
                               hbuf.at[slot] if has_next else None)
            o_copy(c, slot).start()
            if has_next:
                h_copy(c, slot).start()
            return carry

        lax.fori_loop(0, n_chunks, body, 0)
        for c in range(max(n_chunks - 2, 0), n_chunks):
            wait_out(c, c % 2)


def _mm_ln(st, n_rows, alpha, a, b, x, mods, layer, gate_idx, lng, lnb, next_mod=None):
    kdim, d = b.shape
    tm = st.row_tile(1024)
    tk = _pick(kdim, (1024, 512, 256, 128))
    nk = kdim // tk
    rc = min(MM_LN_ROW_CHUNK, tm)
    has_next = next_mod is not None
    any_spec = pl.BlockSpec(memory_space=pl.ANY)
    in_specs = [pl.BlockSpec((tm, tk), lambda i, k: (i, k)), pl.BlockSpec((tk, d), lambda i, k: (k, 0)),
                st.mod_spec(layer, gate_idx, tm), _vec_spec(d), _vec_spec(d)]
    args = [a, b, mods, lng.reshape(1, d), lnb.reshape(1, d)]
    out_shape = [jax.ShapeDtypeStruct((n_rows, d), F32)]
    scratch = [pltpu.VMEM((tm, d), F32), pltpu.VMEM((2, rc, d), F32), pltpu.VMEM((2, rc, d), F32)]
    sems = [pltpu.SemaphoreType.DMA((2,)), pltpu.SemaphoreType.DMA((2,))]
    if has_next:
        nl, n_sh, n_sc = next_mod
        in_specs += [st.mod_spec(nl, n_sh, tm), st.mod_spec(nl, n_sc, tm)]
        args += [mods, mods]
        out_shape.append(jax.ShapeDtypeStruct((n_rows, d), BF16))
        scratch.append(pltpu.VMEM((2, rc, d), BF16))
        sems.append(pltpu.SemaphoreType.DMA((2,)))
    res = pl.pallas_call(
        functools.partial(_mm_ln_kernel, alpha, nk, has_next),
        out_shape=tuple(out_shape),
        grid=(n_rows // tm, nk),
        in_specs=in_specs + [any_spec],
        out_specs=tuple(any_spec for _ in out_shape),
        scratch_shapes=scratch + sems,
        compiler_params=_params("arbitrary", "arbitrary"),
        name="matmul_deepnorm",
    )(*args, x)
    return res if has_next else (res[0], None)


def _attn_kernel(n_lat_tiles, group, q_ref, kl_ref, vl_ref, kc_ref, vc_ref, o_ref, k_all, v_all):
    qi = pl.program_id(2)
    seq, ctx_len = kl_ref.shape[0], kc_ref.shape[0]

    @pl.when(qi == 0)
    def _():
        k_all[pl.ds(0, seq), :] = kl_ref[...]
        k_all[pl.ds(seq, ctx_len), :] = kc_ref[...]
        v_all[pl.ds(0, seq), pl.ds(0, HEAD_DIM)] = vl_ref[...]
        v_all[pl.ds(seq, ctx_len), pl.ds(0, HEAD_DIM)] = vc_ref[...]
        v_all[:, pl.ds(HEAD_DIM, HEAD_DIM)] = jnp.ones((seq + ctx_len, HEAD_DIM), v_all.dtype)

    def attend(k, v):
        for g in range(group):
            cols = slice(g * HEAD_DIM, (g + 1) * HEAD_DIM)
            s = lax.dot_general(q_ref[:, cols], k, (((1,), (1,)), ((), ())), preferred_element_type=F32)
            p = jnp.exp(s - jnp.max(s, axis=-1, keepdims=True)).astype(v.dtype)
            o = jnp.dot(p, v, preferred_element_type=F32)
            o_ref[:, cols] = (o[:, :HEAD_DIM] / o[:, HEAD_DIM:]).astype(o_ref.dtype)

    @pl.when(qi < n_lat_tiles)
    def _():
        attend(k_all[...], v_all[...])

    @pl.when(qi >= n_lat_tiles)
    def _():
        attend(k_all[pl.ds(seq, ctx_len), :], v_all[pl.ds(seq, ctx_len), :])


def _attention(st, qkv, d):
    n_rows, n = qkv.shape
    kv = (n - d) // 2
    n_kv = kv // HEAD_DIM
    group = d // kv
    tq = min(256, st.ctx_len)
    ctx_tiles = st.ctx_len // tq
    lat_tiles = st.seq // tq
    qw = group * HEAD_DIM
    k_col0 = d // HEAD_DIM
    v_col0 = (d + kv) // HEAD_DIM

    def q_map(b, h, qi):
        row = jnp.where(qi < lat_tiles, b * lat_tiles + qi, st.n_lat // tq + b * ctx_tiles + (qi - lat_tiles))
        return (row, h)

    ctx_blk0 = st.n_lat // st.ctx_len
    return pl.pallas_call(
        functools.partial(_attn_kernel, lat_tiles, group),
        out_shape=jax.ShapeDtypeStruct((n_rows, d), BF16),
        grid=(st.batch, n_kv, lat_tiles + ctx_tiles),
        in_specs=[pl.BlockSpec((tq, qw), q_map),
                  pl.BlockSpec((st.seq, HEAD_DIM), lambda b, h, qi: (b, k_col0 + h)),
                  pl.BlockSpec((st.seq, HEAD_DIM), lambda b, h, qi: (b, v_col0 + h)),
                  pl.BlockSpec((st.ctx_len, HEAD_DIM), lambda b, h, qi: (ctx_blk0 + b, k_col0 + h)),
                  pl.BlockSpec((st.ctx_len, HEAD_DIM), lambda b, h, qi: (ctx_blk0 + b, v_col0 + h))],
        out_specs=pl.BlockSpec((tq, qw), q_map),
        scratch_shapes=[pltpu.VMEM((st.seq + st.ctx_len, HEAD_DIM), qkv.dtype),
                        pltpu.VMEM((st.seq + st.ctx_len, 2 * HEAD_DIM), qkv.dtype)],
        compiler_params=_params("arbitrary", "arbitrary", "arbitrary"),
        name="gqa_attention",
    )(qkv, qkv, qkv, qkv, qkv)


def _scan8(a, b, reverse):
    row = lax.broadcasted_iota(jnp.int32, a.shape, 0)
    for s in (1, 2, 4):
        if reverse:
            keep = row < SUBLANES - s
            shift = SUBLANES - s
        else:
            keep = row >= s
            shift = s
        a_sh = jnp.where(keep, pltpu.roll(a, shift, 0), 1.0)
        b_sh = jnp.where(keep, pltpu.roll(b, shift, 0), 0.0)
        b = a * b_sh + b
        a = a * a_sh
    return a, b


def _lru_kernel(tm, lat_tiles, reverse, *refs):
    if reverse:
        (xp_ref, x_ref, xn_ref, cw_ref, cb_ref, gw_ref, gb_ref, lam_ref, recf_ref, gelu_ref,
         o_ref, ext_ref, a_ref, b_ref, carry_ref) = refs
    else:
        (xp_ref, x_ref, xn_ref, cw_ref, cb_ref, gw_ref, gb_ref, lam_ref,
         o_ref, ext_ref, a_ref, b_ref, carry_ref) = refs
    s = pl.program_id(1)
    is_ctx = s == 0
    tile = lat_tiles - s if reverse else s - 1
    first = jnp.logical_or(is_ctx, tile == 0)
    last = jnp.logical_or(is_ctx, tile == lat_tiles - 1)

    @pl.when(is_ctx)
    def _():
        carry_ref[...] = jnp.zeros_like(carry_ref)

    ext_ref[pl.ds(HALO, tm), :] = x_ref[...]
    ext_ref[pl.ds(0, HALO), :] = jnp.where(first, 0.0, xp_ref[...])
    ext_ref[pl.ds(HALO + tm, HALO), :] = jnp.where(last, 0.0, xn_ref[...])

    heads, bw, _ = gw_ref.shape
    lam = lam_ref[...]
    softplus = jnp.maximum(-lam, 0.0) + jnp.log1p(jnp.exp(-jnp.abs(lam)))
    for h in range(heads):
        cols = pl.ds(h * bw, bw)
        xr = cb_ref[:, cols] + ext_ref[pl.ds(HALO - CONV_LEFT, tm), cols] * cw_ref[0:1, cols]
        for k in range(1, CONV_W):
            xr = xr + ext_ref[pl.ds(HALO - CONV_LEFT + k, tm), cols] * cw_ref[k:k + 1, cols]
        g = jnp.dot(xr.astype(BF16), gw_ref[h], preferred_element_type=F32)
        r = jax.nn.sigmoid(g[:, :bw] + gb_ref[0:1, cols])
        i = jax.nn.sigmoid(g[:, bw:] + gb_ref[1:2, cols])
        log_a = -LRU_C * r * softplus[:, h * bw:(h + 1) * bw]
        a = jnp.exp(log_a)
        a_ref[:, cols] = a
        b_ref[:, cols] = jnp.sqrt(1.0 - a * a) * (i * xr)

    n_chunks = tm // SUBLANES

    def body(c, carry):
        c = n_chunks - 1 - c if reverse else c
        rows = pl.ds(pl.multiple_of(c * SUBLANES, SUBLANES), SUBLANES)
        a_cum, b_cum = _scan8(a_ref[rows, :], b_ref[rows, :], reverse)
        hs = a_cum * carry + b_cum
        if reverse:
            o_ref[rows, :] = (gelu_ref[rows, :] * (recf_ref[rows, :] + hs)).astype(o_ref.dtype)
            edge = hs[0:1, :]
        else:
            o_ref[rows, :] = hs
            edge = hs[SUBLANES - 1:SUBLANES, :]
        return jnp.broadcast_to(edge, carry.shape)

    carry_ref[...] = lax.fori_loop(0, n_chunks, body, carry_ref[...])


def _lru_scan(st, yx, conv_w, conv_b, gate_w, gate_b, lam, reverse, rec_f=None):
    n_rows = yx.shape[0]
    d = st.d
    tm = min(256, st.ctx_len)
    assert st.ctx_len == tm
    lat_tiles = st.seq // tm
    ctx_blk0 = st.n_lat // tm
    per = tm // HALO
    last_halo = n_rows // HALO - 1

    def blk(b, s):
        lat = b * lat_tiles + (lat_tiles - s if reverse else s - 1)
        return jnp.where(s == 0, ctx_blk0 + b, lat)

    heads = gate_w.shape[1]
    bw = d // heads
    gw = jnp.concatenate([gate_w[0], gate_w[1]], axis=-1).astype(BF16)
    in_specs = [pl.BlockSpec((HALO, d), lambda b, s: (jnp.maximum(blk(b, s) * per - 1, 0), 1)),
                pl.BlockSpec((tm, d), lambda b, s: (blk(b, s), 1)),
                pl.BlockSpec((HALO, d), lambda b, s: (jnp.minimum((blk(b, s) + 1) * per, last_halo), 1)),
                pl.BlockSpec((CONV_W, d), lambda b, s: (0, 0)), _vec_spec(d),
                pl.BlockSpec((heads, bw, 2 * bw), lambda b, s: (0, 0, 0)),
                pl.BlockSpec((2, d), lambda b, s: (0, 0)), _vec_spec(d)]
    args = [yx, yx, yx, conv_w, conv_b.reshape(1, d), gw, gate_b.reshape(2, d), lam.reshape(1, d)]
    if reverse:
        in_specs += [pl.BlockSpec((tm, d), lambda b, s: (blk(b, s), 0)),
                     pl.BlockSpec((tm, d), lambda b, s: (blk(b, s), 0))]
        args += [rec_f, yx]
    return pl.pallas_call(
        functools.partial(_lru_kernel, tm, lat_tiles, reverse),
        out_shape=jax.ShapeDtypeStruct((n_rows, d), BF16 if reverse else F32),
        grid=(st.batch, lat_tiles + 1),
        in_specs=in_specs,
        out_specs=pl.BlockSpec((tm, d), lambda b, s: (blk(b, s), 0)),
        scratch_shapes=[pltpu.VMEM((tm + 2 * HALO, d), F32), pltpu.VMEM((tm, d), F32),
                        pltpu.VMEM((tm, d), F32), pltpu.VMEM((SUBLANES, d), F32)],
        compiler_params=_params("arbitrary", "arbitrary"),
        name="rglru_reverse" if reverse else "rglru_forward",
    )(*args)


ROUTE_LANES = 8
MOE_ROW_TILE = 512


def _lane_pick(rec, k):
    lane = lax.broadcasted_iota(jnp.int32, rec.shape, 1)
    return jnp.sum(jnp.where(lane == k, rec, 0.0), axis=1, keepdims=True)


def _router_kernel(x_ref, sh_ref, sc_ref, w_ref, route_ref, cnt_ref, run_ref):
    @pl.when(pl.program_id(0) == 0)
    def _():
        run_ref[...] = jnp.zeros_like(run_ref)

    h = x_ref[...] * (1.0 + sc_ref[0]) + sh_ref[0]
    logits = jnp.dot(h, w_ref[...], preferred_element_type=F32, precision=HIGHEST)
    tm, n_exp = logits.shape
    lane = lax.broadcasted_iota(jnp.int32, logits.shape, 1).astype(F32)
    m1 = jnp.max(logits, axis=-1, keepdims=True)
    i1 = jnp.min(jnp.where(logits == m1, lane, float(n_exp)), axis=-1, keepdims=True)
    pick1 = lane == i1
    rest = jnp.where(pick1, -jnp.inf, logits)
    m2 = jnp.max(rest, axis=-1, keepdims=True)
    i2 = jnp.min(jnp.where(rest == m2, lane, float(n_exp)), axis=-1, keepdims=True)
    pick2 = lane == i2
    e2 = jnp.exp(m2 - m1)
    den = 1.0 + e2
    onehot = jnp.where(pick1, 1.0, jnp.where(pick2, 1.0, 0.0))
    earlier = (lax.broadcasted_iota(jnp.int32, (tm, tm), 0) > lax.broadcasted_iota(jnp.int32, (tm, tm), 1))
    before = jnp.dot(jnp.where(earlier, 1.0, 0.0).astype(BF16), onehot.astype(BF16),
                     preferred_element_type=F32) + run_ref[...]
    rank1 = jnp.sum(jnp.where(pick1, before, 0.0), axis=-1, keepdims=True)
    rank2 = jnp.sum(jnp.where(pick2, before, 0.0), axis=-1, keepdims=True)
    rec = jnp.zeros((tm, ROUTE_LANES), F32)
    for k, v in enumerate((i1, i2, 1.0 / den, e2 / den, rank1, rank2)):
        rec = jnp.where(lax.broadcasted_iota(jnp.int32, rec.shape, 1) == k, v, rec)
    route_ref[...] = rec
    run_ref[...] += jnp.sum(onehot, axis=0, keepdims=True)
    cnt_ref[...] = run_ref[...]


def _router(st, n_rows, x, mods, layer, w):
    d, n_exp = w.shape
    tm = min(256, st.ctx_len)
    return pl.pallas_call(
        _router_kernel,
        out_shape=(jax.ShapeDtypeStruct((n_rows, ROUTE_LANES), F32), jax.ShapeDtypeStruct((1, n_exp), F32)),
        grid=(n_rows // tm,),
        in_specs=[pl.BlockSpec((tm, d), lambda i: (i, 0)), st.mod_spec(layer, 3, tm), st.mod_spec(layer, 4, tm),
                  pl.BlockSpec((d, n_exp), lambda i: (0, 0))],
        out_specs=(pl.BlockSpec((tm, ROUTE_LANES), lambda i: (i, 0)), pl.BlockSpec((1, n_exp), lambda i: (0, 0))),
        scratch_shapes=[pltpu.VMEM((1, n_exp), F32)],
        compiler_params=_params("arbitrary"),
        name="moe_router",
    )(x, mods, mods, w)


def _dispatch_plan(route, counts, tm):
    n_tok = route.shape[0]
    n_exp = counts.shape[1]
    expert = route[:, 0:2].astype(jnp.int32)
    rank = route[:, 4:6].astype(jnp.int32)
    cnt = counts[0].astype(jnp.int32)
    padded = (cnt + tm - 1) // tm * tm
    ends = jnp.cumsum(padded)
    starts = ends - padded
    dest = (starts[expert] + rank).T.reshape(2 * n_tok)
    n_tiles = (2 * n_tok + n_exp * tm) // tm
    tile_row0 = jnp.arange(n_tiles, dtype=jnp.int32) * tm
    tile_expert = jnp.minimum(jnp.sum(tile_row0[:, None] >= ends[None, :], axis=1), n_exp - 1).astype(jnp.int32)
    n_used = (ends[-1:] // tm).astype(jnp.int32)
    return dest, tile_expert, n_used, n_tiles


HI16 = 0xFFFF0000


def _pack_bf16_pairs(h):
    half = h.shape[1] // 2
    lo = lax.bitcast_convert_type(h[:, :half].astype(jnp.bfloat16).astype(F32), jnp.uint32)
    hi = lax.bitcast_convert_type(h[:, half:].astype(jnp.bfloat16).astype(F32), jnp.uint32)
    return (lo >> 16) | (hi & jnp.uint32(HI16))


def _unpack_bf16_pairs(p):
    lo = lax.bitcast_convert_type(p << 16, F32).astype(jnp.bfloat16)
    hi = lax.bitcast_convert_type(p & jnp.uint32(HI16), F32).astype(jnp.bfloat16)
    return lo, hi


def _dispatch_kernel(tm, n_tok, dest_ref, x_ref, sh_ref, sc_ref, init_ref, o_ref, h_ref, sem):
    del init_ref
    base = pl.program_id(0) * tm
    h_ref[...] = _pack_bf16_pairs(x_ref[...] * (1.0 + sc_ref[0]) + sh_ref[0])

    def row_copy(r, d):
        return pltpu.make_async_copy(h_ref.at[pl.ds(r, 1), :], o_ref.at[pl.ds(d, 1), :], sem)

    def issue(r, carry):
        row_copy(r, dest_ref[base + r]).start()
        row_copy(r, dest_ref[n_tok + base + r]).start()
        return carry

    def drain(r, carry):
        row_copy(r, 0).wait()
        row_copy(r, 0).wait()
        return carry

    lax.fori_loop(0, tm, issue, 0)
    lax.fori_loop(0, tm, drain, 0)


def _dispatch(st, n_rows, x, mods, layer, dest, n_out):
    d = st.d
    tm = min(256, st.ctx_len)
    grid_spec = pltpu.PrefetchScalarGridSpec(
        num_scalar_prefetch=1,
        grid=(n_rows // tm,),
        in_specs=[pl.BlockSpec((tm, d), lambda i, dest: (i, 0)), st.mod_spec(layer, 3, tm), st.mod_spec(layer, 4, tm),
                  pl.BlockSpec(memory_space=pl.ANY)],
        out_specs=pl.BlockSpec(memory_space=pl.ANY),
        scratch_shapes=[pltpu.VMEM((tm, d // 2), jnp.uint32), pltpu.SemaphoreType.DMA],
    )
    return pl.pallas_call(
        functools.partial(_dispatch_kernel, tm, n_rows),
        out_shape=jax.ShapeDtypeStruct((n_out, d // 2), jnp.uint32),
        grid_spec=grid_spec,
        input_output_aliases={4: 0},
        compiler_params=_params("arbitrary"),
        name="moe_dispatch",
    )(dest, x, mods, mods, jnp.zeros((n_out, d // 2), jnp.uint32))


def _moe_up_kernel(te_ref, nu_ref, a_ref, wg_ref, wu_ref, o_ref):
    i = pl.program_id(1)

    @pl.when(i < nu_ref[0])
    def _():
        lo, hi = _unpack_bf16_pairs(a_ref[...])
        half = lo.shape[1]

        def proj(w_ref):
            return (jnp.dot(lo, w_ref[0, :half, :], preferred_element_type=F32)
                    + jnp.dot(hi, w_ref[0, half:, :], preferred_element_type=F32))

        gate, up = proj(wg_ref), proj(wu_ref)
        o_ref[...] = (gate * jax.nn.sigmoid(gate) * up).astype(o_ref.dtype)

    @pl.when(i >= nu_ref[0])
    def _():
        o_ref[...] = jnp.zeros_like(o_ref)


def _moe_up(xg, w, tile_expert, n_used, tm):
    n_out = xg.shape[0]
    d, f2 = w.shape[1:]
    f = f2 // 2
    tn = _pick(f, (512, 256, 128))
    nj = f // tn
    grid_spec = pltpu.PrefetchScalarGridSpec(
        num_scalar_prefetch=2,
        grid=(nj, n_out // tm),
        in_specs=[pl.BlockSpec((tm, d // 2), lambda j, i, te, nu: (i, 0)),
                  pl.BlockSpec((1, d, tn), lambda j, i, te, nu: (te[i], 0, j)),
                  pl.BlockSpec((1, d, tn), lambda j, i, te, nu: (te[i], 0, nj + j))],
        out_specs=pl.BlockSpec((tm, tn), lambda j, i, te, nu: (i, j)),
    )
    return pl.pallas_call(
        _moe_up_kernel,
        out_shape=jax.ShapeDtypeStruct((n_out, f), BF16),
        grid_spec=grid_spec,
        compiler_params=_params("arbitrary", "arbitrary"),
        name="moe_up",
    )(tile_expert, n_used, xg, w, w)


def _moe_down_kernel(te_ref, nu_ref, a_ref, w_ref, o_ref):
    i = pl.program_id(1)

    @pl.when(i < nu_ref[0])
    def _():
        o_ref[...] = jnp.dot(a_ref[...], w_ref[0], preferred_element_type=F32)

    @pl.when(i >= nu_ref[0])
    def _():
        o_ref[...] = jnp.zeros_like(o_ref)


def _moe_down(act, w, tile_expert, n_used, tm):
    n_out, f = act.shape
    d = w.shape[2]
    tn = _pick(d, (2048, 1024, 512, 256, 128))
    grid_spec = pltpu.PrefetchScalarGridSpec(
        num_scalar_prefetch=2,
        grid=(d // tn, n_out // tm),
        in_specs=[pl.BlockSpec((tm, f), lambda j, i, te, nu: (i, 0)),
                  pl.BlockSpec((1, f, tn), lambda j, i, te, nu: (te[i], 0, j))],
        out_specs=pl.BlockSpec((tm, tn), lambda j, i, te, nu: (i, j)),
    )
    return pl.pallas_call(
        _moe_down_kernel,
        out_shape=jax.ShapeDtypeStruct((n_out, d), F32),
        grid_spec=grid_spec,
        compiler_params=_params("arbitrary", "arbitrary"),
        name="moe_down",
    )(tile_expert, n_used, act, w)


COMBINE_ROW_CHUNK = 64


def _combine_kernel(tm, n_tok, alpha, has_next, dest_ref, *refs):
    route_ref, x_ref, gate_ref, lng_ref, lnb_ref = refs[:5]
    next_refs = refs[5:7] if has_next else None
    refs = refs[7:] if has_next else refs[5:]
    y_ref, xo_ref = refs[:2]
    ho_ref = refs[2] if has_next else None
    buf_ref, sem = refs[-2:]
    i = pl.program_id(0)
    n_steps = pl.num_programs(0)
    slot = i % 2

    def row_copy(slot, which, r, src):
        return pltpu.make_async_copy(y_ref.at[pl.ds(src, 1), :], buf_ref.at[slot, which, pl.ds(r, 1), :],
                                     sem.at[slot])

    def gather(tile, slot):
        def issue(r, carry):
            row_copy(slot, 0, r, dest_ref[tile * tm + r]).start()
            row_copy(slot, 1, r, dest_ref[n_tok + tile * tm + r]).start()
            return carry
        lax.fori_loop(0, tm, issue, 0)

    @pl.when(i == 0)
    def _():
        gather(0, 0)

    @pl.when(i + 1 < n_steps)
    def _():
        gather(i + 1, 1 - slot)

    def drain(r, carry):
        row_copy(slot, 0, r, 0).wait()
        row_copy(slot, 1, r, 0).wait()
        return carry

    lax.fori_loop(0, tm, drain, 0)
    rc = min(COMBINE_ROW_CHUNK, tm)

    def body(c, carry):
        rows = pl.ds(pl.multiple_of(c * rc, rc), rc)
        rec = route_ref[rows, :]
        y = _lane_pick(rec, 2) * buf_ref[slot, 0, rows, :] + _lane_pick(rec, 3) * buf_ref[slot, 1, rows, :]
        _deepnorm_epilogue(alpha, y, x_ref, gate_ref, lng_ref, lnb_ref, next_refs, xo_ref, ho_ref, rows)
        return carry

    lax.fori_loop(0, tm // rc, body, 0)


def _combine(st, n_rows, alpha, yg, route, dest, x, mods, layer, lng, lnb, next_mod=None):
    d = st.d
    tm = min(256, st.ctx_len)
    has_next = next_mod is not None
    row = lambda i, dest: (i, 0)
    in_specs = [pl.BlockSpec((tm, ROUTE_LANES), row), pl.BlockSpec((tm, d), row), st.mod_spec(layer, 5, tm),
                _vec_spec(d), _vec_spec(d)]
    args = [route, x, mods, lng.reshape(1, d), lnb.reshape(1, d)]
    out_shape = [jax.ShapeDtypeStruct((n_rows, d), F32)]
    out_specs = [pl.BlockSpec((tm, d), row)]
    if has_next:
        nl, n_sh, n_sc = next_mod
        in_specs += [st.mod_spec(nl, n_sh, tm), st.mod_spec(nl, n_sc, tm)]
        args += [mods, mods]
        out_shape.append(jax.ShapeDtypeStruct((n_rows, d), BF16))
        out_specs.append(pl.BlockSpec((tm, d), row))
    in_specs.append(pl.BlockSpec(memory_space=pl.ANY))
    args.append(yg)
    grid_spec = pltpu.PrefetchScalarGridSpec(
        num_scalar_prefetch=1,
        grid=(n_rows // tm,),
        in_specs=in_specs,
        out_specs=tuple(out_specs),
        scratch_shapes=[pltpu.VMEM((2, 2, tm, d), F32), pltpu.SemaphoreType.DMA((2,))],
    )
    res = pl.pallas_call(
        functools.partial(_combine_kernel, tm, n_rows, alpha, has_next),
        out_shape=tuple(out_shape),
        grid_spec=grid_spec,
        compiler_params=_params("arbitrary"),
        name="moe_combine",
    )(dest, *args)
    return res if has_next else (res[0], None)


def _moe_layer(st, n_rows, alpha, x, mods, layer, router_w, w_in, w_out, lng, lnb, next_mod):
    route, counts = _router(st, n_rows, x, mods, layer, router_w)
    dest, tile_expert, n_used, n_tiles = _dispatch_plan(route, counts, MOE_ROW_TILE)
    xg = _dispatch(st, n_rows, x, mods, layer, dest, n_tiles * MOE_ROW_TILE)
    act = _moe_up(xg, w_in, tile_expert, n_used, MOE_ROW_TILE)
    yg = _moe_down(act, w_out, tile_expert, n_used, MOE_ROW_TILE)
    return _combine(st, n_rows, alpha, yg, route, dest, x, mods, layer, lng, lnb, next_mod)


def _cast_kernel(is_pad, x_ref, o_ref):
    pad = is_pad(pl.program_id(0), pl.program_id(1))

    @pl.when(jnp.logical_not(pad))
    def _():
        o_ref[...] = x_ref[...].astype(o_ref.dtype)

    @pl.when(pad)
    def _():
        o_ref[...] = jnp.zeros_like(o_ref)


def _cast_blocks(w, layer, block, out_blocks, src_of, n_src):
    def in_map(i, j):
        bi, bj = src_of(i, j)
        return (layer, jnp.minimum(bi, n_src[0] - 1), jnp.minimum(bj, n_src[1] - 1))

    def is_pad(i, j):
        bi, bj = src_of(i, j)
        return jnp.logical_or(bi >= n_src[0], bj >= n_src[1])

    return pl.pallas_call(
        functools.partial(_cast_kernel, is_pad),
        out_shape=jax.ShapeDtypeStruct((out_blocks[0] * block[0], out_blocks[1] * block[1]), BF16),
        grid=out_blocks,
        in_specs=[pl.BlockSpec((None,) + block, in_map)],
        out_specs=pl.BlockSpec(block, lambda i, j: (i, j)),
        compiler_params=_params("arbitrary", "arbitrary"),
        name="cast_weights",
    )(w)


def _cast(w, layer):
    w3 = w.reshape(w.shape[0], -1, w.shape[-1])
    rows, cols = w3.shape[1:]
    block = (_pick(rows, (512, 256, 128, 64, 32, 16)), _pick(cols, (4096, 2048, 1024, 512, 256, 128)))
    n = (rows // block[0], cols // block[1])
    return _cast_blocks(w3, layer, block, n, lambda i, j: (i, j), n).reshape(w.shape[1:])


FFN_PAD_BLOCK = 256


def _cast_pad_ffn(w_in, w_out, layer, mult):
    d, f2 = w_in.shape[1:]
    f = f2 // 2
    blk = _pick(f, (FFN_PAD_BLOCK, 128))
    fp = _round_up(f, mult)
    nb, nbp = f // blk, fp // blk
    tr = _pick(d, (2048, 1024, 512, 256, 128))

    def in_src(i, j):
        up = j - nbp
        return (i, jnp.where(j < nbp, jnp.where(j < nb, j, 2 * nb), jnp.where(up < nb, up + nb, 2 * nb)))

    w_in_p = _cast_blocks(w_in, layer, (tr, blk), (d // tr, 2 * nbp), in_src, (d // tr, 2 * nb))
    dc = _pick(d, (4096, 2048, 1024, 512, 256, 128))
    w_out_p = _cast_blocks(w_out, layer, (blk, dc), (nbp, d // dc), lambda i, j: (i, j), (nb, d // dc))
    return w_in_p.reshape(1, d, 2 * fp), w_out_p


def kernel(x, c, ctx, c_ctx, ada_down, ada_up, ada_b, ln_g, ln_b, pool_w, pool_scale, attn_wqkv, attn_q_gain,
           attn_k_gain, attn_wo, lru_w_in, lru_conv_w, lru_conv_b, lru_gate_w, lru_gate_b, lru_lambda, lru_w_out,
           ffn_w_in, ffn_w_out, moe_router, moe_w_in, moe_w_out):
    batch, seq, d = x.shape
    ctx_len = ctx.shape[1]
    depth = ada_down.shape[0]
    assert batch < MOD_ROWS and seq % ctx_len == 0 and ctx_len % SUBLANES == 0
    st = _Stream(batch, seq, ctx_len, d)
    alpha = (2.0 * depth) ** 0.25
    mixers = [i % N_MIXERS for i in range(depth)]
    is_moe = [i % 2 == 1 for i in range(depth)]
    ctx_needed_after = [any(mixers[j] != 0 for j in range(i + 1, depth)) for i in range(depth)]

    cvec = jnp.concatenate([c, c_ctx[None], jnp.zeros((MOD_ROWS - batch - 1, d), F32)], axis=0)
    mods = _adaln(cvec, ada_down, ada_up, ada_b)

    xs = jnp.concatenate([x.reshape(batch * seq, d), ctx.reshape(batch * ctx_len, d)], axis=0)
    n_rows = st.n_all
    h = None
    for i in range(depth):
        mixer = mixers[i]
        mi = mixers[:i].count(mixer)
        fi = is_moe[:i].count(is_moe[i])
        lng, lnb = ln_g[i], ln_b[i]
        ffn_mod = None if is_moe[i] else (i, 3, 4)
        if not ctx_needed_after[i] and mixer == 0 and n_rows != st.n_lat:
            n_rows = st.n_lat
            xs = xs[:n_rows]

        if mixer == 0:
            xs, hf = _pool_layer(st, n_rows, i, alpha, xs, mods, _cast(pool_w, mi), pool_scale[mi],
                                 lng[0], lnb[0])
        elif mixer == 1:
            qkv = _qkv_proj(st, h, _cast(attn_wqkv, mi), attn_q_gain[mi], attn_k_gain[mi])
            o = _attention(st, qkv, d)
            rows_out = n_rows if ctx_needed_after[i] else st.n_lat
            xs, hf = _mm_ln(st, rows_out, alpha, o, _cast(attn_wo, mi), xs, mods, i, 2, lng[0], lnb[0],
                            next_mod=ffn_mod)
            n_rows = rows_out
        else:
            yx = _lru_in_proj(h, _cast(lru_w_in, mi))
            rec_f = _lru_scan(st, yx, lru_conv_w[mi], lru_conv_b[mi], lru_gate_w[mi, 0], lru_gate_b[mi, 0],
                              lru_lambda[mi, 0], False)
            m = _lru_scan(st, yx, lru_conv_w[mi], lru_conv_b[mi], lru_gate_w[mi, 1], lru_gate_b[mi, 1],
                          lru_lambda[mi, 1], True, rec_f)
            rows_out = n_rows if ctx_needed_after[i] else st.n_lat
            xs, hf = _mm_ln(st, rows_out, alpha, m, _cast(lru_w_out, mi), xs, mods, i, 2, lng[0], lnb[0],
                            next_mod=ffn_mod)
            n_rows = rows_out

        next_mod = (i + 1, 0, 1) if i + 1 < depth and mixers[i + 1] != 0 else None
        if is_moe[i]:
            xs, h = _moe_layer(st, n_rows, alpha, xs, mods, i, moe_router[fi], _cast(moe_w_in, fi),
                               _cast(moe_w_out, fi), lng[1], lnb[1], next_mod)
        else:
            w_in_p, w_out_p = _cast_pad_ffn(ffn_w_in, ffn_w_out, fi, 1024)
            act = _swiglu_up(hf, w_in_p, n_rows)
            xs, h = _mm_ln(st, n_rows, alpha, act, w_out_p, xs, mods, i, 5, lng[1], lnb[1], next_mod=next_mod)
    return xs[:st.n_lat].reshape(batch, seq, d)
```

```python
import functools

import jax
import jax.numpy as jnp
from jax import lax
from jax.experimental import pallas as pl
from jax.experimental.pallas import tpu as pltpu

F32 = jnp.float32
BF16 = jnp.bfloat16
HIGHEST = lax.Precision.HIGHEST

HEAD_DIM = 128
AXIS_DIM = HEAD_DIM // 2
GRID_W = 64
ROPE_THETA = 10000.0
LN_EPS = 1e-6
RMS_EPS = 1e-6
LRU_C = 8.0
LOG2E = 1.4426950408889634
CONV_W = 4
CONV_LEFT = CONV_W // 2
POOL_WINDOWS = (2, 4, 8, 16)
N_MOD = 6
N_MIXERS = 3
MOD_ROWS = 8

SUBLANES = 8
LANES = 128
VMEM_LIMIT_BYTES = 56 * 1024 * 1024
HALO = SUBLANES


def _params(*sem):
    return pltpu.CompilerParams(dimension_semantics=sem, vmem_limit_bytes=VMEM_LIMIT_BYTES)


def _pick(n, prefs):
    for p in prefs:
        if n % p == 0:
            return p
    return n


def _round_up(n, m):
    return (n + m - 1) // m * m


def _ln_rows(z, g, b):
    mu = jnp.mean(z, axis=-1, keepdims=True)
    zc = z - mu
    var = jnp.mean(zc * zc, axis=-1, keepdims=True)
    return zc * lax.rsqrt(var + LN_EPS) * g + b


class _Stream:
    def __init__(self, batch, seq, ctx_len, d):
        self.batch, self.seq, self.ctx_len, self.d = batch, seq, ctx_len, d
        self.n_lat = batch * seq
        self.n_all = self.n_lat + batch * ctx_len

    def row_tile(self, largest):
        t = largest
        while self.seq % t or (self.batch * self.ctx_len) % t:
            t //= 2
        return t

    def group(self, row0):
        return jnp.minimum(row0 // self.seq, self.batch)

    def mod_spec(self, layer, which, tm):
        def imap(i, *_):
            return ((layer * MOD_ROWS + self.group(i * tm)) * N_MOD + which, 0, 0)
        return pl.BlockSpec((1, 1, self.d), imap)

    def seq_pos(self, row0):
        is_ctx = row0 >= self.n_lat
        pos = jnp.where(is_ctx, (row0 - self.n_lat) % self.ctx_len, row0 % self.seq)
        return pos, jnp.where(is_ctx, self.ctx_len, self.seq)


def _vec_spec(d):
    return pl.BlockSpec((1, d), lambda *_: (0, 0))


def _halo_specs(tm, width, n_rows, col_block=0):
    per = tm // HALO
    last = n_rows // HALO - 1
    prev = pl.BlockSpec((HALO, width), lambda i, *_: (jnp.maximum(i * per - 1, 0), col_block))
    cur = pl.BlockSpec((tm, width), lambda i, *_: (i, col_block))
    nxt = pl.BlockSpec((HALO, width), lambda i, *_: (jnp.minimum((i + 1) * per, last), col_block))
    return prev, cur, nxt


def _adaln_kernel(c_ref, down_ref, up_ref, b_ref, o_ref):
    c = c_ref[...]
    s = c * jax.nn.sigmoid(c)
    t = jnp.dot(s, down_ref[0], preferred_element_type=F32, precision=HIGHEST)
    o_ref[0] = jnp.dot(t, up_ref[0], preferred_element_type=F32, precision=HIGHEST) + b_ref[0]


def _adaln(cvec, ada_down, ada_up, ada_b):
    depth, d, rank = ada_down.shape
    n = ada_up.shape[-1]
    tn = _pick(n, (4096, 2048, 1024, 512, 256, 128))
    out = pl.pallas_call(
        _adaln_kernel,
        out_shape=jax.ShapeDtypeStruct((depth, MOD_ROWS, n), F32),
        grid=(depth, n // tn),
        in_specs=[
            pl.BlockSpec((MOD_ROWS, d), lambda l, j: (0, 0)),
            pl.BlockSpec((1, d, rank), lambda l, j: (l, 0, 0)),
            pl.BlockSpec((1, rank, tn), lambda l, j: (l, 0, j)),
            pl.BlockSpec((1, 1, tn), lambda l, j: (l, 0, j)),
        ],
        out_specs=pl.BlockSpec((1, MOD_ROWS, tn), lambda l, j: (l, 0, j)),
        compiler_params=_params("arbitrary", "arbitrary"),
        name="adaln",
    )(cvec, ada_down, ada_up, ada_b.reshape(depth, 1, n))
    return out.reshape(depth * MOD_ROWS * N_MOD, 1, d)


def _pool_kernel(st, tm, alpha, xp_ref, x_ref, xn_ref, sh_ref, sc_ref, gm_ref, shf_ref, scf_ref,
                 w_ref, ps_ref, lng_ref, lnb_ref, xo_ref, ho_ref, ext_ref, z_ref):
    row0 = pl.program_id(0) * tm
    pos0, seqlen = st.seq_pos(row0)
    first = pos0 == 0
    last = pos0 + tm == seqlen
    one_sc = 1.0 + sc_ref[0]
    sh = sh_ref[0]
    x = x_ref[...]
    ext_ref[pl.ds(HALO, tm), :] = x * one_sc + sh
    ext_ref[pl.ds(0, HALO), :] = jnp.where(first, 0.0, xp_ref[...] * one_sc + sh)
    ext_ref[pl.ds(HALO + tm, HALO), :] = jnp.where(last, 0.0, xn_ref[...] * one_sc + sh)

    pos = pos0 + lax.broadcasted_iota(jnp.int32, (tm, 1), 0)
    groups = len(POOL_WINDOWS)
    gw = st.d // groups
    for g, k in enumerate(POOL_WINDOWS):
        cols = pl.ds(g * gw, gw)
        lo = jnp.maximum(pos - k // 2, 0)
        hi = jnp.minimum(pos + (k - k // 2), seqlen)
        cnt = (hi - lo).astype(F32)
        win = ext_ref[pl.ds(HALO - k // 2, tm), cols]
        for j in range(1, k):
            win = win + ext_ref[pl.ds(HALO - k // 2 + j, tm), cols]
        diff = (win / cnt - ext_ref[pl.ds(HALO, tm), cols]).astype(BF16)
        y = jnp.dot(diff, w_ref[g], preferred_element_type=F32) * ps_ref[:, cols]
        z_ref[:, cols] = alpha * x_ref[:, cols] + gm_ref[0, :, cols] * y
    xn = _ln_rows(z_ref[...], lng_ref[...], lnb_ref[...])
    xo_ref[...] = xn
    ho_ref[...] = (xn * (1.0 + scf_ref[0]) + shf_ref[0]).astype(BF16)


def _pool_layer(st, n_rows, layer, alpha, x, mods, w, pscale, lng, lnb):
    d = st.d
    tm = min(256, st.ctx_len)
    prev, cur, nxt = _halo_specs(tm, d, n_rows)
    groups, gw, _ = w.shape
    return pl.pallas_call(
        functools.partial(_pool_kernel, st, tm, alpha),
        out_shape=(jax.ShapeDtypeStruct((n_rows, d), F32), jax.ShapeDtypeStruct((n_rows, d), BF16)),
        grid=(n_rows // tm,),
        in_specs=[prev, cur, nxt] + [st.mod_spec(layer, q, tm) for q in (0, 1, 2, 3, 4)] + [
            pl.BlockSpec((groups, gw, gw), lambda i: (0, 0, 0)),
            _vec_spec(d), _vec_spec(d), _vec_spec(d)],
        out_specs=(pl.BlockSpec((tm, d), lambda i: (i, 0)), pl.BlockSpec((tm, d), lambda i: (i, 0))),
        scratch_shapes=[pltpu.VMEM((tm + 2 * HALO, d), F32), pltpu.VMEM((tm, d), F32)],
        compiler_params=_params("arbitrary"),
        name="pool_mixer",
    )(x, x, x, mods, mods, mods, mods, mods, w, pscale.reshape(1, d), lng.reshape(1, d), lnb.reshape(1, d))


def _qkv_kernel(n_qk_tiles, a_ref, b_ref, gain_ref, c_ref, s1_ref, s2_ref, o_ref):
    j = pl.program_id(1)
    acc = jnp.dot(a_ref[...], b_ref[...], preferred_element_type=F32)
    tn = acc.shape[1]

    @pl.when(j < n_qk_tiles)
    def _():
        cos, s1, s2 = c_ref[...], s1_ref[...], s2_ref[...]
        for hh in range(tn // HEAD_DIM):
            cols = slice(hh * HEAD_DIM, (hh + 1) * HEAD_DIM)
            xh = acc[:, cols]
            n = xh * lax.rsqrt(jnp.mean(xh * xh, axis=-1, keepdims=True) + RMS_EPS) * gain_ref[:, cols]
            half = AXIS_DIM // 2
            rot = n * cos + pltpu.roll(n, HEAD_DIM - half, 1) * s1 + pltpu.roll(n, half, 1) * s2
            o_ref[:, cols] = rot.astype(o_ref.dtype)

    @pl.when(j >= n_qk_tiles)
    def _():
        o_ref[...] = acc.astype(o_ref.dtype)


def _rope_tables(seq, tm):
    rows = seq // GRID_W
    r = jnp.repeat(jnp.arange(rows, dtype=F32), GRID_W)
    col = jnp.tile(jnp.arange(GRID_W, dtype=F32), rows)
    inv = ROPE_THETA ** (-jnp.arange(0, AXIS_DIM, 2, dtype=F32) / AXIS_DIM)
    ang_r = r[:, None] * inv
    ang_c = col[:, None] * inv
    zero = jnp.zeros_like(ang_r)
    cos = jnp.concatenate([jnp.cos(ang_r), jnp.cos(ang_r), jnp.cos(ang_c), jnp.cos(ang_c)], axis=-1)
    s1 = jnp.concatenate([-jnp.sin(ang_r), zero, -jnp.sin(ang_c), zero], axis=-1)
    s2 = jnp.concatenate([zero, jnp.sin(ang_r), zero, jnp.sin(ang_c)], axis=-1)
    ident = jnp.ones((tm, HEAD_DIM), F32)
    nul = jnp.zeros((tm, HEAD_DIM), F32)
    return (jnp.concatenate([cos, ident]), jnp.concatenate([s1, nul]), jnp.concatenate([s2, nul]))


def _qkv_proj(st, h, w, q_gain, k_gain):
    n_rows, d = h.shape
    n = w.shape[1]
    kv = (n - d) // 2
    tm = st.row_tile(1024)
    tn = _pick(kv, (1024, 512, 256, 128))
    scale = HEAD_DIM ** -0.5 * LOG2E
    gain = jnp.concatenate([jnp.tile(q_gain * scale, d // HEAD_DIM), jnp.tile(k_gain, kv // HEAD_DIM),
                            jnp.ones((kv,), F32)]).reshape(1, n)
    cos, s1, s2 = _rope_tables(st.seq, tm)
    per_seq = st.seq // tm

    def tab_map(i, j):
        return (jnp.where(i * tm < st.n_lat, i % per_seq, per_seq), 0)

    tab = pl.BlockSpec((tm, HEAD_DIM), tab_map)
    return pl.pallas_call(
        functools.partial(_qkv_kernel, (d + kv) // tn),
        out_shape=jax.ShapeDtypeStruct((n_rows, n), BF16),
        grid=(n_rows // tm, n // tn),
        in_specs=[pl.BlockSpec((tm, d), lambda i, j: (i, 0)), pl.BlockSpec((d, tn), lambda i, j: (0, j)),
                  pl.BlockSpec((1, tn), lambda i, j: (0, j)), tab, tab, tab],
        out_specs=pl.BlockSpec((tm, tn), lambda i, j: (i, j)),
        compiler_params=_params("arbitrary", "arbitrary"),
        name="qkv_proj",
    )(h, w, gain, cos, s1, s2)


def _gelu_tanh(x):
    return 0.5 * x * (1.0 + jnp.tanh(0.7978845608028654 * (x + 0.044715 * (x * x * x))))


def _lru_in_kernel(n_gelu_tiles, a_ref, b_ref, o_ref):
    j = pl.program_id(1)
    acc = jnp.dot(a_ref[...], b_ref[...], preferred_element_type=F32)

    @pl.when(j < n_gelu_tiles)
    def _():
        o_ref[...] = _gelu_tanh(acc)

    @pl.when(j >= n_gelu_tiles)
    def _():
        o_ref[...] = acc


def _lru_in_proj(h, w):
    n_rows, d = h.shape
    n = w.shape[1]
    tm = _pick(n_rows, (1024, 512, 256, 128, 64))
    tn = _pick(n // 2, (1024, 512, 256, 128))
    return pl.pallas_call(
        functools.partial(_lru_in_kernel, (n // 2) // tn),
        out_shape=jax.ShapeDtypeStruct((n_rows, n), F32),
        grid=(n_rows // tm, n // tn),
        in_specs=[pl.BlockSpec((tm, d), lambda i, j: (i, 0)), pl.BlockSpec((d, tn), lambda i, j: (0, j))],
        out_specs=pl.BlockSpec((tm, tn), lambda i, j: (i, j)),
        compiler_params=_params("arbitrary", "arbitrary"),
        name="lru_in_proj",
    )(h, w)


def _swiglu_kernel(a_ref, wg_ref, wu_ref, o_ref):
    a = a_ref[...]
    gate = jnp.dot(a, wg_ref[0], preferred_element_type=F32)
    up = jnp.dot(a, wu_ref[0], preferred_element_type=F32)
    o_ref[...] = (gate * jax.nn.sigmoid(gate) * up).astype(o_ref.dtype)


def _swiglu_up(h, w, n_rows):
    e, d, f2 = w.shape
    f = f2 // 2
    tm = _pick(n_rows, (1024, 512, 256, 128, 64))
    tn = _pick(f, (512, 256, 128))
    nj = f // tn
    return pl.pallas_call(
        _swiglu_kernel,
        out_shape=jax.ShapeDtypeStruct((n_rows, e * f), BF16),
        grid=(n_rows // tm, e, nj),
        in_specs=[pl.BlockSpec((tm, d), lambda i, x, j: (i, 0)),
                  pl.BlockSpec((1, d, tn), lambda i, x, j: (x, 0, j)),
                  pl.BlockSpec((1, d, tn), lambda i, x, j: (x, 0, nj + j))],
        out_specs=pl.BlockSpec((tm, tn), lambda i, x, j: (i, x * nj + j)),
        compiler_params=_params("arbitrary", "arbitrary", "arbitrary"),
        name="swiglu_up",
    )(h, w, w)


def _deepnorm_epilogue(alpha, y, x_ref, gate_ref, lng_ref, lnb_ref, next_refs, xo_ref, ho_ref, rows=slice(None)):
    xn = _ln_rows(alpha * x_ref[rows, :] + gate_ref[0] * y, lng_ref[...], lnb_ref[...])
    xo_ref[rows, :] = xn
    if ho_ref is not None:
        shn_ref, scn_ref = next_refs
        ho_ref[rows, :] = (xn * (1.0 + scn_ref[0]) + shn_ref[0]).astype(BF16)


MM_LN_COL_CHUNK = 1024
MM_LN_ROW_CHUNK = 128
MM_LN_X_SLOTS = 4
MM_LN_OUT_SLOTS = 2


def _mm_ln_kernel(alpha, nk, has_next, *refs):
    a_ref, b_ref, gate_ref, lng_ref, lnb_ref = refs[:5]
    next_refs = refs[5:7] if has_next else None
    refs = refs[7:] if has_next else refs[5:]
    if has_next:
        x_hbm, xo_hbm, ho_hbm, acc_ref, xbuf, obuf, hbuf, xsem, osem, hsem = refs
    else:
        x_hbm, xo_hbm, acc_ref, xbuf, obuf, xsem, osem = refs
        ho_hbm = hbuf = hsem = None
    i, k = pl.program_id(0), pl.program_id(1)
    tm, d = acc_ref.shape
    x_slots, rc = xbuf.shape[:2]
    o_slots = obuf.shape[0]
    n_chunks = tm // rc

    def hbm_rows(ref, c):
        return ref.at[pl.ds(pl.multiple_of(i * tm + c * rc, rc), rc), :]

    def x_copy(c, slot):
        return pltpu.make_async_copy(hbm_rows(x_hbm, c), xbuf.at[slot], xsem.at[slot])

    def o_copy(c, slot):
        return pltpu.make_async_copy(obuf.at[slot], hbm_rows(xo_hbm, c), osem.at[slot])

    def h_copy(c, slot):
        return pltpu.make_async_copy(hbuf.at[slot], hbm_rows(ho_hbm, c), hsem.at[slot])

    def wait_out(c, slot):
        o_copy(c, slot).wait()
        if has_next:
            h_copy(c, slot).wait()

    @pl.when(k == 0)
    def _():
        acc_ref[...] = jnp.zeros_like(acc_ref)

    @pl.when(k == nk - 1)
    def _():
        for c in range(min(x_slots, n_chunks)):
            x_copy(c, c).start()

    a = a_ref[...]
    cw = min(MM_LN_COL_CHUNK, d)
    for c in range(d // cw):
        cols = slice(c * cw, (c + 1) * cw)
        acc_ref[:, cols] += jnp.dot(a, b_ref[:, cols], preferred_element_type=F32)

    @pl.when(k == nk - 1)
    def _():
        def body(c, carry):
            xs, os_ = c % x_slots, c % o_slots
            x_copy(c, xs).wait()

            @pl.when(c >= o_slots)
            def _():
                wait_out(c - o_slots, os_)

            y = acc_ref[pl.ds(pl.multiple_of(c * rc, rc), rc), :]
            _deepnorm_epilogue(alpha, y, xbuf.at[xs], gate_ref, lng_ref, lnb_ref, next_refs, obuf.at[os_],
                               hbuf.at[os_] if has_next else None)
            o_copy(c, os_).start()
            if has_next:
                h_copy(c, os_).start()

            @pl.when(c + x_slots < n_chunks)
            def _():
                x_copy(c + x_slots, xs).start()

            return carry

        lax.fori_loop(0, n_chunks, body, 0)
        for c in range(max(n_chunks - o_slots, 0), n_chunks):
            wait_out(c, c % o_slots)


def _mm_ln(st, n_rows, alpha, a, b, x, mods, layer, gate_idx, lng, lnb, next_mod=None):
    kdim, d = b.shape
    tm = st.row_tile(1024)
    tk = _pick(kdim, (1024, 512, 256, 128))
    nk = kdim // tk
    rc = min(MM_LN_ROW_CHUNK, tm)
    has_next = next_mod is not None
    any_spec = pl.BlockSpec(memory_space=pl.ANY)
    in_specs = [pl.BlockSpec((tm, tk), lambda i, k: (i, k)), pl.BlockSpec((tk, d), lambda i, k: (k, 0)),
                st.mod_spec(layer, gate_idx, tm), _vec_spec(d), _vec_spec(d)]
    args = [a, b, mods, lng.reshape(1, d), lnb.reshape(1, d)]
    out_shape = [jax.ShapeDtypeStruct((n_rows, d), F32)]
    scratch = [pltpu.VMEM((tm, d), F32), pltpu.VMEM((MM_LN_X_SLOTS, rc, d), F32),
               pltpu.VMEM((MM_LN_OUT_SLOTS, rc, d), F32)]
    sems = [pltpu.SemaphoreType.DMA((MM_LN_X_SLOTS,)), pltpu.SemaphoreType.DMA((MM_LN_OUT_SLOTS,))]
    if has_next:
        nl, n_sh, n_sc = next_mod
        in_specs += [st.mod_spec(nl, n_sh, tm), st.mod_spec(nl, n_sc, tm)]
        args += [mods, mods]
        out_shape.append(jax.ShapeDtypeStruct((n_rows, d), BF16))
        scratch.append(pltpu.VMEM((MM_LN_OUT_SLOTS, rc, d), BF16))
        sems.append(pltpu.SemaphoreType.DMA((MM_LN_OUT_SLOTS,)))
    res = pl.pallas_call(
        functools.partial(_mm_ln_kernel, alpha, nk, has_next),
        out_shape=tuple(out_shape),
        grid=(n_rows // tm, nk),
        in_specs=in_specs + [any_spec],
        out_specs=tuple(any_spec for _ in out_shape),
        scratch_shapes=scratch + sems,
        compiler_params=_params("arbitrary", "arbitrary"),
        name="matmul_deepnorm",
    )(*args, x)
    return res if has_next else (res[0], None)


def _attn_kernel(n_lat_tiles, group, q_ref, kl_ref, vl_ref, kc_ref, vc_ref, o_ref, k_all, v_all):
    qi = pl.program_id(2)
    seq, ctx_len = kl_ref.shape[0], kc_ref.shape[0]

    @pl.when(qi == 0)
    def _():
        k_all[pl.ds(0, seq), :] = kl_ref[...]
        k_all[pl.ds(seq, ctx_len), :] = kc_ref[...]
        v_all[pl.ds(0, seq), pl.ds(0, HEAD_DIM)] = vl_ref[...]
        v_all[pl.ds(seq, ctx_len), pl.ds(0, HEAD_DIM)] = vc_ref[...]
        v_all[:, pl.ds(HEAD_DIM, HEAD_DIM)] = jnp.ones((seq + ctx_len, HEAD_DIM), v_all.dtype)

    def attend(k, v):
        for g in range(group):
            cols = slice(g * HEAD_DIM, (g + 1) * HEAD_DIM)
            s = lax.dot_general(q_ref[:, cols], k, (((1,), (1,)), ((), ())), preferred_element_type=F32)
            p = jnp.exp2(s - jnp.max(s, axis=-1, keepdims=True)).astype(v.dtype)
            o = jnp.dot(p, v, preferred_element_type=F32)
            o_ref[:, cols] = (o[:, :HEAD_DIM] / o[:, HEAD_DIM:]).astype(o_ref.dtype)

    @pl.when(qi < n_lat_tiles)
    def _():
        attend(k_all[...], v_all[...])

    @pl.when(qi >= n_lat_tiles)
    def _():
        attend(k_all[pl.ds(seq, ctx_len), :], v_all[pl.ds(seq, ctx_len), :])


def _attention(st, qkv, d):
    n_rows, n = qkv.shape
    kv = (n - d) // 2
    n_kv = kv // HEAD_DIM
    group = d // kv
    tq = min(256, st.ctx_len)
    ctx_tiles = st.ctx_len // tq
    lat_tiles = st.seq // tq
    qw = group * HEAD_DIM
    k_col0 = d // HEAD_DIM
    v_col0 = (d + kv) // HEAD_DIM

    def q_map(b, h, qi):
        row = jnp.where(qi < lat_tiles, b * lat_tiles + qi, st.n_lat // tq + b * ctx_tiles + (qi - lat_tiles))
        return (row, h)

    ctx_blk0 = st.n_lat // st.ctx_len
    return pl.pallas_call(
        functools.partial(_attn_kernel, lat_tiles, group),
        out_shape=jax.ShapeDtypeStruct((n_rows, d), BF16),
        grid=(st.batch, n_kv, lat_tiles + ctx_tiles),
        in_specs=[pl.BlockSpec((tq, qw), q_map),
                  pl.BlockSpec((st.seq, HEAD_DIM), lambda b, h, qi: (b, k_col0 + h)),
                  pl.BlockSpec((st.seq, HEAD_DIM), lambda b, h, qi: (b, v_col0 + h)),
                  pl.BlockSpec((st.ctx_len, HEAD_DIM), lambda b, h, qi: (ctx_blk0 + b, k_col0 + h)),
                  pl.BlockSpec((st.ctx_len, HEAD_DIM), lambda b, h, qi: (ctx_blk0 + b, v_col0 + h))],
        out_specs=pl.BlockSpec((tq, qw), q_map),
        scratch_shapes=[pltpu.VMEM((st.seq + st.ctx_len, HEAD_DIM), qkv.dtype),
                        pltpu.VMEM((st.seq + st.ctx_len, 2 * HEAD_DIM), qkv.dtype)],
        compiler_params=_params("arbitrary", "arbitrary", "arbitrary"),
        name="gqa_attention",
    )(qkv, qkv, qkv, qkv, qkv)


def _sigmoid(x):
    return 0.5 * jnp.tanh(0.5 * x) + 0.5


def _scan8(a, b, reverse):
    row = lax.broadcasted_iota(jnp.int32, a.shape, 0)
    for s in (1, 2, 4):
        if reverse:
            keep = row < SUBLANES - s
            shift = SUBLANES - s
        else:
            keep = row >= s
            shift = s
        a_sh = jnp.where(keep, pltpu.roll(a, shift, 0), 1.0)
        b_sh = jnp.where(keep, pltpu.roll(b, shift, 0), 0.0)
        b = a * b_sh + b
        a = a * a_sh
    return a, b


def _lru_kernel(tm, lat_tiles, reverse, *refs):
    if reverse:
        (xp_ref, x_ref, xn_ref, cw_ref, cb_ref, gw_ref, gb_ref, lam_ref, recf_ref, gelu_ref,
         o_ref, ext_ref, a_ref, b_ref, carry_ref) = refs
    else:
        (xp_ref, x_ref, xn_ref, cw_ref, cb_ref, gw_ref, gb_ref, lam_ref,
         o_ref, ext_ref, a_ref, b_ref, carry_ref) = refs
    s = pl.program_id(1)
    is_ctx = s == 0
    tile = lat_tiles - s if reverse else s - 1
    first = jnp.logical_or(is_ctx, tile == 0)
    last = jnp.logical_or(is_ctx, tile == lat_tiles - 1)

    @pl.when(is_ctx)
    def _():
        carry_ref[...] = jnp.zeros_like(carry_ref)

    ext_ref[pl.ds(HALO, tm), :] = x_ref[...]
    ext_ref[pl.ds(0, HALO), :] = jnp.where(first, 0.0, xp_ref[...])
    ext_ref[pl.ds(HALO + tm, HALO), :] = jnp.where(last, 0.0, xn_ref[...])

    heads, bw, _ = gw_ref.shape
    lam = lam_ref[...]
    softplus = jnp.maximum(-lam, 0.0) + jnp.log1p(jnp.exp(-jnp.abs(lam)))
    for h in range(heads):
        cols = pl.ds(h * bw, bw)
        xr = cb_ref[:, cols] + ext_ref[pl.ds(HALO - CONV_LEFT, tm), cols] * cw_ref[0:1, cols]
        for k in range(1, CONV_W):
            xr = xr + ext_ref[pl.ds(HALO - CONV_LEFT + k, tm), cols] * cw_ref[k:k + 1, cols]
        g = jnp.dot(xr.astype(BF16), gw_ref[h], preferred_element_type=F32)
        r = _sigmoid(g[:, :bw] + gb_ref[0:1, cols])
        i = _sigmoid(g[:, bw:] + gb_ref[1:2, cols])
        log_a = -LRU_C * r * softplus[:, h * bw:(h + 1) * bw]
        a = jnp.exp(log_a)
        a_ref[:, cols] = a
        b_ref[:, cols] = jnp.sqrt(1.0 - a * a) * (i * xr)

    n_chunks = tm // SUBLANES

    def body(c, carry):
        c = n_chunks - 1 - c if reverse else c
        rows = pl.ds(pl.multiple_of(c * SUBLANES, SUBLANES), SUBLANES)
        a_cum, b_cum = _scan8(a_ref[rows, :], b_ref[rows, :], reverse)
        hs = a_cum * carry + b_cum
        if reverse:
            o_ref[rows, :] = (gelu_ref[rows, :] * (recf_ref[rows, :] + hs)).astype(o_ref.dtype)
            edge = hs[0:1, :]
        else:
            o_ref[rows, :] = hs
            edge = hs[SUBLANES - 1:SUBLANES, :]
        return jnp.broadcast_to(edge, carry.shape)

    carry_ref[...] = lax.fori_loop(0, n_chunks, body, carry_ref[...])


def _lru_scan(st, yx, conv_w, conv_b, gate_w, gate_b, lam, reverse, rec_f=None):
    n_rows = yx.shape[0]
    d = st.d
    tm = min(256, st.ctx_len)
    assert st.ctx_len == tm
    lat_tiles = st.seq // tm
    ctx_blk0 = st.n_lat // tm
    per = tm // HALO
    last_halo = n_rows // HALO - 1

    def blk(b, s):
        lat = b * lat_tiles + (lat_tiles - s if reverse else s - 1)
        return jnp.where(s == 0, ctx_blk0 + b, lat)

    heads = gate_w.shape[1]
    bw = d // heads
    gw = jnp.concatenate([gate_w[0], gate_w[1]], axis=-1).astype(BF16)
    in_specs = [pl.BlockSpec((HALO, d), lambda b, s: (jnp.maximum(blk(b, s) * per - 1, 0), 1)),
                pl.BlockSpec((tm, d), lambda b, s: (blk(b, s), 1)),
                pl.BlockSpec((HALO, d), lambda b, s: (jnp.minimum((blk(b, s) + 1) * per, last_halo), 1)),
                pl.BlockSpec((CONV_W, d), lambda b, s: (0, 0)), _vec_spec(d),
                pl.BlockSpec((heads, bw, 2 * bw), lambda b, s: (0, 0, 0)),
                pl.BlockSpec((2, d), lambda b, s: (0, 0)), _vec_spec(d)]
    args = [yx, yx, yx, conv_w, conv_b.reshape(1, d), gw, gate_b.reshape(2, d), lam.reshape(1, d)]
    if reverse:
        in_specs += [pl.BlockSpec((tm, d), lambda b, s: (blk(b, s), 0)),
                     pl.BlockSpec((tm, d), lambda b, s: (blk(b, s), 0))]
        args += [rec_f, yx]
    return pl.pallas_call(
        functools.partial(_lru_kernel, tm, lat_tiles, reverse),
        out_shape=jax.ShapeDtypeStruct((n_rows, d), BF16 if reverse else F32),
        grid=(st.batch, lat_tiles + 1),
        in_specs=in_specs,
        out_specs=pl.BlockSpec((tm, d), lambda b, s: (blk(b, s), 0)),
        scratch_shapes=[pltpu.VMEM((tm + 2 * HALO, d), F32), pltpu.VMEM((tm, d), F32),
                        pltpu.VMEM((tm, d), F32), pltpu.VMEM((SUBLANES, d), F32)],
        compiler_params=_params("arbitrary", "arbitrary"),
        name="rglru_reverse" if reverse else "rglru_forward",
    )(*args)


ROUTE_LANES = 8
MOE_ROW_TILE = 512


def _lane_pick(rec, k):
    lane = lax.broadcasted_iota(jnp.int32, rec.shape, 1)
    return jnp.sum(jnp.where(lane == k, rec, 0.0), axis=1, keepdims=True)


def _router_kernel(x_ref, sh_ref, sc_ref, w_ref, route_ref, cnt_ref, run_ref):
    @pl.when(pl.program_id(0) == 0)
    def _():
        run_ref[...] = jnp.zeros_like(run_ref)

    h = x_ref[...] * (1.0 + sc_ref[0]) + sh_ref[0]
    logits = jnp.dot(h, w_ref[...], preferred_element_type=F32, precision=HIGHEST)
    tm, n_exp = logits.shape
    lane = lax.broadcasted_iota(jnp.int32, logits.shape, 1).astype(F32)
    m1 = jnp.max(logits, axis=-1, keepdims=True)
    i1 = jnp.min(jnp.where(logits == m1, lane, float(n_exp)), axis=-1, keepdims=True)
    pick1 = lane == i1
    rest = jnp.where(pick1, -jnp.inf, logits)
    m2 = jnp.max(rest, axis=-1, keepdims=True)
    i2 = jnp.min(jnp.where(rest == m2, lane, float(n_exp)), axis=-1, keepdims=True)
    pick2 = lane == i2
    e2 = jnp.exp(m2 - m1)
    den = 1.0 + e2
    onehot = jnp.where(pick1, 1.0, jnp.where(pick2, 1.0, 0.0))
    earlier = (lax.broadcasted_iota(jnp.int32, (tm, tm), 0) > lax.broadcasted_iota(jnp.int32, (tm, tm), 1))
    before = jnp.dot(jnp.where(earlier, 1.0, 0.0).astype(BF16), onehot.astype(BF16),
                     preferred_element_type=F32) + run_ref[...]
    rank1 = jnp.sum(jnp.where(pick1, before, 0.0), axis=-1, keepdims=True)
    rank2 = jnp.sum(jnp.where(pick2, before, 0.0), axis=-1, keepdims=True)
    rec = jnp.zeros((tm, ROUTE_LANES), F32)
    for k, v in enumerate((i1, i2, 1.0 / den, e2 / den, rank1, rank2)):
        rec = jnp.where(lax.broadcasted_iota(jnp.int32, rec.shape, 1) == k, v, rec)
    route_ref[...] = rec
    run_ref[...] += jnp.sum(onehot, axis=0, keepdims=True)
    cnt_ref[...] = run_ref[...]


def _router(st, n_rows, x, mods, layer, w):
    d, n_exp = w.shape
    tm = min(256, st.ctx_len)
    return pl.pallas_call(
        _router_kernel,
        out_shape=(jax.ShapeDtypeStruct((n_rows, ROUTE_LANES), F32), jax.ShapeDtypeStruct((1, n_exp), F32)),
        grid=(n_rows // tm,),
        in_specs=[pl.BlockSpec((tm, d), lambda i: (i, 0)), st.mod_spec(layer, 3, tm), st.mod_spec(layer, 4, tm),
                  pl.BlockSpec((d, n_exp), lambda i: (0, 0))],
        out_specs=(pl.BlockSpec((tm, ROUTE_LANES), lambda i: (i, 0)), pl.BlockSpec((1, n_exp), lambda i: (0, 0))),
        scratch_shapes=[pltpu.VMEM((1, n_exp), F32)],
        compiler_params=_params("arbitrary"),
        name="moe_router",
    )(x, mods, mods, w)


def _dispatch_plan(route, counts, tm):
    n_tok = route.shape[0]
    n_exp = counts.shape[1]
    expert = route[:, 0:2].astype(jnp.int32)
    rank = route[:, 4:6].astype(jnp.int32)
    cnt = counts[0].astype(jnp.int32)
    padded = (cnt + tm - 1) // tm * tm
    ends = jnp.cumsum(padded)
    starts = ends - padded
    dest = (starts[expert] + rank).T.reshape(2 * n_tok)
    n_tiles = (2 * n_tok + n_exp * tm) // tm
    tile_row0 = jnp.arange(n_tiles, dtype=jnp.int32) * tm
    tile_expert = jnp.minimum(jnp.sum(tile_row0[:, None] >= ends[None, :], axis=1), n_exp - 1).astype(jnp.int32)
    n_used = (ends[-1:] // tm).astype(jnp.int32)
    return dest, tile_expert, n_used, n_tiles


HI16 = 0xFFFF0000


def _pack_bf16_pairs(lo, hi):
    lo = lax.bitcast_convert_type(lo.astype(jnp.bfloat16).astype(F32), jnp.uint32)
    hi = lax.bitcast_convert_type(hi.astype(jnp.bfloat16).astype(F32), jnp.uint32)
    return (lo >> 16) | (hi & jnp.uint32(HI16))


def _unpack_bf16_pairs(p):
    return lax.bitcast_convert_type(p << 16, F32), lax.bitcast_convert_type(p & jnp.uint32(HI16), F32)


def _dispatch_kernel(tm, n_tok, dest_ref, x_ref, sh_ref, sc_ref, init_ref, o_ref, h_ref, sem):
    del init_ref
    base = pl.program_id(0) * tm
    h = x_ref[...] * (1.0 + sc_ref[0]) + sh_ref[0]
    half = h.shape[1] // 2
    h_ref[...] = _pack_bf16_pairs(h[:, :half], h[:, half:])

    def row_copy(r, d):
        return pltpu.make_async_copy(h_ref.at[pl.ds(r, 1), :], o_ref.at[pl.ds(d, 1), :], sem)

    def issue(r, carry):
        row_copy(r, dest_ref[base + r]).start()
        row_copy(r, dest_ref[n_tok + base + r]).start()
        return carry

    def drain(r, carry):
        row_copy(r, 0).wait()
        row_copy(r, 0).wait()
        return carry

    lax.fori_loop(0, tm, issue, 0)
    lax.fori_loop(0, tm, drain, 0)


def _dispatch(st, n_rows, x, mods, layer, dest, n_out):
    d = st.d
    tm = min(256, st.ctx_len)
    grid_spec = pltpu.PrefetchScalarGridSpec(
        num_scalar_prefetch=1,
        grid=(n_rows // tm,),
        in_specs=[pl.BlockSpec((tm, d), lambda i, dest: (i, 0)), st.mod_spec(layer, 3, tm), st.mod_spec(layer, 4, tm),
                  pl.BlockSpec(memory_space=pl.ANY)],
        out_specs=pl.BlockSpec(memory_space=pl.ANY),
        scratch_shapes=[pltpu.VMEM((tm, d // 2), jnp.uint32), pltpu.SemaphoreType.DMA],
    )
    return pl.pallas_call(
        functools.partial(_dispatch_kernel, tm, n_rows),
        out_shape=jax.ShapeDtypeStruct((n_out, d // 2), jnp.uint32),
        grid_spec=grid_spec,
        input_output_aliases={4: 0},
        compiler_params=_params("arbitrary"),
        name="moe_dispatch",
    )(dest, x, mods, mods, jnp.zeros((n_out, d // 2), jnp.uint32))


def _moe_up_kernel(te_ref, nu_ref, a_ref, wg_ref, wu_ref, o_ref):
    i = pl.program_id(1)

    @pl.when(i < nu_ref[0])
    def _():
        lo, hi = (t.astype(jnp.bfloat16) for t in _unpack_bf16_pairs(a_ref[...]))
        half = lo.shape[1]

        def proj(w_ref):
            return (jnp.dot(lo, w_ref[0, :half, :], preferred_element_type=F32)
                    + jnp.dot(hi, w_ref[0, half:, :], preferred_element_type=F32))

        gate, up = proj(wg_ref), proj(wu_ref)
        o_ref[...] = (gate * jax.nn.sigmoid(gate) * up).astype(o_ref.dtype)

    @pl.when(i >= nu_ref[0])
    def _():
        o_ref[...] = jnp.zeros_like(o_ref)


def _moe_up(xg, w, tile_expert, n_used, tm):
    n_out = xg.shape[0]
    d, f2 = w.shape[1:]
    f = f2 // 2
    tn = _pick(f, (512, 256, 128))
    nj = f // tn
    grid_spec = pltpu.PrefetchScalarGridSpec(
        num_scalar_prefetch=2,
        grid=(nj, n_out // tm),
        in_specs=[pl.BlockSpec((tm, d // 2), lambda j, i, te, nu: (i, 0)),
                  pl.BlockSpec((1, d, tn), lambda j, i, te, nu: (te[i], 0, j)),
                  pl.BlockSpec((1, d, tn), lambda j, i, te, nu: (te[i], 0, nj + j))],
        out_specs=pl.BlockSpec((tm, tn), lambda j, i, te, nu: (i, j)),
    )
    return pl.pallas_call(
        _moe_up_kernel,
        out_shape=jax.ShapeDtypeStruct((n_out, f), BF16),
        grid_spec=grid_spec,
        compiler_params=_params("arbitrary", "arbitrary"),
        name="moe_up",
    )(tile_expert, n_used, xg, w, w)


def _moe_down_kernel(te_ref, nu_ref, a_ref, w_ref, o_ref):
    i = pl.program_id(1)

    @pl.when(i < nu_ref[0])
    def _():
        a = a_ref[...]
        half = w_ref.shape[2] // 2
        o_ref[...] = _pack_bf16_pairs(jnp.dot(a, w_ref[0, :, :half], preferred_element_type=F32),
                                      jnp.dot(a, w_ref[0, :, half:], preferred_element_type=F32))

    @pl.when(i >= nu_ref[0])
    def _():
        o_ref[...] = jnp.zeros_like(o_ref)


def _moe_down(act, w, tile_expert, n_used, tm):
    n_out, f = act.shape
    d = w.shape[2]
    grid_spec = pltpu.PrefetchScalarGridSpec(
        num_scalar_prefetch=2,
        grid=(1, n_out // tm),
        in_specs=[pl.BlockSpec((tm, f), lambda j, i, te, nu: (i, 0)),
                  pl.BlockSpec((1, f, d), lambda j, i, te, nu: (te[i], 0, 0))],
        out_specs=pl.BlockSpec((tm, d // 2), lambda j, i, te, nu: (i, 0)),
    )
    return pl.pallas_call(
        _moe_down_kernel,
        out_shape=jax.ShapeDtypeStruct((n_out, d // 2), jnp.uint32),
        grid_spec=grid_spec,
        compiler_params=_params("arbitrary", "arbitrary"),
        name="moe_down",
    )(tile_expert, n_used, act, w)


COMBINE_ROW_CHUNK = 64


def _combine_kernel(tm, n_tok, alpha, has_next, dest_ref, *refs):
    route_ref, x_ref, gate_ref, lng_ref, lnb_ref = refs[:5]
    next_refs = refs[5:7] if has_next else None
    refs = refs[7:] if has_next else refs[5:]
    y_ref, xo_ref = refs[:2]
    ho_ref = refs[2] if has_next else None
    buf_ref, sem = refs[-2:]
    i = pl.program_id(0)
    n_steps = pl.num_programs(0)
    slot = i % 2

    def row_copy(slot, which, r, src):
        return pltpu.make_async_copy(y_ref.at[pl.ds(src, 1), :], buf_ref.at[slot, which, pl.ds(r, 1), :],
                                     sem.at[slot])

    def gather(tile, slot):
        def issue(r, carry):
            row_copy(slot, 0, r, dest_ref[tile * tm + r]).start()
            row_copy(slot, 1, r, dest_ref[n_tok + tile * tm + r]).start()
            return carry
        lax.fori_loop(0, tm, issue, 0)

    @pl.when(i == 0)
    def _():
        gather(0, 0)

    @pl.when(i + 1 < n_steps)
    def _():
        gather(i + 1, 1 - slot)

    def drain(r, carry):
        row_copy(slot, 0, r, 0).wait()
        row_copy(slot, 1, r, 0).wait()
        return carry

    lax.fori_loop(0, tm, drain, 0)
    rc = min(COMBINE_ROW_CHUNK, tm)

    def body(c, carry):
        rows = pl.ds(pl.multiple_of(c * rc, rc), rc)
        rec = route_ref[rows, :]
        g1, g2 = _lane_pick(rec, 2), _lane_pick(rec, 3)
        lo1, hi1 = _unpack_bf16_pairs(buf_ref[slot, 0, rows, :])
        lo2, hi2 = _unpack_bf16_pairs(buf_ref[slot, 1, rows, :])
        y = jnp.concatenate([g1 * lo1 + g2 * lo2, g1 * hi1 + g2 * hi2], axis=1)
        _deepnorm_epilogue(alpha, y, x_ref, gate_ref, lng_ref, lnb_ref, next_refs, xo_ref, ho_ref, rows)
        return carry

    lax.fori_loop(0, tm // rc, body, 0)


def _combine(st, n_rows, alpha, yg, route, dest, x, mods, layer, lng, lnb, next_mod=None):
    d = st.d
    tm = min(256, st.ctx_len)
    has_next = next_mod is not None
    row = lambda i, dest: (i, 0)
    in_specs = [pl.BlockSpec((tm, ROUTE_LANES), row), pl.BlockSpec((tm, d), row), st.mod_spec(layer, 5, tm),
                _vec_spec(d), _vec_spec(d)]
    args = [route, x, mods, lng.reshape(1, d), lnb.reshape(1, d)]
    out_shape = [jax.ShapeDtypeStruct((n_rows, d), F32)]
    out_specs = [pl.BlockSpec((tm, d), row)]
    if has_next:
        nl, n_sh, n_sc = next_mod
        in_specs += [st.mod_spec(nl, n_sh, tm), st.mod_spec(nl, n_sc, tm)]
        args += [mods, mods]
        out_shape.append(jax.ShapeDtypeStruct((n_rows, d), BF16))
        out_specs.append(pl.BlockSpec((tm, d), row))
    in_specs.append(pl.BlockSpec(memory_space=pl.ANY))
    args.append(yg)
    grid_spec = pltpu.PrefetchScalarGridSpec(
        num_scalar_prefetch=1,
        grid=(n_rows // tm,),
        in_specs=in_specs,
        out_specs=tuple(out_specs),
        scratch_shapes=[pltpu.VMEM((2, 2, tm, d // 2), jnp.uint32), pltpu.SemaphoreType.DMA((2,))],
    )
    res = pl.pallas_call(
        functools.partial(_combine_kernel, tm, n_rows, alpha, has_next),
        out_shape=tuple(out_shape),
        grid_spec=grid_spec,
        compiler_params=_params("arbitrary"),
        name="moe_combine",
    )(dest, *args)
    return res if has_next else (res[0], None)


def _moe_layer(st, n_rows, alpha, x, mods, layer, router_w, w_in, w_out, lng, lnb, next_mod):
    route, counts = _router(st, n_rows, x, mods, layer, router_w)
    dest, tile_expert, n_used, n_tiles = _dispatch_plan(route, counts, MOE_ROW_TILE)
    xg = _dispatch(st, n_rows, x, mods, layer, dest, n_tiles * MOE_ROW_TILE)
    act = _moe_up(xg, w_in, tile_expert, n_used, MOE_ROW_TILE)
    yg = _moe_down(act, w_out, tile_expert, n_used, MOE_ROW_TILE)
    return _combine(st, n_rows, alpha, yg, route, dest, x, mods, layer, lng, lnb, next_mod)


def _cast_kernel(is_pad, x_ref, o_ref):
    pad = is_pad(pl.program_id(0), pl.program_id(1))

    @pl.when(jnp.logical_not(pad))
    def _():
        o_ref[...] = x_ref[...].astype(o_ref.dtype)

    @pl.when(pad)
    def _():
        o_ref[...] = jnp.zeros_like(o_ref)


def _cast_blocks(w, layer, block, out_blocks, src_of, n_src):
    def in_map(i, j):
        bi, bj = src_of(i, j)
        return (layer, jnp.minimum(bi, n_src[0] - 1), jnp.minimum(bj, n_src[1] - 1))

    def is_pad(i, j):
        bi, bj = src_of(i, j)
        return jnp.logical_or(bi >= n_src[0], bj >= n_src[1])

    return pl.pallas_call(
        functools.partial(_cast_kernel, is_pad),
        out_shape=jax.ShapeDtypeStruct((out_blocks[0] * block[0], out_blocks[1] * block[1]), BF16),
        grid=out_blocks,
        in_specs=[pl.BlockSpec((None,) + block, in_map)],
        out_specs=pl.BlockSpec(block, lambda i, j: (i, j)),
        compiler_params=_params("arbitrary", "arbitrary"),
        name="cast_weights",
    )(w)


def _cast(w, layer):
    w3 = w.reshape(w.shape[0], -1, w.shape[-1])
    rows, cols = w3.shape[1:]
    block = (_pick(rows, (512, 256, 128, 64, 32, 16)), _pick(cols, (4096, 2048, 1024, 512, 256, 128)))
    n = (rows // block[0], cols // block[1])
    return _cast_blocks(w3, layer, block, n, lambda i, j: (i, j), n).reshape(w.shape[1:])


FFN_PAD_BLOCK = 256


def _cast_pad_ffn(w_in, w_out, layer, mult):
    d, f2 = w_in.shape[1:]
    f = f2 // 2
    blk = _pick(f, (FFN_PAD_BLOCK, 128))
    fp = _round_up(f, mult)
    nb, nbp = f // blk, fp // blk
    tr = _pick(d, (2048, 1024, 512, 256, 128))

    def in_src(i, j):
        up = j - nbp
        return (i, jnp.where(j < nbp, jnp.where(j < nb, j, 2 * nb), jnp.where(up < nb, up + nb, 2 * nb)))

    w_in_p = _cast_blocks(w_in, layer, (tr, blk), (d // tr, 2 * nbp), in_src, (d // tr, 2 * nb))
    dc = _pick(d, (4096, 2048, 1024, 512, 256, 128))
    w_out_p = _cast_blocks(w_out, layer, (blk, dc), (nbp, d // dc), lambda i, j: (i, j), (nb, d // dc))
    return w_in_p.reshape(1, d, 2 * fp), w_out_p


def kernel(x, c, ctx, c_ctx, ada_down, ada_up, ada_b, ln_g, ln_b, pool_w, pool_scale, attn_wqkv, attn_q_gain,
           attn_k_gain, attn_wo, lru_w_in, lru_conv_w, lru_conv_b, lru_gate_w, lru_gate_b, lru_lambda, lru_w_out,
           ffn_w_in, ffn_w_out, moe_router, moe_w_in, moe_w_out):
    batch, seq, d = x.shape
    ctx_len = ctx.shape[1]
    depth = ada_down.shape[0]
    assert batch < MOD_ROWS and seq % ctx_len == 0 and ctx_len % SUBLANES == 0
    st = _Stream(batch, seq, ctx_len, d)
    alpha = (2.0 * depth) ** 0.25
    mixers = [i % N_MIXERS for i in range(depth)]
    is_moe = [i % 2 == 1 for i in range(depth)]
    ctx_needed_after = [any(mixers[j] != 0 for j in range(i + 1, depth)) for i in range(depth)]

    cvec = jnp.concatenate([c, c_ctx[None], jnp.zeros((MOD_ROWS - batch - 1, d), F32)], axis=0)
    mods = _adaln(cvec, ada_down, ada_up, ada_b)

    xs = jnp.concatenate([x.reshape(batch * seq, d), ctx.reshape(batch * ctx_len, d)], axis=0)
    n_rows = st.n_all
    h = None
    for i in range(depth):
        mixer = mixers[i]
        mi = mixers[:i].count(mixer)
        fi = is_moe[:i].count(is_moe[i])
        lng, lnb = ln_g[i], ln_b[i]
        ffn_mod = None if is_moe[i] else (i, 3, 4)
        if not ctx_needed_after[i] and mixer == 0 and n_rows != st.n_lat:
            n_rows = st.n_lat
            xs = xs[:n_rows]

        if mixer == 0:
            xs, hf = _pool_layer(st, n_rows, i, alpha, xs, mods, _cast(pool_w, mi), pool_scale[mi],
                                 lng[0], lnb[0])
        elif mixer == 1:
            qkv = _qkv_proj(st, h, _cast(attn_wqkv, mi), attn_q_gain[mi], attn_k_gain[mi])
            o = _attention(st, qkv, d)
            rows_out = n_rows if ctx_needed_after[i] else st.n_lat
            xs, hf = _mm_ln(st, rows_out, alpha, o, _cast(attn_wo, mi), xs, mods, i, 2, lng[0], lnb[0],
                            next_mod=ffn_mod)
            n_rows = rows_out
        else:
            yx = _lru_in_proj(h, _cast(lru_w_in, mi))
            rec_f = _lru_scan(st, yx, lru_conv_w[mi], lru_conv_b[mi], lru_gate_w[mi, 0], lru_gate_b[mi, 0],
                              lru_lambda[mi, 0], False)
            m = _lru_scan(st, yx, lru_conv_w[mi], lru_conv_b[mi], lru_gate_w[mi, 1], lru_gate_b[mi, 1],
                          lru_lambda[mi, 1], True, rec_f)
            rows_out = n_rows if ctx_needed_after[i] else st.n_lat
            xs, hf = _mm_ln(st, rows_out, alpha, m, _cast(lru_w_out, mi), xs, mods, i, 2, lng[0], lnb[0],
                            next_mod=ffn_mod)
            n_rows = rows_out

        next_mod = (i + 1, 0, 1) if i + 1 < depth and mixers[i + 1] != 0 else None
        if is_moe[i]:
            xs, h = _moe_layer(st, n_rows, alpha, xs, mods, i, moe_router[fi], _cast(moe_w_in, fi),
                               _cast(moe_w_out, fi), lng[1], lnb[1], next_mod)
        else:
            w_in_p, w_out_p = _cast_pad_ffn(ffn_w_in, ffn_w_out, fi, 1024)
            act = _swiglu_up(hf, w_in_p, n_rows)
            xs, h = _mm_ln(st, n_rows, alpha, act, w_out_p, xs, mods, i, 5, lng[1], lnb[1], next_mod=next_mod)
    return xs[:st.n_lat].reshape(batch, seq, d)
```

```python
import functools

import jax
import jax.numpy as jnp
from jax import lax
from jax.experimental import pallas as pl
from jax.experimental.pallas import tpu as pltpu

F32 = jnp.float32
BF16 = jnp.bfloat16
HIGHEST = lax.Precision.HIGHEST

HEAD_DIM = 128
AXIS_DIM = HEAD_DIM // 2
GRID_W = 64
ROPE_THETA = 10000.0
LN_EPS = 1e-6
RMS_EPS = 1e-6
LRU_C = 8.0
LOG2E = 1.4426950408889634
CONV_W = 4
CONV_LEFT = CONV_W // 2
POOL_WINDOWS = (2, 4, 8, 16)
N_MOD = 6
N_MIXERS = 3
MOD_ROWS = 8

SUBLANES = 8
LANES = 128
VMEM_LIMIT_BYTES = 56 * 1024 * 1024
HALO = SUBLANES


def _params(*sem):
    return pltpu.CompilerParams(dimension_semantics=sem, vmem_limit_bytes=VMEM_LIMIT_BYTES)


def _pick(n, prefs):
    for p in prefs:
        if n % p == 0:
            return p
    return n


def _round_up(n, m):
    return (n + m - 1) // m * m


def _ln_rows(z, g, b):
    mu = jnp.mean(z, axis=-1, keepdims=True)
    zc = z - mu
    var = jnp.mean(zc * zc, axis=-1, keepdims=True)
    return zc * lax.rsqrt(var + LN_EPS) * g + b


class _Stream:
    def __init__(self, batch, seq, ctx_len, d):
        self.batch, self.seq, self.ctx_len, self.d = batch, seq, ctx_len, d
        self.n_lat = batch * seq
        self.n_all = self.n_lat + batch * ctx_len

    def row_tile(self, largest):
        t = largest
        while self.seq % t or (self.batch * self.ctx_len) % t:
            t //= 2
        return t

    def group(self, row0):
        return jnp.minimum(row0 // self.seq, self.batch)

    def mod_spec(self, layer, which, tm):
        def imap(i, *_):
            return ((layer * MOD_ROWS + self.group(i * tm)) * N_MOD + which, 0, 0)
        return pl.BlockSpec((1, 1, self.d), imap)

    def seq_pos(self, row0):
        is_ctx = row0 >= self.n_lat
        pos = jnp.where(is_ctx, (row0 - self.n_lat) % self.ctx_len, row0 % self.seq)
        return pos, jnp.where(is_ctx, self.ctx_len, self.seq)


def _vec_spec(d):
    return pl.BlockSpec((1, d), lambda *_: (0, 0))


def _halo_specs(tm, width, n_rows, col_block=0):
    per = tm // HALO
    last = n_rows // HALO - 1
    prev = pl.BlockSpec((HALO, width), lambda i, *_: (jnp.maximum(i * per - 1, 0), col_block))
    cur = pl.BlockSpec((tm, width), lambda i, *_: (i, col_block))
    nxt = pl.BlockSpec((HALO, width), lambda i, *_: (jnp.minimum((i + 1) * per, last), col_block))
    return prev, cur, nxt


def _adaln_kernel(c_ref, down_ref, up_ref, b_ref, o_ref):
    c = c_ref[...]
    s = c * jax.nn.sigmoid(c)
    t = jnp.dot(s, down_ref[0], preferred_element_type=F32, precision=HIGHEST)
    o_ref[0] = jnp.dot(t, up_ref[0], preferred_element_type=F32, precision=HIGHEST) + b_ref[0]


def _adaln(cvec, ada_down, ada_up, ada_b):
    depth, d, rank = ada_down.shape
    n = ada_up.shape[-1]
    tn = _pick(n, (4096, 2048, 1024, 512, 256, 128))
    out = pl.pallas_call(
        _adaln_kernel,
        out_shape=jax.ShapeDtypeStruct((depth, MOD_ROWS, n), F32),
        grid=(depth, n // tn),
        in_specs=[
            pl.BlockSpec((MOD_ROWS, d), lambda l, j: (0, 0)),
            pl.BlockSpec((1, d, rank), lambda l, j: (l, 0, 0)),
            pl.BlockSpec((1, rank, tn), lambda l, j: (l, 0, j)),
            pl.BlockSpec((1, 1, tn), lambda l, j: (l, 0, j)),
        ],
        out_specs=pl.BlockSpec((1, MOD_ROWS, tn), lambda l, j: (l, 0, j)),
        compiler_params=_params("arbitrary", "arbitrary"),
        name="adaln",
    )(cvec, ada_down, ada_up, ada_b.reshape(depth, 1, n))
    return out.reshape(depth * MOD_ROWS * N_MOD, 1, d)


POOL_ROW_CHUNK = 128


def _pool_kernel(st, tm, alpha, xp_ref, x_ref, xn_ref, sh_ref, sc_ref, gm_ref, shf_ref, scf_ref,
                 w_ref, ps_ref, lng_ref, lnb_ref, xo_ref, ho_ref, ext_ref, z_ref):
    row0 = pl.program_id(0) * tm
    pos0, seqlen = st.seq_pos(row0)
    first = pos0 == 0
    last = pos0 + tm == seqlen
    one_sc = 1.0 + sc_ref[0]
    sh = sh_ref[0]
    x = x_ref[...]
    ext_ref[pl.ds(HALO, tm), :] = x * one_sc + sh
    ext_ref[pl.ds(0, HALO), :] = jnp.where(first, 0.0, xp_ref[...] * one_sc + sh)
    ext_ref[pl.ds(HALO + tm, HALO), :] = jnp.where(last, 0.0, xn_ref[...] * one_sc + sh)

    pos = pos0 + lax.broadcasted_iota(jnp.int32, (tm, 1), 0)
    groups = len(POOL_WINDOWS)
    gw = st.d // groups
    for g, k in enumerate(POOL_WINDOWS):
        cols = pl.ds(g * gw, gw)
        lo = jnp.maximum(pos - k // 2, 0)
        hi = jnp.minimum(pos + (k - k // 2), seqlen)
        cnt = (hi - lo).astype(F32)
        win = ext_ref[pl.ds(HALO - k // 2, tm), cols]
        for j in range(1, k):
            win = win + ext_ref[pl.ds(HALO - k // 2 + j, tm), cols]
        diff = (win / cnt - ext_ref[pl.ds(HALO, tm), cols]).astype(BF16)
        z_ref[:, cols] = jnp.dot(diff, w_ref[g], preferred_element_type=F32) * ps_ref[:, cols]

    rc = min(POOL_ROW_CHUNK, tm)

    def body(c, carry):
        rows = pl.ds(pl.multiple_of(c * rc, rc), rc)
        _deepnorm_epilogue(alpha, z_ref[rows, :], x_ref, gm_ref, lng_ref, lnb_ref, (shf_ref, scf_ref), xo_ref, ho_ref,
                           rows)
        return carry

    lax.fori_loop(0, tm // rc, body, 0)


def _pool_layer(st, n_rows, layer, alpha, x, mods, w, pscale, lng, lnb):
    d = st.d
    tm = min(256, st.ctx_len)
    prev, cur, nxt = _halo_specs(tm, d, n_rows)
    groups, gw, _ = w.shape
    return pl.pallas_call(
        functools.partial(_pool_kernel, st, tm, alpha),
        out_shape=(jax.ShapeDtypeStruct((n_rows, d), F32), jax.ShapeDtypeStruct((n_rows, d), BF16)),
        grid=(n_rows // tm,),
        in_specs=[prev, cur, nxt] + [st.mod_spec(layer, q, tm) for q in (0, 1, 2, 3, 4)] + [
            pl.BlockSpec((groups, gw, gw), lambda i: (0, 0, 0)),
            _vec_spec(d), _vec_spec(d), _vec_spec(d)],
        out_specs=(pl.BlockSpec((tm, d), lambda i: (i, 0)), pl.BlockSpec((tm, d), lambda i: (i, 0))),
        scratch_shapes=[pltpu.VMEM((tm + 2 * HALO, d), F32), pltpu.VMEM((tm, d), F32)],
        compiler_params=_params("arbitrary"),
        name="pool_mixer",
    )(x, x, x, mods, mods, mods, mods, mods, w, pscale.reshape(1, d), lng.reshape(1, d), lnb.reshape(1, d))


def _qkv_kernel(n_qk_tiles, a_ref, b_ref, gain_ref, c_ref, s1_ref, s2_ref, o_ref):
    j = pl.program_id(1)
    acc = jnp.dot(a_ref[...], b_ref[...], preferred_element_type=F32)
    tn = acc.shape[1]

    @pl.when(j < n_qk_tiles)
    def _():
        cos, s1, s2 = c_ref[...], s1_ref[...], s2_ref[...]
        for hh in range(tn // HEAD_DIM):
            cols = slice(hh * HEAD_DIM, (hh + 1) * HEAD_DIM)
            xh = acc[:, cols]
            n = xh * lax.rsqrt(jnp.mean(xh * xh, axis=-1, keepdims=True) + RMS_EPS) * gain_ref[:, cols]
            half = AXIS_DIM // 2
            rot = n * cos + pltpu.roll(n, HEAD_DIM - half, 1) * s1 + pltpu.roll(n, half, 1) * s2
            o_ref[:, cols] = rot.astype(o_ref.dtype)

    @pl.when(j >= n_qk_tiles)
    def _():
        o_ref[...] = acc.astype(o_ref.dtype)


def _rope_tables(seq, tm):
    rows = seq // GRID_W
    r = jnp.repeat(jnp.arange(rows, dtype=F32), GRID_W)
    col = jnp.tile(jnp.arange(GRID_W, dtype=F32), rows)
    inv = ROPE_THETA ** (-jnp.arange(0, AXIS_DIM, 2, dtype=F32) / AXIS_DIM)
    ang_r = r[:, None] * inv
    ang_c = col[:, None] * inv
    zero = jnp.zeros_like(ang_r)
    cos = jnp.concatenate([jnp.cos(ang_r), jnp.cos(ang_r), jnp.cos(ang_c), jnp.cos(ang_c)], axis=-1)
    s1 = jnp.concatenate([-jnp.sin(ang_r), zero, -jnp.sin(ang_c), zero], axis=-1)
    s2 = jnp.concatenate([zero, jnp.sin(ang_r), zero, jnp.sin(ang_c)], axis=-1)
    ident = jnp.ones((tm, HEAD_DIM), F32)
    nul = jnp.zeros((tm, HEAD_DIM), F32)
    return (jnp.concatenate([cos, ident]), jnp.concatenate([s1, nul]), jnp.concatenate([s2, nul]))


def _qkv_proj(st, h, w, q_gain, k_gain):
    n_rows, d = h.shape
    n = w.shape[1]
    kv = (n - d) // 2
    tm = st.row_tile(1024)
    tn = _pick(kv, (1024, 512, 256, 128))
    scale = HEAD_DIM ** -0.5 * LOG2E
    gain = jnp.concatenate([jnp.tile(q_gain * scale, d // HEAD_DIM), jnp.tile(k_gain, kv // HEAD_DIM),
                            jnp.ones((kv,), F32)]).reshape(1, n)
    cos, s1, s2 = _rope_tables(st.seq, tm)
    per_seq = st.seq // tm

    def tab_map(i, j):
        return (jnp.where(i * tm < st.n_lat, i % per_seq, per_seq), 0)

    tab = pl.BlockSpec((tm, HEAD_DIM), tab_map)
    return pl.pallas_call(
        functools.partial(_qkv_kernel, (d + kv) // tn),
        out_shape=jax.ShapeDtypeStruct((n_rows, n), BF16),
        grid=(n_rows // tm, n // tn),
        in_specs=[pl.BlockSpec((tm, d), lambda i, j: (i, 0)), pl.BlockSpec((d, tn), lambda i, j: (0, j)),
                  pl.BlockSpec((1, tn), lambda i, j: (0, j)), tab, tab, tab],
        out_specs=pl.BlockSpec((tm, tn), lambda i, j: (i, j)),
        compiler_params=_params("arbitrary", "arbitrary"),
        name="qkv_proj",
    )(h, w, gain, cos, s1, s2)


def _gelu_tanh(x):
    return 0.5 * x * (1.0 + jnp.tanh(0.7978845608028654 * (x + 0.044715 * (x * x * x))))


def _lru_in_kernel(n_gelu_tiles, a_ref, b_ref, o_ref):
    j = pl.program_id(1)
    acc = jnp.dot(a_ref[...], b_ref[...], preferred_element_type=F32)

    @pl.when(j < n_gelu_tiles)
    def _():
        o_ref[...] = _gelu_tanh(acc)

    @pl.when(j >= n_gelu_tiles)
    def _():
        o_ref[...] = acc


def _lru_in_proj(h, w):
    n_rows, d = h.shape
    n = w.shape[1]
    tm = _pick(n_rows, (1024, 512, 256, 128, 64))
    tn = _pick(n // 2, (1024, 512, 256, 128))
    return pl.pallas_call(
        functools.partial(_lru_in_kernel, (n // 2) // tn),
        out_shape=jax.ShapeDtypeStruct((n_rows, n), F32),
        grid=(n_rows // tm, n // tn),
        in_specs=[pl.BlockSpec((tm, d), lambda i, j: (i, 0)), pl.BlockSpec((d, tn), lambda i, j: (0, j))],
        out_specs=pl.BlockSpec((tm, tn), lambda i, j: (i, j)),
        compiler_params=_params("arbitrary", "arbitrary"),
        name="lru_in_proj",
    )(h, w)


def _swiglu_kernel(a_ref, wg_ref, wu_ref, o_ref):
    a = a_ref[...]
    gate = jnp.dot(a, wg_ref[0], preferred_element_type=F32)
    up = jnp.dot(a, wu_ref[0], preferred_element_type=F32)
    o_ref[...] = (gate * jax.nn.sigmoid(gate) * up).astype(o_ref.dtype)


def _swiglu_up(h, w, n_rows):
    e, d, f2 = w.shape
    f = f2 // 2
    tm = _pick(n_rows, (1024, 512, 256, 128, 64))
    tn = _pick(f, (512, 256, 128))
    nj = f // tn
    return pl.pallas_call(
        _swiglu_kernel,
        out_shape=jax.ShapeDtypeStruct((n_rows, e * f), BF16),
        grid=(n_rows // tm, e, nj),
        in_specs=[pl.BlockSpec((tm, d), lambda i, x, j: (i, 0)),
                  pl.BlockSpec((1, d, tn), lambda i, x, j: (x, 0, j)),
                  pl.BlockSpec((1, d, tn), lambda i, x, j: (x, 0, nj + j))],
        out_specs=pl.BlockSpec((tm, tn), lambda i, x, j: (i, x * nj + j)),
        compiler_params=_params("arbitrary", "arbitrary", "arbitrary"),
        name="swiglu_up",
    )(h, w, w)


def _deepnorm_epilogue(alpha, y, x_ref, gate_ref, lng_ref, lnb_ref, next_refs, xo_ref, ho_ref, rows=slice(None)):
    xn = _ln_rows(alpha * x_ref[rows, :] + gate_ref[0] * y, lng_ref[...], lnb_ref[...])
    xo_ref[rows, :] = xn
    if ho_ref is not None:
        shn_ref, scn_ref = next_refs
        ho_ref[rows, :] = (xn * (1.0 + scn_ref[0]) + shn_ref[0]).astype(BF16)


MM_LN_COL_CHUNK = 1024
MM_LN_ROW_CHUNK = 128
MM_LN_X_SLOTS = 4
MM_LN_OUT_SLOTS = 2


def _mm_ln_kernel(alpha, nk, has_next, *refs):
    a_ref, b_ref, gate_ref, lng_ref, lnb_ref = refs[:5]
    next_refs = refs[5:7] if has_next else None
    refs = refs[7:] if has_next else refs[5:]
    if has_next:
        x_hbm, xo_hbm, ho_hbm, acc_ref, xbuf, obuf, hbuf, xsem, osem, hsem = refs
    else:
        x_hbm, xo_hbm, acc_ref, xbuf, obuf, xsem, osem = refs
        ho_hbm = hbuf = hsem = None
    i, k = pl.program_id(0), pl.program_id(1)
    tm, d = acc_ref.shape
    x_slots, rc = xbuf.shape[:2]
    o_slots = obuf.shape[0]
    n_chunks = tm // rc

    def hbm_rows(ref, c):
        return ref.at[pl.ds(pl.multiple_of(i * tm + c * rc, rc), rc), :]

    def x_copy(c, slot):
        return pltpu.make_async_copy(hbm_rows(x_hbm, c), xbuf.at[slot], xsem.at[slot])

    def o_copy(c, slot):
        return pltpu.make_async_copy(obuf.at[slot], hbm_rows(xo_hbm, c), osem.at[slot])

    def h_copy(c, slot):
        return pltpu.make_async_copy(hbuf.at[slot], hbm_rows(ho_hbm, c), hsem.at[slot])

    def wait_out(c, slot):
        o_copy(c, slot).wait()
        if has_next:
            h_copy(c, slot).wait()

    @pl.when(k == 0)
    def _():
        acc_ref[...] = jnp.zeros_like(acc_ref)

    @pl.when(k == nk - 1)
    def _():
        for c in range(min(x_slots, n_chunks)):
            x_copy(c, c).start()

    a = a_ref[...]
    cw = min(MM_LN_COL_CHUNK, d)
    for c in range(d // cw):
        cols = slice(c * cw, (c + 1) * cw)
        acc_ref[:, cols] += jnp.dot(a, b_ref[:, cols], preferred_element_type=F32)

    @pl.when(k == nk - 1)
    def _():
        def body(c, carry):
            xs, os_ = c % x_slots, c % o_slots
            x_copy(c, xs).wait()

            @pl.when(c >= o_slots)
            def _():
                wait_out(c - o_slots, os_)

            y = acc_ref[pl.ds(pl.multiple_of(c * rc, rc), rc), :]
            _deepnorm_epilogue(alpha, y, xbuf.at[xs], gate_ref, lng_ref, lnb_ref, next_refs, obuf.at[os_],
                               hbuf.at[os_] if has_next else None)
            o_copy(c, os_).start()
            if has_next:
                h_copy(c, os_).start()

            @pl.when(c + x_slots < n_chunks)
            def _():
                x_copy(c + x_slots, xs).start()

            return carry

        lax.fori_loop(0, n_chunks, body, 0)
        for c in range(max(n_chunks - o_slots, 0), n_chunks):
            wait_out(c, c % o_slots)


def _mm_ln(st, n_rows, alpha, a, b, x, mods, layer, gate_idx, lng, lnb, next_mod=None):
    kdim, d = b.shape
    tm = st.row_tile(1024)
    tk = _pick(kdim, (1024, 512, 256, 128))
    nk = kdim // tk
    rc = min(MM_LN_ROW_CHUNK, tm)
    has_next = next_mod is not None
    any_spec = pl.BlockSpec(memory_space=pl.ANY)
    in_specs = [pl.BlockSpec((tm, tk), lambda i, k: (i, k)), pl.BlockSpec((tk, d), lambda i, k: (k, 0)),
                st.mod_spec(layer, gate_idx, tm), _vec_spec(d), _vec_spec(d)]
    args = [a, b, mods, lng.reshape(1, d), lnb.reshape(1, d)]
    out_shape = [jax.ShapeDtypeStruct((n_rows, d), F32)]
    scratch = [pltpu.VMEM((tm, d), F32), pltpu.VMEM((MM_LN_X_SLOTS, rc, d), F32),
               pltpu.VMEM((MM_LN_OUT_SLOTS, rc, d), F32)]
    sems = [pltpu.SemaphoreType.DMA((MM_LN_X_SLOTS,)), pltpu.SemaphoreType.DMA((MM_LN_OUT_SLOTS,))]
    if has_next:
        nl, n_sh, n_sc = next_mod
        in_specs += [st.mod_spec(nl, n_sh, tm), st.mod_spec(nl, n_sc, tm)]
        args += [mods, mods]
        out_shape.append(jax.ShapeDtypeStruct((n_rows, d), BF16))
        scratch.append(pltpu.VMEM((MM_LN_OUT_SLOTS, rc, d), BF16))
        sems.append(pltpu.SemaphoreType.DMA((MM_LN_OUT_SLOTS,)))
    res = pl.pallas_call(
        functools.partial(_mm_ln_kernel, alpha, nk, has_next),
        out_shape=tuple(out_shape),
        grid=(n_rows // tm, nk),
        in_specs=in_specs + [any_spec],
        out_specs=tuple(any_spec for _ in out_shape),
        scratch_shapes=scratch + sems,
        compiler_params=_params("arbitrary", "arbitrary"),
        name="matmul_deepnorm",
    )(*args, x)
    return res if has_next else (res[0], None)


def _attn_kernel(n_lat_tiles, group, q_ref, kl_ref, vl_ref, kc_ref, vc_ref, o_ref, k_all, v_all):
    qi = pl.program_id(2)
    seq, ctx_len = kl_ref.shape[0], kc_ref.shape[0]

    @pl.when(qi == 0)
    def _():
        k_all[pl.ds(0, seq), :] = kl_ref[...]
        k_all[pl.ds(seq, ctx_len), :] = kc_ref[...]
        v_all[pl.ds(0, seq), pl.ds(0, HEAD_DIM)] = vl_ref[...]
        v_all[pl.ds(seq, ctx_len), pl.ds(0, HEAD_DIM)] = vc_ref[...]
        v_all[:, pl.ds(HEAD_DIM, HEAD_DIM)] = jnp.ones((seq + ctx_len, HEAD_DIM), v_all.dtype)

    def attend(k, v):
        for g in range(group):
            cols = slice(g * HEAD_DIM, (g + 1) * HEAD_DIM)
            s = lax.dot_general(q_ref[:, cols], k, (((1,), (1,)), ((), ())), preferred_element_type=F32)
            p = jnp.exp2(s - jnp.max(s, axis=-1, keepdims=True)).astype(v.dtype)
            o = jnp.dot(p, v, preferred_element_type=F32)
            o_ref[:, cols] = (o[:, :HEAD_DIM] / o[:, HEAD_DIM:]).astype(o_ref.dtype)

    @pl.when(qi < n_lat_tiles)
    def _():
        attend(k_all[...], v_all[...])

    @pl.when(qi >= n_lat_tiles)
    def _():
        attend(k_all[pl.ds(seq, ctx_len), :], v_all[pl.ds(seq, ctx_len), :])


def _attention(st, qkv, d):
    n_rows, n = qkv.shape
    kv = (n - d) // 2
    n_kv = kv // HEAD_DIM
    group = d // kv
    tq = min(256, st.ctx_len)
    ctx_tiles = st.ctx_len // tq
    lat_tiles = st.seq // tq
    qw = group * HEAD_DIM
    k_col0 = d // HEAD_DIM
    v_col0 = (d + kv) // HEAD_DIM

    def q_map(b, h, qi):
        row = jnp.where(qi < lat_tiles, b * lat_tiles + qi, st.n_lat // tq + b * ctx_tiles + (qi - lat_tiles))
        return (row, h)

    ctx_blk0 = st.n_lat // st.ctx_len
    return pl.pallas_call(
        functools.partial(_attn_kernel, lat_tiles, group),
        out_shape=jax.ShapeDtypeStruct((n_rows, d), BF16),
        grid=(st.batch, n_kv, lat_tiles + ctx_tiles),
        in_specs=[pl.BlockSpec((tq, qw), q_map),
                  pl.BlockSpec((st.seq, HEAD_DIM), lambda b, h, qi: (b, k_col0 + h)),
                  pl.BlockSpec((st.seq, HEAD_DIM), lambda b, h, qi: (b, v_col0 + h)),
                  pl.BlockSpec((st.ctx_len, HEAD_DIM), lambda b, h, qi: (ctx_blk0 + b, k_col0 + h)),
                  pl.BlockSpec((st.ctx_len, HEAD_DIM), lambda b, h, qi: (ctx_blk0 + b, v_col0 + h))],
        out_specs=pl.BlockSpec((tq, qw), q_map),
        scratch_shapes=[pltpu.VMEM((st.seq + st.ctx_len, HEAD_DIM), qkv.dtype),
                        pltpu.VMEM((st.seq + st.ctx_len, 2 * HEAD_DIM), qkv.dtype)],
        compiler_params=_params("arbitrary", "arbitrary", "arbitrary"),
        name="gqa_attention",
    )(qkv, qkv, qkv, qkv, qkv)


def _sigmoid(x):
    return 0.5 * jnp.tanh(0.5 * x) + 0.5


def _scan8(a, b, reverse):
    row = lax.broadcasted_iota(jnp.int32, a.shape, 0)
    for s in (1, 2, 4):
        if reverse:
            keep = row < SUBLANES - s
            shift = SUBLANES - s
        else:
            keep = row >= s
            shift = s
        a_sh = jnp.where(keep, pltpu.roll(a, shift, 0), 1.0)
        b_sh = jnp.where(keep, pltpu.roll(b, shift, 0), 0.0)
        b = a * b_sh + b
        a = a * a_sh
    return a, b


def _lru_kernel(tm, lat_tiles, reverse, *refs):
    if reverse:
        xr_ref, gw_ref, gb_ref, lam_ref, recf_ref, gelu_ref, o_ref, a_ref, b_ref, carry_ref = refs
    else:
        (xp_ref, x_ref, xn_ref, cw_ref, cb_ref, gw_ref, gb_ref, lam_ref,
         o_ref, xr_ref, ext_ref, a_ref, b_ref, carry_ref) = refs
    s = pl.program_id(1)
    is_ctx = s == 0

    @pl.when(is_ctx)
    def _():
        carry_ref[...] = jnp.zeros_like(carry_ref)

    if not reverse:
        tile = s - 1
        first = jnp.logical_or(is_ctx, tile == 0)
        last = jnp.logical_or(is_ctx, tile == lat_tiles - 1)
        ext_ref[pl.ds(HALO, tm), :] = x_ref[...]
        ext_ref[pl.ds(0, HALO), :] = jnp.where(first, 0.0, xp_ref[...])
        ext_ref[pl.ds(HALO + tm, HALO), :] = jnp.where(last, 0.0, xn_ref[...])

    heads, bw, _ = gw_ref.shape
    lam = lam_ref[...]
    softplus = jnp.maximum(-lam, 0.0) + jnp.log1p(jnp.exp(-jnp.abs(lam)))
    for h in range(heads):
        cols = pl.ds(h * bw, bw)
        if reverse:
            xr = xr_ref[:, cols]
        else:
            xr = cb_ref[:, cols] + ext_ref[pl.ds(HALO - CONV_LEFT, tm), cols] * cw_ref[0:1, cols]
            for k in range(1, CONV_W):
                xr = xr + ext_ref[pl.ds(HALO - CONV_LEFT + k, tm), cols] * cw_ref[k:k + 1, cols]
            xr_ref[:, cols] = xr
        g = jnp.dot(xr.astype(BF16), gw_ref[h], preferred_element_type=F32)
        r = _sigmoid(g[:, :bw] + gb_ref[0:1, cols])
        i = _sigmoid(g[:, bw:] + gb_ref[1:2, cols])
        log_a = -LRU_C * r * softplus[:, h * bw:(h + 1) * bw]
        a = jnp.exp(log_a)
        a_ref[:, cols] = a
        b_ref[:, cols] = jnp.sqrt(1.0 - a * a) * (i * xr)

    n_chunks = tm // SUBLANES

    def body(c, carry):
        c = n_chunks - 1 - c if reverse else c
        rows = pl.ds(pl.multiple_of(c * SUBLANES, SUBLANES), SUBLANES)
        a_cum, b_cum = _scan8(a_ref[rows, :], b_ref[rows, :], reverse)
        hs = a_cum * carry + b_cum
        if reverse:
            o_ref[rows, :] = (gelu_ref[rows, :] * (recf_ref[rows, :] + hs)).astype(o_ref.dtype)
            edge = hs[0:1, :]
        else:
            o_ref[rows, :] = hs
            edge = hs[SUBLANES - 1:SUBLANES, :]
        return jnp.broadcast_to(edge, carry.shape)

    carry_ref[...] = lax.fori_loop(0, n_chunks, body, carry_ref[...])


def _lru_scan(st, yx, conv_w, conv_b, gate_w, gate_b, lam, reverse, fwd=None):
    n_rows = yx.shape[0]
    d = st.d
    tm = min(256, st.ctx_len)
    assert st.ctx_len == tm
    lat_tiles = st.seq // tm
    ctx_blk0 = st.n_lat // tm
    per = tm // HALO
    last_halo = n_rows // HALO - 1

    def blk(b, s):
        lat = b * lat_tiles + (lat_tiles - s if reverse else s - 1)
        return jnp.where(s == 0, ctx_blk0 + b, lat)

    heads = gate_w.shape[1]
    bw = d // heads
    gw = jnp.concatenate([gate_w[0], gate_w[1]], axis=-1).astype(BF16)
    tile = pl.BlockSpec((tm, d), lambda b, s: (blk(b, s), 0))
    gate_specs = [pl.BlockSpec((heads, bw, 2 * bw), lambda b, s: (0, 0, 0)),
                  pl.BlockSpec((2, d), lambda b, s: (0, 0)), _vec_spec(d)]
    gate_args = [gw, gate_b.reshape(2, d), lam.reshape(1, d)]
    scratch = [pltpu.VMEM((tm, d), F32), pltpu.VMEM((tm, d), F32), pltpu.VMEM((SUBLANES, d), F32)]
    if reverse:
        rec_f, xr = fwd
        in_specs = [tile] + gate_specs + [tile, tile]
        args = [xr] + gate_args + [rec_f, yx]
        out_shape = jax.ShapeDtypeStruct((n_rows, d), BF16)
        out_specs = tile
    else:
        in_specs = [pl.BlockSpec((HALO, d), lambda b, s: (jnp.maximum(blk(b, s) * per - 1, 0), 1)),
                    pl.BlockSpec((tm, d), lambda b, s: (blk(b, s), 1)),
                    pl.BlockSpec((HALO, d), lambda b, s: (jnp.minimum((blk(b, s) + 1) * per, last_halo), 1)),
                    pl.BlockSpec((CONV_W, d), lambda b, s: (0, 0)), _vec_spec(d)] + gate_specs
        args = [yx, yx, yx, conv_w, conv_b.reshape(1, d)] + gate_args
        out_shape = (jax.ShapeDtypeStruct((n_rows, d), F32), jax.ShapeDtypeStruct((n_rows, d), F32))
        out_specs = (tile, tile)
        scratch = [pltpu.VMEM((tm + 2 * HALO, d), F32)] + scratch
    return pl.pallas_call(
        functools.partial(_lru_kernel, tm, lat_tiles, reverse),
        out_shape=out_shape,
        grid=(st.batch, lat_tiles + 1),
        in_specs=in_specs,
        out_specs=out_specs,
        scratch_shapes=scratch,
        compiler_params=_params("arbitrary", "arbitrary"),
        name="rglru_reverse" if reverse else "rglru_forward",
    )(*args)


ROUTE_LANES = 8
MOE_ROW_TILE = 512


def _lane_pick(rec, k):
    lane = lax.broadcasted_iota(jnp.int32, rec.shape, 1)
    return jnp.sum(jnp.where(lane == k, rec, 0.0), axis=1, keepdims=True)


def _router_kernel(x_ref, sh_ref, sc_ref, w_ref, route_ref, cnt_ref, run_ref):
    @pl.when(pl.program_id(0) == 0)
    def _():
        run_ref[...] = jnp.zeros_like(run_ref)

    h = x_ref[...] * (1.0 + sc_ref[0]) + sh_ref[0]
    n_exp = w_ref.shape[1] // 2
    h_hi = h.astype(jnp.bfloat16)
    h_lo = (h - h_hi.astype(F32)).astype(jnp.bfloat16)
    t = jnp.dot(h_hi, w_ref[...], preferred_element_type=F32)
    u = jnp.dot(h_lo, w_ref[:, :n_exp], preferred_element_type=F32)
    logits = (t[:, n_exp:] + u) + t[:, :n_exp]
    tm = logits.shape[0]
    lane = lax.broadcasted_iota(jnp.int32, logits.shape, 1).astype(F32)
    m1 = jnp.max(logits, axis=-1, keepdims=True)
    i1 = jnp.min(jnp.where(logits == m1, lane, float(n_exp)), axis=-1, keepdims=True)
    pick1 = lane == i1
    rest = jnp.where(pick1, -jnp.inf, logits)
    m2 = jnp.max(rest, axis=-1, keepdims=True)
    i2 = jnp.min(jnp.where(rest == m2, lane, float(n_exp)), axis=-1, keepdims=True)
    pick2 = lane == i2
    e2 = jnp.exp(m2 - m1)
    den = 1.0 + e2
    onehot = jnp.where(pick1, 1.0, jnp.where(pick2, 1.0, 0.0))
    earlier = (lax.broadcasted_iota(jnp.int32, (tm, tm), 0) > lax.broadcasted_iota(jnp.int32, (tm, tm), 1))
    before = jnp.dot(jnp.where(earlier, 1.0, 0.0).astype(BF16), onehot.astype(BF16),
                     preferred_element_type=F32) + run_ref[...]
    rank1 = jnp.sum(jnp.where(pick1, before, 0.0), axis=-1, keepdims=True)
    rank2 = jnp.sum(jnp.where(pick2, before, 0.0), axis=-1, keepdims=True)
    rec = jnp.zeros((tm, ROUTE_LANES), F32)
    for k, v in enumerate((i1, i2, 1.0 / den, e2 / den, rank1, rank2)):
        rec = jnp.where(lax.broadcasted_iota(jnp.int32, rec.shape, 1) == k, v, rec)
    route_ref[...] = rec
    run_ref[...] += jnp.sum(onehot, axis=0, keepdims=True)
    cnt_ref[...] = run_ref[...]


def _router(st, n_rows, x, mods, layer, w):
    d, n_exp = w.shape
    tm = min(256, st.ctx_len)
    w_hi = w.astype(jnp.bfloat16)
    w_split = jnp.concatenate([w_hi, (w - w_hi.astype(F32)).astype(jnp.bfloat16)], axis=1)
    return pl.pallas_call(
        _router_kernel,
        out_shape=(jax.ShapeDtypeStruct((n_rows, ROUTE_LANES), F32), jax.ShapeDtypeStruct((1, n_exp), F32)),
        grid=(n_rows // tm,),
        in_specs=[pl.BlockSpec((tm, d), lambda i: (i, 0)), st.mod_spec(layer, 3, tm), st.mod_spec(layer, 4, tm),
                  pl.BlockSpec((d, 2 * n_exp), lambda i: (0, 0))],
        out_specs=(pl.BlockSpec((tm, ROUTE_LANES), lambda i: (i, 0)), pl.BlockSpec((1, n_exp), lambda i: (0, 0))),
        scratch_shapes=[pltpu.VMEM((1, n_exp), F32)],
        compiler_params=_params("arbitrary"),
        name="moe_router",
    )(x, mods, mods, w_split)


def _dispatch_plan(route, counts, tm):
    n_tok = route.shape[0]
    n_exp = counts.shape[1]
    expert = route[:, 0:2].astype(jnp.int32)
    rank = route[:, 4:6].astype(jnp.int32)
    cnt = counts[0].astype(jnp.int32)
    padded = (cnt + tm - 1) // tm * tm
    ends = jnp.cumsum(padded)
    starts = ends - padded
    dest = (starts[expert] + rank).T.reshape(2 * n_tok)
    n_tiles = (2 * n_tok + n_exp * tm) // tm
    tile_row0 = jnp.arange(n_tiles, dtype=jnp.int32) * tm
    tile_expert = jnp.minimum(jnp.sum(tile_row0[:, None] >= ends[None, :], axis=1), n_exp - 1).astype(jnp.int32)
    n_used = (ends[-1:] // tm).astype(jnp.int32)
    return dest, tile_expert, n_used, n_tiles


HI16 = 0xFFFF0000
ROW_DMA_UNROLL = 8


def _pack_bf16_pairs(lo, hi):
    lo = lax.bitcast_convert_type(lo.astype(jnp.bfloat16).astype(F32), jnp.uint32)
    hi = lax.bitcast_convert_type(hi.astype(jnp.bfloat16).astype(F32), jnp.uint32)
    return (lo >> 16) | (hi & jnp.uint32(HI16))


def _unpack_bf16_pairs(p):
    return lax.bitcast_convert_type(p << 16, F32), lax.bitcast_convert_type(p & jnp.uint32(HI16), F32)


def _dispatch_kernel(tm, n_tok, dest_ref, x_ref, sh_ref, sc_ref, init_ref, o_ref, h_ref, sem):
    del init_ref
    base = pl.program_id(0) * tm
    h = x_ref[...] * (1.0 + sc_ref[0]) + sh_ref[0]
    half = h.shape[1] // 2
    h_ref[...] = _pack_bf16_pairs(h[:, :half], h[:, half:])

    def row_copy(r, d):
        return pltpu.make_async_copy(h_ref.at[pl.ds(r, 1), :], o_ref.at[pl.ds(d, 1), :], sem)

    def issue(r, carry):
        row_copy(r, dest_ref[base + r]).start()
        row_copy(r, dest_ref[n_tok + base + r]).start()
        return carry

    def drain(r, carry):
        row_copy(r, 0).wait()
        row_copy(r, 0).wait()
        return carry

    lax.fori_loop(0, tm, issue, 0, unroll=ROW_DMA_UNROLL)
    lax.fori_loop(0, tm, drain, 0, unroll=ROW_DMA_UNROLL)


def _dispatch(st, n_rows, x, mods, layer, dest, n_out):
    d = st.d
    tm = min(256, st.ctx_len)
    grid_spec = pltpu.PrefetchScalarGridSpec(
        num_scalar_prefetch=1,
        grid=(n_rows // tm,),
        in_specs=[pl.BlockSpec((tm, d), lambda i, dest: (i, 0)), st.mod_spec(layer, 3, tm), st.mod_spec(layer, 4, tm),
                  pl.BlockSpec(memory_space=pl.ANY)],
        out_specs=pl.BlockSpec(memory_space=pl.ANY),
        scratch_shapes=[pltpu.VMEM((tm, d // 2), jnp.uint32), pltpu.SemaphoreType.DMA],
    )
    return pl.pallas_call(
        functools.partial(_dispatch_kernel, tm, n_rows),
        out_shape=jax.ShapeDtypeStruct((n_out, d // 2), jnp.uint32),
        grid_spec=grid_spec,
        input_output_aliases={4: 0},
        compiler_params=_params("arbitrary"),
        name="moe_dispatch",
    )(dest, x, mods, mods, jnp.zeros((n_out, d // 2), jnp.uint32))


def _moe_up_kernel(te_ref, nu_ref, a_ref, wg_ref, wu_ref, o_ref):
    i = pl.program_id(1)

    @pl.when(i < nu_ref[0])
    def _():
        lo, hi = (t.astype(jnp.bfloat16) for t in _unpack_bf16_pairs(a_ref[...]))
        half = lo.shape[1]

        def proj(w_ref):
            return (jnp.dot(lo, w_ref[0, :half, :], preferred_element_type=F32)
                    + jnp.dot(hi, w_ref[0, half:, :], preferred_element_type=F32))

        gate, up = proj(wg_ref), proj(wu_ref)
        o_ref[...] = (gate * jax.nn.sigmoid(gate) * up).astype(o_ref.dtype)

    @pl.when(i >= nu_ref[0])
    def _():
        o_ref[...] = jnp.zeros_like(o_ref)


def _moe_up(xg, w, tile_expert, n_used, tm):
    n_out = xg.shape[0]
    d, f2 = w.shape[1:]
    f = f2 // 2
    tn = _pick(f, (512, 256, 128))
    nj = f // tn
    grid_spec = pltpu.PrefetchScalarGridSpec(
        num_scalar_prefetch=2,
        grid=(nj, n_out // tm),
        in_specs=[pl.BlockSpec((tm, d // 2), lambda j, i, te, nu: (i, 0)),
                  pl.BlockSpec((1, d, tn), lambda j, i, te, nu: (te[i], 0, j)),
                  pl.BlockSpec((1, d, tn), lambda j, i, te, nu: (te[i], 0, nj + j))],
        out_specs=pl.BlockSpec((tm, tn), lambda j, i, te, nu: (i, j)),
    )
    return pl.pallas_call(
        _moe_up_kernel,
        out_shape=jax.ShapeDtypeStruct((n_out, f), BF16),
        grid_spec=grid_spec,
        compiler_params=_params("arbitrary", "arbitrary"),
        name="moe_up",
    )(tile_expert, n_used, xg, w, w)


def _moe_down_kernel(te_ref, nu_ref, a_ref, w_ref, o_ref):
    i = pl.program_id(1)

    @pl.when(i < nu_ref[0])
    def _():
        a = a_ref[...]
        half = w_ref.shape[2] // 2
        o_ref[...] = _pack_bf16_pairs(jnp.dot(a, w_ref[0, :, :half], preferred_element_type=F32),
                                      jnp.dot(a, w_ref[0, :, half:], preferred_element_type=F32))

    @pl.when(i >= nu_ref[0])
    def _():
        o_ref[...] = jnp.zeros_like(o_ref)


def _moe_down(act, w, tile_expert, n_used, tm):
    n_out, f = act.shape
    d = w.shape[2]
    grid_spec = pltpu.PrefetchScalarGridSpec(
        num_scalar_prefetch=2,
        grid=(1, n_out // tm),
        in_specs=[pl.BlockSpec((tm, f), lambda j, i, te, nu: (i, 0)),
                  pl.BlockSpec((1, f, d), lambda j, i, te, nu: (te[i], 0, 0))],
        out_specs=pl.BlockSpec((tm, d // 2), lambda j, i, te, nu: (i, 0)),
    )
    return pl.pallas_call(
        _moe_down_kernel,
        out_shape=jax.ShapeDtypeStruct((n_out, d // 2), jnp.uint32),
        grid_spec=grid_spec,
        compiler_params=_params("arbitrary", "arbitrary"),
        name="moe_down",
    )(tile_expert, n_used, act, w)


COMBINE_ROW_CHUNK = 128


def _combine_kernel(tm, n_tok, alpha, has_next, dest_ref, *refs):
    route_ref, x_ref, gate_ref, lng_ref, lnb_ref = refs[:5]
    next_refs = refs[5:7] if has_next else None
    refs = refs[7:] if has_next else refs[5:]
    y_ref, xo_ref = refs[:2]
    ho_ref = refs[2] if has_next else None
    buf_ref, sem = refs[-2:]
    i = pl.program_id(0)
    n_steps = pl.num_programs(0)
    slot = i % 2

    def row_copy(slot, which, r, src):
        return pltpu.make_async_copy(y_ref.at[pl.ds(src, 1), :], buf_ref.at[slot, which, pl.ds(r, 1), :],
                                     sem.at[slot])

    def gather(tile, slot):
        def issue(r, carry):
            row_copy(slot, 0, r, dest_ref[tile * tm + r]).start()
            row_copy(slot, 1, r, dest_ref[n_tok + tile * tm + r]).start()
            return carry
        lax.fori_loop(0, tm, issue, 0, unroll=ROW_DMA_UNROLL)

    @pl.when(i == 0)
    def _():
        gather(0, 0)

    @pl.when(i + 1 < n_steps)
    def _():
        gather(i + 1, 1 - slot)

    def drain(r, carry):
        row_copy(slot, 0, r, 0).wait()
        row_copy(slot, 1, r, 0).wait()
        return carry

    lax.fori_loop(0, tm, drain, 0, unroll=ROW_DMA_UNROLL)
    rc = min(COMBINE_ROW_CHUNK, tm)

    def body(c, carry):
        rows = pl.ds(pl.multiple_of(c * rc, rc), rc)
        rec = route_ref[rows, :]
        g1, g2 = _lane_pick(rec, 2), _lane_pick(rec, 3)
        lo1, hi1 = _unpack_bf16_pairs(buf_ref[slot, 0, rows, :])
        lo2, hi2 = _unpack_bf16_pairs(buf_ref[slot, 1, rows, :])
        y = jnp.concatenate([g1 * lo1 + g2 * lo2, g1 * hi1 + g2 * hi2], axis=1)
        _deepnorm_epilogue(alpha, y, x_ref, gate_ref, lng_ref, lnb_ref, next_refs, xo_ref, ho_ref, rows)
        return carry

    lax.fori_loop(0, tm // rc, body, 0)


def _combine(st, n_rows, alpha, yg, route, dest, x, mods, layer, lng, lnb, next_mod=None):
    d = st.d
    tm = min(256, st.ctx_len)
    has_next = next_mod is not None
    row = lambda i, dest: (i, 0)
    in_specs = [pl.BlockSpec((tm, ROUTE_LANES), row), pl.BlockSpec((tm, d), row), st.mod_spec(layer, 5, tm),
                _vec_spec(d), _vec_spec(d)]
    args = [route, x, mods, lng.reshape(1, d), lnb.reshape(1, d)]
    out_shape = [jax.ShapeDtypeStruct((n_rows, d), F32)]
    out_specs = [pl.BlockSpec((tm, d), row)]
    if has_next:
        nl, n_sh, n_sc = next_mod
        in_specs += [st.mod_spec(nl, n_sh, tm), st.mod_spec(nl, n_sc, tm)]
        args += [mods, mods]
        out_shape.append(jax.ShapeDtypeStruct((n_rows, d), BF16))
        out_specs.append(pl.BlockSpec((tm, d), row))
    in_specs.append(pl.BlockSpec(memory_space=pl.ANY))
    args.append(yg)
    grid_spec = pltpu.PrefetchScalarGridSpec(
        num_scalar_prefetch=1,
        grid=(n_rows // tm,),
        in_specs=in_specs,
        out_specs=tuple(out_specs),
        scratch_shapes=[pltpu.VMEM((2, 2, tm, d // 2), jnp.uint32), pltpu.SemaphoreType.DMA((2,))],
    )
    res = pl.pallas_call(
        functools.partial(_combine_kernel, tm, n_rows, alpha, has_next),
        out_shape=tuple(out_shape),
        grid_spec=grid_spec,
        compiler_params=_params("arbitrary"),
        name="moe_combine",
    )(dest, *args)
    return res if has_next else (res[0], None)


def _moe_layer(st, n_rows, alpha, x, mods, layer, router_w, w_in, w_out, lng, lnb, next_mod):
    route, counts = _router(st, n_rows, x, mods, layer, router_w)
    dest, tile_expert, n_used, n_tiles = _dispatch_plan(route, counts, MOE_ROW_TILE)
    xg = _dispatch(st, n_rows, x, mods, layer, dest, n_tiles * MOE_ROW_TILE)
    act = _moe_up(xg, w_in, tile_expert, n_used, MOE_ROW_TILE)
    yg = _moe_down(act, w_out, tile_expert, n_used, MOE_ROW_TILE)
    return _combine(st, n_rows, alpha, yg, route, dest, x, mods, layer, lng, lnb, next_mod)


def _cast_kernel(is_pad, x_ref, o_ref):
    pad = is_pad(pl.program_id(0), pl.program_id(1))

    @pl.when(jnp.logical_not(pad))
    def _():
        o_ref[...] = x_ref[...].astype(o_ref.dtype)

    @pl.when(pad)
    def _():
        o_ref[...] = jnp.zeros_like(o_ref)


def _cast_blocks(w, layer, block, out_blocks, src_of, n_src):
    def in_map(i, j):
        bi, bj = src_of(i, j)
        return (layer, jnp.minimum(bi, n_src[0] - 1), jnp.minimum(bj, n_src[1] - 1))

    def is_pad(i, j):
        bi, bj = src_of(i, j)
        return jnp.logical_or(bi >= n_src[0], bj >= n_src[1])

    return pl.pallas_call(
        functools.partial(_cast_kernel, is_pad),
        out_shape=jax.ShapeDtypeStruct((out_blocks[0] * block[0], out_blocks[1] * block[1]), BF16),
        grid=out_blocks,
        in_specs=[pl.BlockSpec((None,) + block, in_map)],
        out_specs=pl.BlockSpec(block, lambda i, j: (i, j)),
        compiler_params=_params("arbitrary", "arbitrary"),
        name="cast_weights",
    )(w)


def _cast(w, layer):
    w3 = w.reshape(w.shape[0], -1, w.shape[-1])
    rows, cols = w3.shape[1:]
    block = (_pick(rows, (512, 256, 128, 64, 32, 16)), _pick(cols, (4096, 2048, 1024, 512, 256, 128)))
    n = (rows // block[0], cols // block[1])
    return _cast_blocks(w3, layer, block, n, lambda i, j: (i, j), n).reshape(w.shape[1:])


FFN_PAD_BLOCK = 256


def _cast_pad_ffn(w_in, w_out, layer, mult):
    d, f2 = w_in.shape[1:]
    f = f2 // 2
    blk = _pick(f, (FFN_PAD_BLOCK, 128))
    fp = _round_up(f, mult)
    nb, nbp = f // blk, fp // blk
    tr = _pick(d, (2048, 1024, 512, 256, 128))

    def in_src(i, j):
        up = j - nbp
        return (i, jnp.where(j < nbp, jnp.where(j < nb, j, 2 * nb), jnp.where(up < nb, up + nb, 2 * nb)))

    w_in_p = _cast_blocks(w_in, layer, (tr, blk), (d // tr, 2 * nbp), in_src, (d // tr, 2 * nb))
    dc = _pick(d, (4096, 2048, 1024, 512, 256, 128))
    w_out_p = _cast_blocks(w_out, layer, (blk, dc), (nbp, d // dc), lambda i, j: (i, j), (nb, d // dc))
    return w_in_p.reshape(1, d, 2 * fp), w_out_p


def kernel(x, c, ctx, c_ctx, ada_down, ada_up, ada_b, ln_g, ln_b, pool_w, pool_scale, attn_wqkv, attn_q_gain,
           attn_k_gain, attn_wo, lru_w_in, lru_conv_w, lru_conv_b, lru_gate_w, lru_gate_b, lru_lambda, lru_w_out,
           ffn_w_in, ffn_w_out, moe_router, moe_w_in, moe_w_out):
    batch, seq, d = x.shape
    ctx_len = ctx.shape[1]
    depth = ada_down.shape[0]
    assert batch < MOD_ROWS and seq % ctx_len == 0 and ctx_len % SUBLANES == 0
    st = _Stream(batch, seq, ctx_len, d)
    alpha = (2.0 * depth) ** 0.25
    mixers = [i % N_MIXERS for i in range(depth)]
    is_moe = [i % 2 == 1 for i in range(depth)]
    ctx_needed_after = [any(mixers[j] != 0 for j in range(i + 1, depth)) for i in range(depth)]

    cvec = jnp.concatenate([c, c_ctx[None], jnp.zeros((MOD_ROWS - batch - 1, d), F32)], axis=0)
    mods = _adaln(cvec, ada_down, ada_up, ada_b)

    xs = jnp.concatenate([x.reshape(batch * seq, d), ctx.reshape(batch * ctx_len, d)], axis=0)
    n_rows = st.n_all
    h = None
    for i in range(depth):
        mixer = mixers[i]
        mi = mixers[:i].count(mixer)
        fi = is_moe[:i].count(is_moe[i])
        lng, lnb = ln_g[i], ln_b[i]
        ffn_mod = None if is_moe[i] else (i, 3, 4)
        if not ctx_needed_after[i] and mixer == 0 and n_rows != st.n_lat:
            n_rows = st.n_lat
            xs = xs[:n_rows]

        if mixer == 0:
            xs, hf = _pool_layer(st, n_rows, i, alpha, xs, mods, _cast(pool_w, mi), pool_scale[mi],
                                 lng[0], lnb[0])
        elif mixer == 1:
            qkv = _qkv_proj(st, h, _cast(attn_wqkv, mi), attn_q_gain[mi], attn_k_gain[mi])
            o = _attention(st, qkv, d)
            rows_out = n_rows if ctx_needed_after[i] else st.n_lat
            xs, hf = _mm_ln(st, rows_out, alpha, o, _cast(attn_wo, mi), xs, mods, i, 2, lng[0], lnb[0],
                            next_mod=ffn_mod)
            n_rows = rows_out
        else:
            yx = _lru_in_proj(h, _cast(lru_w_in, mi))
            fwd = _lru_scan(st, yx, lru_conv_w[mi], lru_conv_b[mi], lru_gate_w[mi, 0], lru_gate_b[mi, 0],
                            lru_lambda[mi, 0], False)
            m = _lru_scan(st, yx, lru_conv_w[mi], lru_conv_b[mi], lru_gate_w[mi, 1], lru_gate_b[mi, 1],
                          lru_lambda[mi, 1], True, fwd)
            rows_out = n_rows if ctx_needed_after[i] else st.n_lat
            xs, hf = _mm_ln(st, rows_out, alpha, m, _cast(lru_w_out, mi), xs, mods, i, 2, lng[0], lnb[0],
                            next_mod=ffn_mod)
            n_rows = rows_out

        next_mod = (i + 1, 0, 1) if i + 1 < depth and mixers[i + 1] != 0 else None
        if is_moe[i]:
            xs, h = _moe_layer(st, n_rows, alpha, xs, mods, i, moe_router[fi], _cast(moe_w_in, fi),
                               _cast(moe_w_out, fi), lng[1], lnb[1], next_mod)
        else:
            w_in_p, w_out_p = _cast_pad_ffn(ffn_w_in, ffn_w_out, fi, 1024)
            act = _swiglu_up(hf, w_in_p, n_rows)
            xs, h = _mm_ln(st, n_rows, alpha, act, w_out_p, xs, mods, i, 5, lng[1], lnb[1], next_mod=next_mod)
    return xs[:st.n_lat].reshape(batch, seq, d)
```

```python
import functools

import jax
import jax.numpy as jnp
from jax import lax
from jax.experimental import pallas as pl
from jax.experimental.pallas import tpu as pltpu

F32 = jnp.float32
BF16 = jnp.bfloat16
HIGHEST = lax.Precision.HIGHEST

HEAD_DIM = 128
AXIS_DIM = HEAD_DIM // 2
GRID_W = 64
ROPE_THETA = 10000.0
LN_EPS = 1e-6
RMS_EPS = 1e-6
LRU_C = 8.0
LOG2E = 1.4426950408889634
CONV_W = 4
CONV_LEFT = CONV_W // 2
POOL_WINDOWS = (2, 4, 8, 16)
N_MOD = 6
N_MIXERS = 3
MOD_ROWS = 8

SUBLANES = 8
LANES = 128
VMEM_LIMIT_BYTES = 56 * 1024 * 1024
HALO = SUBLANES


def _params(*sem):
    return pltpu.CompilerParams(dimension_semantics=sem, vmem_limit_bytes=VMEM_LIMIT_BYTES)


def _pick(n, prefs):
    for p in prefs:
        if n % p == 0:
            return p
    return n


def _round_up(n, m):
    return (n + m - 1) // m * m


def _ln_rows(z, g, b):
    mu = jnp.mean(z, axis=-1, keepdims=True)
    zc = z - mu
    var = jnp.mean(zc * zc, axis=-1, keepdims=True)
    return zc * lax.rsqrt(var + LN_EPS) * g + b


class _Stream:
    def __init__(self, batch, seq, ctx_len, d):
        self.batch, self.seq, self.ctx_len, self.d = batch, seq, ctx_len, d
        self.n_lat = batch * seq
        self.n_all = self.n_lat + batch * ctx_len

    def row_tile(self, largest):
        t = largest
        while self.seq % t or (self.batch * self.ctx_len) % t:
            t //= 2
        return t

    def group(self, row0):
        return jnp.minimum(row0 // self.seq, self.batch)

    def mod_spec(self, layer, which, tm):
        def imap(i, *_):
            return ((layer * MOD_ROWS + self.group(i * tm)) * N_MOD + which, 0, 0)
        return pl.BlockSpec((1, 1, self.d), imap)

    def seq_pos(self, row0):
        is_ctx = row0 >= self.n_lat
        pos = jnp.where(is_ctx, (row0 - self.n_lat) % self.ctx_len, row0 % self.seq)
        return pos, jnp.where(is_ctx, self.ctx_len, self.seq)


def _vec_spec(d):
    return pl.BlockSpec((1, d), lambda *_: (0, 0))


def _halo_specs(tm, width, n_rows, col_block=0):
    per = tm // HALO
    last = n_rows // HALO - 1
    prev = pl.BlockSpec((HALO, width), lambda i, *_: (jnp.maximum(i * per - 1, 0), col_block))
    cur = pl.BlockSpec((tm, width), lambda i, *_: (i, col_block))
    nxt = pl.BlockSpec((HALO, width), lambda i, *_: (jnp.minimum((i + 1) * per, last), col_block))
    return prev, cur, nxt


def _adaln_kernel(c_ref, down_ref, up_ref, b_ref, o_ref):
    c = c_ref[...]
    s = c * jax.nn.sigmoid(c)
    t = jnp.dot(s, down_ref[0], preferred_element_type=F32, precision=HIGHEST)
    o_ref[0] = jnp.dot(t, up_ref[0], preferred_element_type=F32, precision=HIGHEST) + b_ref[0]


def _adaln(cvec, ada_down, ada_up, ada_b):
    depth, d, rank = ada_down.shape
    n = ada_up.shape[-1]
    tn = _pick(n, (4096, 2048, 1024, 512, 256, 128))
    out = pl.pallas_call(
        _adaln_kernel,
        out_shape=jax.ShapeDtypeStruct((depth, MOD_ROWS, n), F32),
        grid=(depth, n // tn),
        in_specs=[
            pl.BlockSpec((MOD_ROWS, d), lambda l, j: (0, 0)),
            pl.BlockSpec((1, d, rank), lambda l, j: (l, 0, 0)),
            pl.BlockSpec((1, rank, tn), lambda l, j: (l, 0, j)),
            pl.BlockSpec((1, 1, tn), lambda l, j: (l, 0, j)),
        ],
        out_specs=pl.BlockSpec((1, MOD_ROWS, tn), lambda l, j: (l, 0, j)),
        compiler_params=_params("arbitrary", "arbitrary"),
        name="adaln",
    )(cvec, ada_down, ada_up, ada_b.reshape(depth, 1, n))
    return out.reshape(depth * MOD_ROWS * N_MOD, 1, d)


POOL_ROW_CHUNK = 128


def _pool_kernel(st, tm, alpha, xp_ref, x_ref, xn_ref, sh_ref, sc_ref, gm_ref, shf_ref, scf_ref,
                 w_ref, ps_ref, lng_ref, lnb_ref, xo_ref, ho_ref, ext_ref, z_ref):
    row0 = pl.program_id(0) * tm
    pos0, seqlen = st.seq_pos(row0)
    first = pos0 == 0
    last = pos0 + tm == seqlen
    one_sc = 1.0 + sc_ref[0]
    sh = sh_ref[0]
    x = x_ref[...]
    ext_ref[pl.ds(HALO, tm), :] = x * one_sc + sh
    ext_ref[pl.ds(0, HALO), :] = jnp.where(first, 0.0, xp_ref[...] * one_sc + sh)
    ext_ref[pl.ds(HALO + tm, HALO), :] = jnp.where(last, 0.0, xn_ref[...] * one_sc + sh)

    pos = pos0 + lax.broadcasted_iota(jnp.int32, (tm, 1), 0)
    groups = len(POOL_WINDOWS)
    gw = st.d // groups
    for g, k in enumerate(POOL_WINDOWS):
        cols = pl.ds(g * gw, gw)
        lo = jnp.maximum(pos - k // 2, 0)
        hi = jnp.minimum(pos + (k - k // 2), seqlen)
        cnt = (hi - lo).astype(F32)
        win = ext_ref[pl.ds(HALO - k // 2, tm), cols]
        for j in range(1, k):
            win = win + ext_ref[pl.ds(HALO - k // 2 + j, tm), cols]
        diff = (win / cnt - ext_ref[pl.ds(HALO, tm), cols]).astype(BF16)
        z_ref[:, cols] = jnp.dot(diff, w_ref[g], preferred_element_type=F32) * ps_ref[:, cols]

    rc = min(POOL_ROW_CHUNK, tm)

    def body(c, carry):
        rows = pl.ds(pl.multiple_of(c * rc, rc), rc)
        _deepnorm_epilogue(alpha, z_ref[rows, :], x_ref, gm_ref, lng_ref, lnb_ref, (shf_ref, scf_ref), xo_ref, ho_ref,
                           rows)
        return carry

    lax.fori_loop(0, tm // rc, body, 0)


def _pool_layer(st, n_rows, layer, alpha, x, mods, w, pscale, lng, lnb):
    d = st.d
    tm = min(256, st.ctx_len)
    prev, cur, nxt = _halo_specs(tm, d, n_rows)
    groups, gw, _ = w.shape
    return pl.pallas_call(
        functools.partial(_pool_kernel, st, tm, alpha),
        out_shape=(jax.ShapeDtypeStruct((n_rows, d), F32), jax.ShapeDtypeStruct((n_rows, d), BF16)),
        grid=(n_rows // tm,),
        in_specs=[prev, cur, nxt] + [st.mod_spec(layer, q, tm) for q in (0, 1, 2, 3, 4)] + [
            pl.BlockSpec((groups, gw, gw), lambda i: (0, 0, 0)),
            _vec_spec(d), _vec_spec(d), _vec_spec(d)],
        out_specs=(pl.BlockSpec((tm, d), lambda i: (i, 0)), pl.BlockSpec((tm, d), lambda i: (i, 0))),
        scratch_shapes=[pltpu.VMEM((tm + 2 * HALO, d), F32), pltpu.VMEM((tm, d), F32)],
        compiler_params=_params("arbitrary"),
        name="pool_mixer",
    )(x, x, x, mods, mods, mods, mods, mods, w, pscale.reshape(1, d), lng.reshape(1, d), lnb.reshape(1, d))


def _qkv_kernel(n_qk_tiles, a_ref, b_ref, gain_ref, c_ref, s1_ref, s2_ref, o_ref):
    j = pl.program_id(1)
    acc = jnp.dot(a_ref[...], b_ref[...], preferred_element_type=F32)
    tn = acc.shape[1]

    @pl.when(j < n_qk_tiles)
    def _():
        cos, s1, s2 = c_ref[...], s1_ref[...], s2_ref[...]
        for hh in range(tn // HEAD_DIM):
            cols = slice(hh * HEAD_DIM, (hh + 1) * HEAD_DIM)
            xh = acc[:, cols]
            n = xh * lax.rsqrt(jnp.mean(xh * xh, axis=-1, keepdims=True) + RMS_EPS) * gain_ref[:, cols]
            half = AXIS_DIM // 2
            rot = n * cos + pltpu.roll(n, HEAD_DIM - half, 1) * s1 + pltpu.roll(n, half, 1) * s2
            o_ref[:, cols] = rot.astype(o_ref.dtype)

    @pl.when(j >= n_qk_tiles)
    def _():
        o_ref[...] = acc.astype(o_ref.dtype)


def _rope_tables(seq, tm):
    rows = seq // GRID_W
    r = jnp.repeat(jnp.arange(rows, dtype=F32), GRID_W)
    col = jnp.tile(jnp.arange(GRID_W, dtype=F32), rows)
    inv = ROPE_THETA ** (-jnp.arange(0, AXIS_DIM, 2, dtype=F32) / AXIS_DIM)
    ang_r = r[:, None] * inv
    ang_c = col[:, None] * inv
    zero = jnp.zeros_like(ang_r)
    cos = jnp.concatenate([jnp.cos(ang_r), jnp.cos(ang_r), jnp.cos(ang_c), jnp.cos(ang_c)], axis=-1)
    s1 = jnp.concatenate([-jnp.sin(ang_r), zero, -jnp.sin(ang_c), zero], axis=-1)
    s2 = jnp.concatenate([zero, jnp.sin(ang_r), zero, jnp.sin(ang_c)], axis=-1)
    ident = jnp.ones((tm, HEAD_DIM), F32)
    nul = jnp.zeros((tm, HEAD_DIM), F32)
    return (jnp.concatenate([cos, ident]), jnp.concatenate([s1, nul]), jnp.concatenate([s2, nul]))


def _qkv_proj(st, h, w, q_gain, k_gain):
    n_rows, d = h.shape
    n = w.shape[1]
    kv = (n - d) // 2
    tm = st.row_tile(1024)
    tn = _pick(kv, (1024, 512, 256, 128))
    scale = HEAD_DIM ** -0.5 * LOG2E
    gain = jnp.concatenate([jnp.tile(q_gain * scale, d // HEAD_DIM), jnp.tile(k_gain, kv // HEAD_DIM),
                            jnp.ones((kv,), F32)]).reshape(1, n)
    cos, s1, s2 = _rope_tables(st.seq, tm)
    per_seq = st.seq // tm

    def tab_map(i, j):
        return (jnp.where(i * tm < st.n_lat, i % per_seq, per_seq), 0)

    tab = pl.BlockSpec((tm, HEAD_DIM), tab_map)
    return pl.pallas_call(
        functools.partial(_qkv_kernel, (d + kv) // tn),
        out_shape=jax.ShapeDtypeStruct((n_rows, n), BF16),
        grid=(n_rows // tm, n // tn),
        in_specs=[pl.BlockSpec((tm, d), lambda i, j: (i, 0)), pl.BlockSpec((d, tn), lambda i, j: (0, j)),
                  pl.BlockSpec((1, tn), lambda i, j: (0, j)), tab, tab, tab],
        out_specs=pl.BlockSpec((tm, tn), lambda i, j: (i, j)),
        compiler_params=_params("arbitrary", "arbitrary"),
        name="qkv_proj",
    )(h, w, gain, cos, s1, s2)


def _gelu_tanh(x):
    return 0.5 * x * (1.0 + jnp.tanh(0.7978845608028654 * (x + 0.044715 * (x * x * x))))


def _lru_in_kernel(n_gelu_tiles, a_ref, b_ref, o_ref):
    j = pl.program_id(1)
    acc = jnp.dot(a_ref[...], b_ref[...], preferred_element_type=F32)

    @pl.when(j < n_gelu_tiles)
    def _():
        o_ref[...] = _gelu_tanh(acc)

    @pl.when(j >= n_gelu_tiles)
    def _():
        o_ref[...] = acc


def _lru_in_proj(h, w):
    n_rows, d = h.shape
    n = w.shape[1]
    tm = _pick(n_rows, (1024, 512, 256, 128, 64))
    tn = _pick(n // 2, (1024, 512, 256, 128))
    return pl.pallas_call(
        functools.partial(_lru_in_kernel, (n // 2) // tn),
        out_shape=jax.ShapeDtypeStruct((n_rows, n), F32),
        grid=(n_rows // tm, n // tn),
        in_specs=[pl.BlockSpec((tm, d), lambda i, j: (i, 0)), pl.BlockSpec((d, tn), lambda i, j: (0, j))],
        out_specs=pl.BlockSpec((tm, tn), lambda i, j: (i, j)),
        compiler_params=_params("arbitrary", "arbitrary"),
        name="lru_in_proj",
    )(h, w)


def _swiglu_kernel(a_ref, wg_ref, wu_ref, o_ref):
    a = a_ref[...]
    gate = jnp.dot(a, wg_ref[0], preferred_element_type=F32)
    up = jnp.dot(a, wu_ref[0], preferred_element_type=F32)
    o_ref[...] = (gate * jax.nn.sigmoid(gate) * up).astype(o_ref.dtype)


def _swiglu_up(h, w, n_rows):
    e, d, f2 = w.shape
    f = f2 // 2
    tm = _pick(n_rows, (1024, 512, 256, 128, 64))
    tn = _pick(f, (512, 256, 128))
    nj = f // tn
    return pl.pallas_call(
        _swiglu_kernel,
        out_shape=jax.ShapeDtypeStruct((n_rows, e * f), BF16),
        grid=(n_rows // tm, e, nj),
        in_specs=[pl.BlockSpec((tm, d), lambda i, x, j: (i, 0)),
                  pl.BlockSpec((1, d, tn), lambda i, x, j: (x, 0, j)),
                  pl.BlockSpec((1, d, tn), lambda i, x, j: (x, 0, nj + j))],
        out_specs=pl.BlockSpec((tm, tn), lambda i, x, j: (i, x * nj + j)),
        compiler_params=_params("arbitrary", "arbitrary", "arbitrary"),
        name="swiglu_up",
    )(h, w, w)


def _deepnorm_epilogue(alpha, y, x_ref, gate_ref, lng_ref, lnb_ref, next_refs, xo_ref, ho_ref, rows=slice(None)):
    xn = _ln_rows(alpha * x_ref[rows, :] + gate_ref[0] * y, lng_ref[...], lnb_ref[...])
    xo_ref[rows, :] = xn
    if ho_ref is not None:
        shn_ref, scn_ref = next_refs
        ho_ref[rows, :] = (xn * (1.0 + scn_ref[0]) + shn_ref[0]).astype(BF16)


MM_LN_COL_CHUNK = 1024
MM_LN_ROW_CHUNK = 128
MM_LN_X_SLOTS = 4
MM_LN_OUT_SLOTS = 2


def _mm_ln_kernel(alpha, nk, has_next, *refs):
    a_ref, b_ref, gate_ref, lng_ref, lnb_ref = refs[:5]
    next_refs = refs[5:7] if has_next else None
    refs = refs[7:] if has_next else refs[5:]
    if has_next:
        x_hbm, xo_hbm, ho_hbm, acc_ref, xbuf, obuf, hbuf, xsem, osem, hsem = refs
    else:
        x_hbm, xo_hbm, acc_ref, xbuf, obuf, xsem, osem = refs
        ho_hbm = hbuf = hsem = None
    i, k = pl.program_id(0), pl.program_id(1)
    tm, d = acc_ref.shape
    x_slots, rc = xbuf.shape[:2]
    o_slots = obuf.shape[0]
    n_chunks = tm // rc

    def hbm_rows(ref, c):
        return ref.at[pl.ds(pl.multiple_of(i * tm + c * rc, rc), rc), :]

    def x_copy(c, slot):
        return pltpu.make_async_copy(hbm_rows(x_hbm, c), xbuf.at[slot], xsem.at[slot])

    def o_copy(c, slot):
        return pltpu.make_async_copy(obuf.at[slot], hbm_rows(xo_hbm, c), osem.at[slot])

    def h_copy(c, slot):
        return pltpu.make_async_copy(hbuf.at[slot], hbm_rows(ho_hbm, c), hsem.at[slot])

    def wait_out(c, slot):
        o_copy(c, slot).wait()
        if has_next:
            h_copy(c, slot).wait()

    @pl.when(k == 0)
    def _():
        acc_ref[...] = jnp.zeros_like(acc_ref)

    @pl.when(k == nk - 1)
    def _():
        for c in range(min(x_slots, n_chunks)):
            x_copy(c, c).start()

    a = a_ref[...]
    cw = min(MM_LN_COL_CHUNK, d)
    for c in range(d // cw):
        cols = slice(c * cw, (c + 1) * cw)
        acc_ref[:, cols] += jnp.dot(a, b_ref[:, cols], preferred_element_type=F32)

    @pl.when(k == nk - 1)
    def _():
        def body(c, carry):
            xs, os_ = c % x_slots, c % o_slots
            x_copy(c, xs).wait()

            @pl.when(c >= o_slots)
            def _():
                wait_out(c - o_slots, os_)

            y = acc_ref[pl.ds(pl.multiple_of(c * rc, rc), rc), :]
            _deepnorm_epilogue(alpha, y, xbuf.at[xs], gate_ref, lng_ref, lnb_ref, next_refs, obuf.at[os_],
                               hbuf.at[os_] if has_next else None)
            o_copy(c, os_).start()
            if has_next:
                h_copy(c, os_).start()

            @pl.when(c + x_slots < n_chunks)
            def _():
                x_copy(c + x_slots, xs).start()

            return carry

        lax.fori_loop(0, n_chunks, body, 0)
        for c in range(max(n_chunks - o_slots, 0), n_chunks):
            wait_out(c, c % o_slots)


def _mm_ln(st, n_rows, alpha, a, b, x, mods, layer, gate_idx, lng, lnb, next_mod=None):
    kdim, d = b.shape
    tm = st.row_tile(1024)
    tk = _pick(kdim, (1024, 512, 256, 128))
    nk = kdim // tk
    rc = min(MM_LN_ROW_CHUNK, tm)
    has_next = next_mod is not None
    any_spec = pl.BlockSpec(memory_space=pl.ANY)
    in_specs = [pl.BlockSpec((tm, tk), lambda i, k: (i, k)), pl.BlockSpec((tk, d), lambda i, k: (k, 0)),
                st.mod_spec(layer, gate_idx, tm), _vec_spec(d), _vec_spec(d)]
    args = [a, b, mods, lng.reshape(1, d), lnb.reshape(1, d)]
    out_shape = [jax.ShapeDtypeStruct((n_rows, d), F32)]
    scratch = [pltpu.VMEM((tm, d), F32), pltpu.VMEM((MM_LN_X_SLOTS, rc, d), F32),
               pltpu.VMEM((MM_LN_OUT_SLOTS, rc, d), F32)]
    sems = [pltpu.SemaphoreType.DMA((MM_LN_X_SLOTS,)), pltpu.SemaphoreType.DMA((MM_LN_OUT_SLOTS,))]
    if has_next:
        nl, n_sh, n_sc = next_mod
        in_specs += [st.mod_spec(nl, n_sh, tm), st.mod_spec(nl, n_sc, tm)]
        args += [mods, mods]
        out_shape.append(jax.ShapeDtypeStruct((n_rows, d), BF16))
        scratch.append(pltpu.VMEM((MM_LN_OUT_SLOTS, rc, d), BF16))
        sems.append(pltpu.SemaphoreType.DMA((MM_LN_OUT_SLOTS,)))
    res = pl.pallas_call(
        functools.partial(_mm_ln_kernel, alpha, nk, has_next),
        out_shape=tuple(out_shape),
        grid=(n_rows // tm, nk),
        in_specs=in_specs + [any_spec],
        out_specs=tuple(any_spec for _ in out_shape),
        scratch_shapes=scratch + sems,
        compiler_params=_params("arbitrary", "arbitrary"),
        name="matmul_deepnorm",
    )(*args, x)
    return res if has_next else (res[0], None)


ATTN_KEY_ALIGN = 256
ATTN_KEY_RANGES = 2


def _attn_kernel(n_lat_tiles, group, q_ref, kl_ref, vl_ref, kc_ref, vc_ref, o_ref, k_all, v_all):
    qi = pl.program_id(2)
    seq, ctx_len = kl_ref.shape[0], kc_ref.shape[0]

    @pl.when(qi == 0)
    def _():
        k_all[pl.ds(0, seq), :] = kl_ref[...]
        k_all[pl.ds(seq, ctx_len), :] = kc_ref[...]
        v_all[pl.ds(0, seq), pl.ds(0, HEAD_DIM)] = vl_ref[...]
        v_all[pl.ds(seq, ctx_len), pl.ds(0, HEAD_DIM)] = vc_ref[...]
        v_all[:, pl.ds(HEAD_DIM, HEAD_DIM)] = jnp.ones((seq + ctx_len, HEAD_DIM), v_all.dtype)

    def partial_softmax(q, rows):
        s = lax.dot_general(q, k_all[rows, :], (((1,), (1,)), ((), ())), preferred_element_type=F32)
        m = jnp.max(s, axis=-1, keepdims=True)
        return jnp.dot(jnp.exp2(s - m).astype(v_all.dtype), v_all[rows, :], preferred_element_type=F32), m

    def attend(key_ranges):
        for g in range(group):
            cols = slice(g * HEAD_DIM, (g + 1) * HEAD_DIM)
            q = q_ref[:, cols]
            parts = [partial_softmax(q, rows) for rows in key_ranges]
            o, m = parts[0]
            for o2, m2 in parts[1:]:
                m_new = jnp.maximum(m, m2)
                o = o * jnp.exp2(m - m_new) + o2 * jnp.exp2(m2 - m_new)
                m = m_new
            o_ref[:, cols] = (o[:, :HEAD_DIM] / o[:, HEAD_DIM:]).astype(o_ref.dtype)

    @pl.when(qi < n_lat_tiles)
    def _():
        n_keys = seq + ctx_len
        step = _round_up(-(-n_keys // ATTN_KEY_RANGES), ATTN_KEY_ALIGN)
        attend([pl.ds(lo, min(step, n_keys - lo)) for lo in range(0, n_keys, step)])

    @pl.when(qi >= n_lat_tiles)
    def _():
        attend([pl.ds(seq, ctx_len)])


def _attention(st, qkv, d):
    n_rows, n = qkv.shape
    kv = (n - d) // 2
    n_kv = kv // HEAD_DIM
    group = d // kv
    tq = min(256, st.ctx_len)
    ctx_tiles = st.ctx_len // tq
    lat_tiles = st.seq // tq
    qw = group * HEAD_DIM
    k_col0 = d // HEAD_DIM
    v_col0 = (d + kv) // HEAD_DIM

    def q_map(b, h, qi):
        row = jnp.where(qi < lat_tiles, b * lat_tiles + qi, st.n_lat // tq + b * ctx_tiles + (qi - lat_tiles))
        return (row, h)

    ctx_blk0 = st.n_lat // st.ctx_len
    return pl.pallas_call(
        functools.partial(_attn_kernel, lat_tiles, group),
        out_shape=jax.ShapeDtypeStruct((n_rows, d), BF16),
        grid=(st.batch, n_kv, lat_tiles + ctx_tiles),
        in_specs=[pl.BlockSpec((tq, qw), q_map),
                  pl.BlockSpec((st.seq, HEAD_DIM), lambda b, h, qi: (b, k_col0 + h)),
                  pl.BlockSpec((st.seq, HEAD_DIM), lambda b, h, qi: (b, v_col0 + h)),
                  pl.BlockSpec((st.ctx_len, HEAD_DIM), lambda b, h, qi: (ctx_blk0 + b, k_col0 + h)),
                  pl.BlockSpec((st.ctx_len, HEAD_DIM), lambda b, h, qi: (ctx_blk0 + b, v_col0 + h))],
        out_specs=pl.BlockSpec((tq, qw), q_map),
        scratch_shapes=[pltpu.VMEM((st.seq + st.ctx_len, HEAD_DIM), qkv.dtype),
                        pltpu.VMEM((st.seq + st.ctx_len, 2 * HEAD_DIM), qkv.dtype)],
        compiler_params=_params("arbitrary", "arbitrary", "arbitrary"),
        name="gqa_attention",
    )(qkv, qkv, qkv, qkv, qkv)


def _sigmoid(x):
    return 0.5 * jnp.tanh(0.5 * x) + 0.5


def _scan8(a, b, reverse):
    row = lax.broadcasted_iota(jnp.int32, a.shape, 0)
    for s in (1, 2, 4):
        if reverse:
            keep = row < SUBLANES - s
            shift = SUBLANES - s
        else:
            keep = row >= s
            shift = s
        a_sh = jnp.where(keep, pltpu.roll(a, shift, 0), 1.0)
        b_sh = jnp.where(keep, pltpu.roll(b, shift, 0), 0.0)
        b = a * b_sh + b
        a = a * a_sh
    return a, b


def _lru_kernel(tm, lat_tiles, reverse, *refs):
    if reverse:
        xr_ref, gw_ref, gb_ref, lam_ref, recf_ref, gelu_ref, o_ref, a_ref, b_ref, carry_ref = refs
    else:
        (xp_ref, x_ref, xn_ref, cw_ref, cb_ref, gw_ref, gb_ref, lam_ref,
         o_ref, xr_ref, ext_ref, a_ref, b_ref, carry_ref) = refs
    s = pl.program_id(1)
    is_ctx = s == 0

    @pl.when(is_ctx)
    def _():
        carry_ref[...] = jnp.zeros_like(carry_ref)

    if not reverse:
        tile = s - 1
        first = jnp.logical_or(is_ctx, tile == 0)
        last = jnp.logical_or(is_ctx, tile == lat_tiles - 1)
        ext_ref[pl.ds(HALO, tm), :] = x_ref[...]
        ext_ref[pl.ds(0, HALO), :] = jnp.where(first, 0.0, xp_ref[...])
        ext_ref[pl.ds(HALO + tm, HALO), :] = jnp.where(last, 0.0, xn_ref[...])

    heads, bw, _ = gw_ref.shape
    lam = lam_ref[...]
    softplus = jnp.maximum(-lam, 0.0) + jnp.log1p(jnp.exp(-jnp.abs(lam)))
    for h in range(heads):
        cols = pl.ds(h * bw, bw)
        if reverse:
            xr = xr_ref[:, cols]
        else:
            xr = cb_ref[:, cols] + ext_ref[pl.ds(HALO - CONV_LEFT, tm), cols] * cw_ref[0:1, cols]
            for k in range(1, CONV_W):
                xr = xr + ext_ref[pl.ds(HALO - CONV_LEFT + k, tm), cols] * cw_ref[k:k + 1, cols]
            xr_ref[:, cols] = xr
        g = jnp.dot(xr.astype(BF16), gw_ref[h], preferred_element_type=F32)
        r = _sigmoid(g[:, :bw] + gb_ref[0:1, cols])
        i = _sigmoid(g[:, bw:] + gb_ref[1:2, cols])
        log_a = -LRU_C * r * softplus[:, h * bw:(h + 1) * bw]
        a = jnp.exp(log_a)
        a_ref[:, cols] = a
        b_ref[:, cols] = jnp.sqrt(1.0 - a * a) * (i * xr)

    n_chunks = tm // SUBLANES

    def body(c, carry):
        c = n_chunks - 1 - c if reverse else c
        rows = pl.ds(pl.multiple_of(c * SUBLANES, SUBLANES), SUBLANES)
        a_cum, b_cum = _scan8(a_ref[rows, :], b_ref[rows, :], reverse)
        hs = a_cum * carry + b_cum
        if reverse:
            o_ref[rows, :] = (gelu_ref[rows, :] * (recf_ref[rows, :] + hs)).astype(o_ref.dtype)
            edge = hs[0:1, :]
        else:
            o_ref[rows, :] = hs
            edge = hs[SUBLANES - 1:SUBLANES, :]
        return jnp.broadcast_to(edge, carry.shape)

    carry_ref[...] = lax.fori_loop(0, n_chunks, body, carry_ref[...])


def _lru_scan(st, yx, conv_w, conv_b, gate_w, gate_b, lam, reverse, fwd=None):
    n_rows = yx.shape[0]
    d = st.d
    tm = min(256, st.ctx_len)
    assert st.ctx_len == tm
    lat_tiles = st.seq // tm
    ctx_blk0 = st.n_lat // tm
    per = tm // HALO
    last_halo = n_rows // HALO - 1

    def blk(b, s):
        lat = b * lat_tiles + (lat_tiles - s if reverse else s - 1)
        return jnp.where(s == 0, ctx_blk0 + b, lat)

    heads = gate_w.shape[1]
    bw = d // heads
    gw = jnp.concatenate([gate_w[0], gate_w[1]], axis=-1).astype(BF16)
    tile = pl.BlockSpec((tm, d), lambda b, s: (blk(b, s), 0))
    gate_specs = [pl.BlockSpec((heads, bw, 2 * bw), lambda b, s: (0, 0, 0)),
                  pl.BlockSpec((2, d), lambda b, s: (0, 0)), _vec_spec(d)]
    gate_args = [gw, gate_b.reshape(2, d), lam.reshape(1, d)]
    scratch = [pltpu.VMEM((tm, d), F32), pltpu.VMEM((tm, d), F32), pltpu.VMEM((SUBLANES, d), F32)]
    if reverse:
        rec_f, xr = fwd
        in_specs = [tile] + gate_specs + [tile, tile]
        args = [xr] + gate_args + [rec_f, yx]
        out_shape = jax.ShapeDtypeStruct((n_rows, d), BF16)
        out_specs = tile
    else:
        in_specs = [pl.BlockSpec((HALO, d), lambda b, s: (jnp.maximum(blk(b, s) * per - 1, 0), 1)),
                    pl.BlockSpec((tm, d), lambda b, s: (blk(b, s), 1)),
                    pl.BlockSpec((HALO, d), lambda b, s: (jnp.minimum((blk(b, s) + 1) * per, last_halo), 1)),
                    pl.BlockSpec((CONV_W, d), lambda b, s: (0, 0)), _vec_spec(d)] + gate_specs
        args = [yx, yx, yx, conv_w, conv_b.reshape(1, d)] + gate_args
        out_shape = (jax.ShapeDtypeStruct((n_rows, d), F32), jax.ShapeDtypeStruct((n_rows, d), F32))
        out_specs = (tile, tile)
        scratch = [pltpu.VMEM((tm + 2 * HALO, d), F32)] + scratch
    return pl.pallas_call(
        functools.partial(_lru_kernel, tm, lat_tiles, reverse),
        out_shape=out_shape,
        grid=(st.batch, lat_tiles + 1),
        in_specs=in_specs,
        out_specs=out_specs,
        scratch_shapes=scratch,
        compiler_params=_params("arbitrary", "arbitrary"),
        name="rglru_reverse" if reverse else "rglru_forward",
    )(*args)


ROUTE_LANES = 8
MOE_ROW_TILE = 512


def _split_hi_lo(x):
    hi = lax.bitcast_convert_type(lax.bitcast_convert_type(x, jnp.uint32) & jnp.uint32(HI16), F32)
    return hi.astype(jnp.bfloat16), (x - hi).astype(jnp.bfloat16)


def _lane_pick(rec, k):
    lane = lax.broadcasted_iota(jnp.int32, rec.shape, 1)
    return jnp.sum(jnp.where(lane == k, rec, 0.0), axis=1, keepdims=True)


def _router_kernel(x_ref, sh_ref, sc_ref, w_ref, route_ref, cnt_ref, run_ref):
    @pl.when(pl.program_id(0) == 0)
    def _():
        run_ref[...] = jnp.zeros_like(run_ref)

    h = x_ref[...] * (1.0 + sc_ref[0]) + sh_ref[0]
    n_exp = w_ref.shape[1] // 2
    h_hi, h_lo = _split_hi_lo(h)
    t = jnp.dot(h_hi, w_ref[...], preferred_element_type=F32)
    u = jnp.dot(h_lo, w_ref[:, :n_exp], preferred_element_type=F32)
    logits = (t[:, n_exp:] + u) + t[:, :n_exp]
    tm = logits.shape[0]
    lane = lax.broadcasted_iota(jnp.int32, logits.shape, 1).astype(F32)
    m1 = jnp.max(logits, axis=-1, keepdims=True)
    i1 = jnp.min(jnp.where(logits == m1, lane, float(n_exp)), axis=-1, keepdims=True)
    pick1 = lane == i1
    rest = jnp.where(pick1, -jnp.inf, logits)
    m2 = jnp.max(rest, axis=-1, keepdims=True)
    i2 = jnp.min(jnp.where(rest == m2, lane, float(n_exp)), axis=-1, keepdims=True)
    pick2 = lane == i2
    e2 = jnp.exp(m2 - m1)
    den = 1.0 + e2
    onehot = jnp.where(pick1, 1.0, jnp.where(pick2, 1.0, 0.0))
    earlier = (lax.broadcasted_iota(jnp.int32, (tm, tm), 0) > lax.broadcasted_iota(jnp.int32, (tm, tm), 1))
    before = jnp.dot(jnp.where(earlier, 1.0, 0.0).astype(BF16), onehot.astype(BF16),
                     preferred_element_type=F32) + run_ref[...]
    rank1 = jnp.sum(jnp.where(pick1, before, 0.0), axis=-1, keepdims=True)
    rank2 = jnp.sum(jnp.where(pick2, before, 0.0), axis=-1, keepdims=True)
    rec = jnp.zeros((tm, ROUTE_LANES), F32)
    for k, v in enumerate((i1, i2, 1.0 / den, e2 / den, rank1, rank2)):
        rec = jnp.where(lax.broadcasted_iota(jnp.int32, rec.shape, 1) == k, v, rec)
    route_ref[...] = rec
    run_ref[...] += jnp.sum(onehot, axis=0, keepdims=True)
    cnt_ref[...] = run_ref[...]


def _router(st, n_rows, x, mods, layer, w):
    d, n_exp = w.shape
    tm = min(256, st.ctx_len)
    w_split = jnp.concatenate(_split_hi_lo(w), axis=1)
    return pl.pallas_call(
        _router_kernel,
        out_shape=(jax.ShapeDtypeStruct((n_rows, ROUTE_LANES), F32), jax.ShapeDtypeStruct((1, n_exp), F32)),
        grid=(n_rows // tm,),
        in_specs=[pl.BlockSpec((tm, d), lambda i: (i, 0)), st.mod_spec(layer, 3, tm), st.mod_spec(layer, 4, tm),
                  pl.BlockSpec((d, 2 * n_exp), lambda i: (0, 0))],
        out_specs=(pl.BlockSpec((tm, ROUTE_LANES), lambda i: (i, 0)), pl.BlockSpec((1, n_exp), lambda i: (0, 0))),
        scratch_shapes=[pltpu.VMEM((1, n_exp), F32)],
        compiler_params=_params("arbitrary"),
        name="moe_router",
    )(x, mods, mods, w_split)


def _dispatch_plan(route, counts, tm):
    n_tok = route.shape[0]
    n_exp = counts.shape[1]
    expert = route[:, 0:2].astype(jnp.int32)
    rank = route[:, 4:6].astype(jnp.int32)
    cnt = counts[0].astype(jnp.int32)
    padded = (cnt + tm - 1) // tm * tm
    ends = jnp.cumsum(padded)
    starts = ends - padded
    dest = (starts[expert] + rank).T.reshape(2 * n_tok)
    n_tiles = (2 * n_tok + n_exp * tm) // tm
    tile_row0 = jnp.arange(n_tiles, dtype=jnp.int32) * tm
    tile_expert = jnp.minimum(jnp.sum(tile_row0[:, None] >= ends[None, :], axis=1), n_exp - 1).astype(jnp.int32)
    n_used = (ends[-1:] // tm).astype(jnp.int32)
    return dest, tile_expert, n_used, n_tiles


HI16 = 0xFFFF0000
ROW_DMA_UNROLL = 8


def _pack_bf16_pairs(lo, hi):
    lo = lax.bitcast_convert_type(lo.astype(jnp.bfloat16).astype(F32), jnp.uint32)
    hi = lax.bitcast_convert_type(hi.astype(jnp.bfloat16).astype(F32), jnp.uint32)
    return (lo >> 16) | (hi & jnp.uint32(HI16))


def _unpack_bf16_pairs(p):
    return lax.bitcast_convert_type(p << 16, F32), lax.bitcast_convert_type(p & jnp.uint32(HI16), F32)


def _dispatch_kernel(tm, n_tok, dest_ref, x_ref, sh_ref, sc_ref, init_ref, o_ref, h_ref, sem):
    del init_ref
    base = pl.program_id(0) * tm
    h = x_ref[...] * (1.0 + sc_ref[0]) + sh_ref[0]
    half = h.shape[1] // 2
    h_ref[...] = _pack_bf16_pairs(h[:, :half], h[:, half:])

    def row_copy(r, d):
        return pltpu.make_async_copy(h_ref.at[pl.ds(r, 1), :], o_ref.at[pl.ds(d, 1), :], sem)

    def issue(r, carry):
        row_copy(r, dest_ref[base + r]).start()
        row_copy(r, dest_ref[n_tok + base + r]).start()
        return carry

    def drain(r, carry):
        row_copy(r, 0).wait()
        row_copy(r, 0).wait()
        return carry

    lax.fori_loop(0, tm, issue, 0, unroll=ROW_DMA_UNROLL)
    lax.fori_loop(0, tm, drain, 0, unroll=ROW_DMA_UNROLL)


def _dispatch(st, n_rows, x, mods, layer, dest, n_out):
    d = st.d
    tm = min(256, st.ctx_len)
    grid_spec = pltpu.PrefetchScalarGridSpec(
        num_scalar_prefetch=1,
        grid=(n_rows // tm,),
        in_specs=[pl.BlockSpec((tm, d), lambda i, dest: (i, 0)), st.mod_spec(layer, 3, tm), st.mod_spec(layer, 4, tm),
                  pl.BlockSpec(memory_space=pl.ANY)],
        out_specs=pl.BlockSpec(memory_space=pl.ANY),
        scratch_shapes=[pltpu.VMEM((tm, d // 2), jnp.uint32), pltpu.SemaphoreType.DMA],
    )
    return pl.pallas_call(
        functools.partial(_dispatch_kernel, tm, n_rows),
        out_shape=jax.ShapeDtypeStruct((n_out, d // 2), jnp.uint32),
        grid_spec=grid_spec,
        input_output_aliases={4: 0},
        compiler_params=_params("arbitrary"),
        name="moe_dispatch",
    )(dest, x, mods, mods, jnp.zeros((n_out, d // 2), jnp.uint32))


def _moe_up_kernel(te_ref, nu_ref, a_ref, wg_ref, wu_ref, o_ref, wbf_ref):
    i = pl.program_id(1)

    @pl.when(jnp.logical_or(i == 0, te_ref[i] != te_ref[jnp.maximum(i - 1, 0)]))
    def _():
        wbf_ref[0] = wg_ref[...].astype(wbf_ref.dtype)
        wbf_ref[1] = wu_ref[...].astype(wbf_ref.dtype)

    @pl.when(i < nu_ref[0])
    def _():
        lo, hi = (t.astype(jnp.bfloat16) for t in _unpack_bf16_pairs(a_ref[...]))
        half = lo.shape[1]

        def proj(which):
            return (jnp.dot(lo, wbf_ref[which, :half, :], preferred_element_type=F32)
                    + jnp.dot(hi, wbf_ref[which, half:, :], preferred_element_type=F32))

        gate, up = proj(0), proj(1)
        o_ref[...] = (gate * jax.nn.sigmoid(gate) * up).astype(o_ref.dtype)

    @pl.when(i >= nu_ref[0])
    def _():
        o_ref[...] = jnp.zeros_like(o_ref)


def _moe_up(xg, w, layer, tile_expert, n_used, tm):
    n_out = xg.shape[0]
    d, f2 = w.shape[2:]
    f = f2 // 2
    tn = _pick(f, (512, 256, 128))
    nj = f // tn
    grid_spec = pltpu.PrefetchScalarGridSpec(
        num_scalar_prefetch=2,
        grid=(nj, n_out // tm),
        in_specs=[pl.BlockSpec((tm, d // 2), lambda j, i, te, nu: (i, 0)),
                  pl.BlockSpec((None, None, d, tn), lambda j, i, te, nu: (layer, te[i], 0, j)),
                  pl.BlockSpec((None, None, d, tn), lambda j, i, te, nu: (layer, te[i], 0, nj + j))],
        out_specs=pl.BlockSpec((tm, tn), lambda j, i, te, nu: (i, j)),
        scratch_shapes=[pltpu.VMEM((2, d, tn), BF16)],
    )
    return pl.pallas_call(
        _moe_up_kernel,
        out_shape=jax.ShapeDtypeStruct((n_out, f), BF16),
        grid_spec=grid_spec,
        compiler_params=_params("arbitrary", "arbitrary"),
        name="moe_up",
    )(tile_expert, n_used, xg, w, w)


def _moe_down_kernel(te_ref, nu_ref, a_ref, w_ref, o_ref):
    i = pl.program_id(1)

    @pl.when(i < nu_ref[0])
    def _():
        a = a_ref[...]
        half = w_ref.shape[2] // 2
        o_ref[...] = _pack_bf16_pairs(jnp.dot(a, w_ref[0, :, :half], preferred_element_type=F32),
                                      jnp.dot(a, w_ref[0, :, half:], preferred_element_type=F32))

    @pl.when(i >= nu_ref[0])
    def _():
        o_ref[...] = jnp.zeros_like(o_ref)


def _moe_down(act, w, tile_expert, n_used, tm):
    n_out, f = act.shape
    d = w.shape[2]
    grid_spec = pltpu.PrefetchScalarGridSpec(
        num_scalar_prefetch=2,
        grid=(1, n_out // tm),
        in_specs=[pl.BlockSpec((tm, f), lambda j, i, te, nu: (i, 0)),
                  pl.BlockSpec((1, f, d), lambda j, i, te, nu: (te[i], 0, 0))],
        out_specs=pl.BlockSpec((tm, d // 2), lambda j, i, te, nu: (i, 0)),
    )
    return pl.pallas_call(
        _moe_down_kernel,
        out_shape=jax.ShapeDtypeStruct((n_out, d // 2), jnp.uint32),
        grid_spec=grid_spec,
        compiler_params=_params("arbitrary", "arbitrary"),
        name="moe_down",
    )(tile_expert, n_used, act, w)


COMBINE_ROW_CHUNK = 128


def _combine_kernel(tm, n_tok, alpha, has_next, dest_ref, *refs):
    route_ref, x_ref, gate_ref, lng_ref, lnb_ref = refs[:5]
    next_refs = refs[5:7] if has_next else None
    refs = refs[7:] if has_next else refs[5:]
    y_ref, xo_ref = refs[:2]
    ho_ref = refs[2] if has_next else None
    buf_ref, sem = refs[-2:]
    i = pl.program_id(0)
    n_steps = pl.num_programs(0)
    slot = i % 2

    def row_copy(slot, which, r, src):
        return pltpu.make_async_copy(y_ref.at[pl.ds(src, 1), :], buf_ref.at[slot, which, pl.ds(r, 1), :],
                                     sem.at[slot])

    def gather(tile, slot):
        def issue(r, carry):
            row_copy(slot, 0, r, dest_ref[tile * tm + r]).start()
            row_copy(slot, 1, r, dest_ref[n_tok + tile * tm + r]).start()
            return carry
        lax.fori_loop(0, tm, issue, 0, unroll=ROW_DMA_UNROLL)

    @pl.when(i == 0)
    def _():
        gather(0, 0)

    @pl.when(i + 1 < n_steps)
    def _():
        gather(i + 1, 1 - slot)

    def drain(r, carry):
        row_copy(slot, 0, r, 0).wait()
        row_copy(slot, 1, r, 0).wait()
        return carry

    lax.fori_loop(0, tm, drain, 0, unroll=ROW_DMA_UNROLL)
    rc = min(COMBINE_ROW_CHUNK, tm)

    def body(c, carry):
        rows = pl.ds(pl.multiple_of(c * rc, rc), rc)
        rec = route_ref[rows, :]
        g1, g2 = _lane_pick(rec, 2), _lane_pick(rec, 3)
        lo1, hi1 = _unpack_bf16_pairs(buf_ref[slot, 0, rows, :])
        lo2, hi2 = _unpack_bf16_pairs(buf_ref[slot, 1, rows, :])
        y = jnp.concatenate([g1 * lo1 + g2 * lo2, g1 * hi1 + g2 * hi2], axis=1)
        _deepnorm_epilogue(alpha, y, x_ref, gate_ref, lng_ref, lnb_ref, next_refs, xo_ref, ho_ref, rows)
        return carry

    lax.fori_loop(0, tm // rc, body, 0)


def _combine(st, n_rows, alpha, yg, route, dest, x, mods, layer, lng, lnb, next_mod=None):
    d = st.d
    tm = min(256, st.ctx_len)
    has_next = next_mod is not None
    row = lambda i, dest: (i, 0)
    in_specs = [pl.BlockSpec((tm, ROUTE_LANES), row), pl.BlockSpec((tm, d), row), st.mod_spec(layer, 5, tm),
                _vec_spec(d), _vec_spec(d)]
    args = [route, x, mods, lng.reshape(1, d), lnb.reshape(1, d)]
    out_shape = [jax.ShapeDtypeStruct((n_rows, d), F32)]
    out_specs = [pl.BlockSpec((tm, d), row)]
    if has_next:
        nl, n_sh, n_sc = next_mod
        in_specs += [st.mod_spec(nl, n_sh, tm), st.mod_spec(nl, n_sc, tm)]
        args += [mods, mods]
        out_shape.append(jax.ShapeDtypeStruct((n_rows, d), BF16))
        out_specs.append(pl.BlockSpec((tm, d), row))
    in_specs.append(pl.BlockSpec(memory_space=pl.ANY))
    args.append(yg)
    grid_spec = pltpu.PrefetchScalarGridSpec(
        num_scalar_prefetch=1,
        grid=(n_rows // tm,),
        in_specs=in_specs,
        out_specs=tuple(out_specs),
        scratch_shapes=[pltpu.VMEM((2, 2, tm, d // 2), jnp.uint32), pltpu.SemaphoreType.DMA((2,))],
    )
    res = pl.pallas_call(
        functools.partial(_combine_kernel, tm, n_rows, alpha, has_next),
        out_shape=tuple(out_shape),
        grid_spec=grid_spec,
        compiler_params=_params("arbitrary"),
        name="moe_combine",
    )(dest, *args)
    return res if has_next else (res[0], None)


def _moe_layer(st, n_rows, alpha, x, mods, layer, router_w, w_in, moe_index, w_out, lng, lnb, next_mod):
    route, counts = _router(st, n_rows, x, mods, layer, router_w)
    dest, tile_expert, n_used, n_tiles = _dispatch_plan(route, counts, MOE_ROW_TILE)
    xg = _dispatch(st, n_rows, x, mods, layer, dest, n_tiles * MOE_ROW_TILE)
    act = _moe_up(xg, w_in, moe_index, tile_expert, n_used, MOE_ROW_TILE)
    yg = _moe_down(act, w_out, tile_expert, n_used, MOE_ROW_TILE)
    return _combine(st, n_rows, alpha, yg, route, dest, x, mods, layer, lng, lnb, next_mod)


def _cast_kernel(is_pad, x_ref, o_ref):
    pad = is_pad(pl.program_id(0), pl.program_id(1))

    @pl.when(jnp.logical_not(pad))
    def _():
        o_ref[...] = x_ref[...].astype(o_ref.dtype)

    @pl.when(pad)
    def _():
        o_ref[...] = jnp.zeros_like(o_ref)


def _cast_blocks(w, layer, block, out_blocks, src_of, n_src):
    def in_map(i, j):
        bi, bj = src_of(i, j)
        return (layer, jnp.minimum(bi, n_src[0] - 1), jnp.minimum(bj, n_src[1] - 1))

    def is_pad(i, j):
        bi, bj = src_of(i, j)
        return jnp.logical_or(bi >= n_src[0], bj >= n_src[1])

    return pl.pallas_call(
        functools.partial(_cast_kernel, is_pad),
        out_shape=jax.ShapeDtypeStruct((out_blocks[0] * block[0], out_blocks[1] * block[1]), BF16),
        grid=out_blocks,
        in_specs=[pl.BlockSpec((None,) + block, in_map)],
        out_specs=pl.BlockSpec(block, lambda i, j: (i, j)),
        compiler_params=_params("arbitrary", "arbitrary"),
        name="cast_weights",
    )(w)


def _cast(w, layer):
    w3 = w.reshape(w.shape[0], -1, w.shape[-1])
    rows, cols = w3.shape[1:]
    block = (_pick(rows, (512, 256, 128, 64, 32, 16)), _pick(cols, (4096, 2048, 1024, 512, 256, 128)))
    n = (rows // block[0], cols // block[1])
    return _cast_blocks(w3, layer, block, n, lambda i, j: (i, j), n).reshape(w.shape[1:])


FFN_PAD_BLOCK = 256


def _cast_pad_ffn(w_in, w_out, layer, mult):
    d, f2 = w_in.shape[1:]
    f = f2 // 2
    blk = _pick(f, (FFN_PAD_BLOCK, 128))
    fp = _round_up(f, mult)
    nb, nbp = f // blk, fp // blk
    tr = _pick(d, (2048, 1024, 512, 256, 128))

    def in_src(i, j):
        up = j - nbp
        return (i, jnp.where(j < nbp, jnp.where(j < nb, j, 2 * nb), jnp.where(up < nb, up + nb, 2 * nb)))

    w_in_p = _cast_blocks(w_in, layer, (tr, blk), (d // tr, 2 * nbp), in_src, (d // tr, 2 * nb))
    dc = _pick(d, (4096, 2048, 1024, 512, 256, 128))
    w_out_p = _cast_blocks(w_out, layer, (blk, dc), (nbp, d // dc), lambda i, j: (i, j), (nb, d // dc))
    return w_in_p.reshape(1, d, 2 * fp), w_out_p


def kernel(x, c, ctx, c_ctx, ada_down, ada_up, ada_b, ln_g, ln_b, pool_w, pool_scale, attn_wqkv, attn_q_gain,
           attn_k_gain, attn_wo, lru_w_in, lru_conv_w, lru_conv_b, lru_gate_w, lru_gate_b, lru_lambda, lru_w_out,
           ffn_w_in, ffn_w_out, moe_router, moe_w_in, moe_w_out):
    batch, seq, d = x.shape
    ctx_len = ctx.shape[1]
    depth = ada_down.shape[0]
    assert batch < MOD_ROWS and seq % ctx_len == 0 and ctx_len % SUBLANES == 0
    st = _Stream(batch, seq, ctx_len, d)
    alpha = (2.0 * depth) ** 0.25
    mixers = [i % N_MIXERS for i in range(depth)]
    is_moe = [i % 2 == 1 for i in range(depth)]
    ctx_needed_after = [any(mixers[j] != 0 for j in range(i + 1, depth)) for i in range(depth)]

    cvec = jnp.concatenate([c, c_ctx[None], jnp.zeros((MOD_ROWS - batch - 1, d), F32)], axis=0)
    mods = _adaln(cvec, ada_down, ada_up, ada_b)

    xs = jnp.concatenate([x.reshape(batch * seq, d), ctx.reshape(batch * ctx_len, d)], axis=0)
    n_rows = st.n_all
    h = None
    for i in range(depth):
        mixer = mixers[i]
        mi = mixers[:i].count(mixer)
        fi = is_moe[:i].count(is_moe[i])
        lng, lnb = ln_g[i], ln_b[i]
        ffn_mod = None if is_moe[i] else (i, 3, 4)
        if not ctx_needed_after[i] and mixer == 0 and n_rows != st.n_lat:
            n_rows = st.n_lat
            xs = xs[:n_rows]

        if mixer == 0:
            xs, hf = _pool_layer(st, n_rows, i, alpha, xs, mods, _cast(pool_w, mi), pool_scale[mi],
                                 lng[0], lnb[0])
        elif mixer == 1:
            qkv = _qkv_proj(st, h, _cast(attn_wqkv, mi), attn_q_gain[mi], attn_k_gain[mi])
            o = _attention(st, qkv, d)
            rows_out = n_rows if ctx_needed_after[i] else st.n_lat
            xs, hf = _mm_ln(st, rows_out, alpha, o, _cast(attn_wo, mi), xs, mods, i, 2, lng[0], lnb[0],
                            next_mod=ffn_mod)
            n_rows = rows_out
        else:
            yx = _lru_in_proj(h, _cast(lru_w_in, mi))
            fwd = _lru_scan(st, yx, lru_conv_w[mi], lru_conv_b[mi], lru_gate_w[mi, 0], lru_gate_b[mi, 0],
                            lru_lambda[mi, 0], False)
            m = _lru_scan(st, yx, lru_conv_w[mi], lru_conv_b[mi], lru_gate_w[mi, 1], lru_gate_b[mi, 1],
                          lru_lambda[mi, 1], True, fwd)
            rows_out = n_rows if ctx_needed_after[i] else st.n_lat
            xs, hf = _mm_ln(st, rows_out, alpha, m, _cast(lru_w_out, mi), xs, mods, i, 2, lng[0], lnb[0],
                            next_mod=ffn_mod)
            n_rows = rows_out

        next_mod = (i + 1, 0, 1) if i + 1 < depth and mixers[i + 1] != 0 else None
        if is_moe[i]:
            xs, h = _moe_layer(st, n_rows, alpha, xs, mods, i, moe_router[fi], moe_w_in, fi,
                               _cast(moe_w_out, fi), lng[1], lnb[1], next_mod)
        else:
            w_in_p, w_out_p = _cast_pad_ffn(ffn_w_in, ffn_w_out, fi, 1024)
            act = _swiglu_up(hf, w_in_p, n_rows)
            xs, h = _mm_ln(st, n_rows, alpha, act, w_out_p, xs, mods, i, 5, lng[1], lnb[1], next_mod=next_mod)
    return xs[:st.n_lat].reshape(batch, seq, d)
```

```python
import functools

import jax
import jax.numpy as jnp
from jax import lax
from jax.experimental import pallas as pl
from jax.experimental.pallas import tpu as pltpu

F32 = jnp.float32
BF16 = jnp.bfloat16
HIGHEST = lax.Precision.HIGHEST

HEAD_DIM = 128
AXIS_DIM = HEAD_DIM // 2
GRID_W = 64
ROPE_THETA = 10000.0
LN_EPS = 1e-6
RMS_EPS = 1e-6
LRU_C = 8.0
LOG2E = 1.4426950408889634
CONV_W = 4
CONV_LEFT = CONV_W // 2
POOL_WINDOWS = (2, 4, 8, 16)
N_MOD = 6
N_MIXERS = 3
MOD_ROWS = 8

SUBLANES = 8
LANES = 128
VMEM_LIMIT_BYTES = 56 * 1024 * 1024
HALO = SUBLANES


def _params(*sem):
    return pltpu.CompilerParams(dimension_semantics=sem, vmem_limit_bytes=VMEM_LIMIT_BYTES)


def _pick(n, prefs):
    for p in prefs:
        if n % p == 0:
            return p
    return n


def _round_up(n, m):
    return (n + m - 1) // m * m


def _ln_rows(z, g, b):
    mu = jnp.mean(z, axis=-1, keepdims=True)
    zc = z - mu
    var = jnp.mean(zc * zc, axis=-1, keepdims=True)
    return zc * lax.rsqrt(var + LN_EPS) * g + b


class _Stream:
    def __init__(self, batch, seq, ctx_len, d):
        self.batch, self.seq, self.ctx_len, self.d = batch, seq, ctx_len, d
        self.n_lat = batch * seq
        self.n_all = self.n_lat + batch * ctx_len

    def row_tile(self, largest):
        t = largest
        while self.seq % t or (self.batch * self.ctx_len) % t:
            t //= 2
        return t

    def group(self, row0):
        return jnp.minimum(row0 // self.seq, self.batch)

    def mod_spec(self, layer, which, tm):
        def imap(i, *_):
            return ((layer * MOD_ROWS + self.group(i * tm)) * N_MOD + which, 0, 0)
        return pl.BlockSpec((1, 1, self.d), imap)

    def seq_pos(self, row0):
        is_ctx = row0 >= self.n_lat
        pos = jnp.where(is_ctx, (row0 - self.n_lat) % self.ctx_len, row0 % self.seq)
        return pos, jnp.where(is_ctx, self.ctx_len, self.seq)


def _vec_spec(d):
    return pl.BlockSpec((1, d), lambda *_: (0, 0))


def _halo_specs(tm, width, n_rows, col_block=0):
    per = tm // HALO
    last = n_rows // HALO - 1
    prev = pl.BlockSpec((HALO, width), lambda i, *_: (jnp.maximum(i * per - 1, 0), col_block))
    cur = pl.BlockSpec((tm, width), lambda i, *_: (i, col_block))
    nxt = pl.BlockSpec((HALO, width), lambda i, *_: (jnp.minimum((i + 1) * per, last), col_block))
    return prev, cur, nxt


def _adaln_kernel(c_ref, down_ref, up_ref, b_ref, o_ref):
    c = c_ref[...]
    s = c * jax.nn.sigmoid(c)
    t = jnp.dot(s, down_ref[0], preferred_element_type=F32, precision=HIGHEST)
    o_ref[0] = jnp.dot(t, up_ref[0], preferred_element_type=F32, precision=HIGHEST) + b_ref[0]


def _adaln(cvec, ada_down, ada_up, ada_b):
    depth, d, rank = ada_down.shape
    n = ada_up.shape[-1]
    tn = _pick(n, (4096, 2048, 1024, 512, 256, 128))
    out = pl.pallas_call(
        _adaln_kernel,
        out_shape=jax.ShapeDtypeStruct((depth, MOD_ROWS, n), F32),
        grid=(depth, n // tn),
        in_specs=[
            pl.BlockSpec((MOD_ROWS, d), lambda l, j: (0, 0)),
            pl.BlockSpec((1, d, rank), lambda l, j: (l, 0, 0)),
            pl.BlockSpec((1, rank, tn), lambda l, j: (l, 0, j)),
            pl.BlockSpec((1, 1, tn), lambda l, j: (l, 0, j)),
        ],
        out_specs=pl.BlockSpec((1, MOD_ROWS, tn), lambda l, j: (l, 0, j)),
        compiler_params=_params("arbitrary", "arbitrary"),
        name="adaln",
    )(cvec, ada_down, ada_up, ada_b.reshape(depth, 1, n))
    return out.reshape(depth * MOD_ROWS * N_MOD, 1, d)


POOL_ROW_CHUNK = 128


def _pool_kernel(st, tm, alpha, xp_ref, x_ref, xn_ref, sh_ref, sc_ref, gm_ref, shf_ref, scf_ref,
                 w_ref, ps_ref, lng_ref, lnb_ref, xo_ref, ho_ref, ext_ref, z_ref):
    row0 = pl.program_id(0) * tm
    pos0, seqlen = st.seq_pos(row0)
    first = pos0 == 0
    last = pos0 + tm == seqlen
    one_sc = 1.0 + sc_ref[0]
    sh = sh_ref[0]
    x = x_ref[...]
    ext_ref[pl.ds(HALO, tm), :] = x * one_sc + sh
    ext_ref[pl.ds(0, HALO), :] = jnp.where(first, 0.0, xp_ref[...] * one_sc + sh)
    ext_ref[pl.ds(HALO + tm, HALO), :] = jnp.where(last, 0.0, xn_ref[...] * one_sc + sh)

    pos = pos0 + lax.broadcasted_iota(jnp.int32, (tm, 1), 0)
    groups = len(POOL_WINDOWS)
    gw = st.d // groups
    for g, k in enumerate(POOL_WINDOWS):
        cols = pl.ds(g * gw, gw)
        lo = jnp.maximum(pos - k // 2, 0)
        hi = jnp.minimum(pos + (k - k // 2), seqlen)
        cnt = (hi - lo).astype(F32)
        n_ext = tm + 2 * HALO
        win = ext_ref[:, cols]
        win = win + pltpu.roll(win, 1, 0)
        n = 2
        while n < k:
            win = pltpu.roll(win, n_ext - n // 2, 0) + pltpu.roll(win, n // 2, 0)
            n *= 2
        win = win[HALO:HALO + tm]
        diff = (win / cnt - ext_ref[pl.ds(HALO, tm), cols]).astype(BF16)
        z_ref[:, cols] = jnp.dot(diff, w_ref[g], preferred_element_type=F32) * ps_ref[:, cols]

    rc = min(POOL_ROW_CHUNK, tm)

    def body(c, carry):
        rows = pl.ds(pl.multiple_of(c * rc, rc), rc)
        _deepnorm_epilogue(alpha, z_ref[rows, :], x_ref, gm_ref, lng_ref, lnb_ref, (shf_ref, scf_ref), xo_ref, ho_ref,
                           rows)
        return carry

    lax.fori_loop(0, tm // rc, body, 0)


def _pool_layer(st, n_rows, layer, alpha, x, mods, w, pscale, lng, lnb):
    d = st.d
    tm = min(256, st.ctx_len)
    prev, cur, nxt = _halo_specs(tm, d, n_rows)
    groups, gw, _ = w.shape
    return pl.pallas_call(
        functools.partial(_pool_kernel, st, tm, alpha),
        out_shape=(jax.ShapeDtypeStruct((n_rows, d), F32), jax.ShapeDtypeStruct((n_rows, d), BF16)),
        grid=(n_rows // tm,),
        in_specs=[prev, cur, nxt] + [st.mod_spec(layer, q, tm) for q in (0, 1, 2, 3, 4)] + [
            pl.BlockSpec((groups, gw, gw), lambda i: (0, 0, 0)),
            _vec_spec(d), _vec_spec(d), _vec_spec(d)],
        out_specs=(pl.BlockSpec((tm, d), lambda i: (i, 0)), pl.BlockSpec((tm, d), lambda i: (i, 0))),
        scratch_shapes=[pltpu.VMEM((tm + 2 * HALO, d), F32), pltpu.VMEM((tm, d), F32)],
        compiler_params=_params("arbitrary"),
        name="pool_mixer",
    )(x, x, x, mods, mods, mods, mods, mods, w, pscale.reshape(1, d), lng.reshape(1, d), lnb.reshape(1, d))


def _qkv_kernel(n_qk_tiles, a_ref, b_ref, gain_ref, c_ref, s1_ref, s2_ref, o_ref):
    j = pl.program_id(1)
    acc = jnp.dot(a_ref[...], b_ref[...], preferred_element_type=F32)
    tn = acc.shape[1]

    @pl.when(j < n_qk_tiles)
    def _():
        cos, s1, s2 = c_ref[...], s1_ref[...], s2_ref[...]
        for hh in range(tn // HEAD_DIM):
            cols = slice(hh * HEAD_DIM, (hh + 1) * HEAD_DIM)
            xh = acc[:, cols]
            n = xh * lax.rsqrt(jnp.mean(xh * xh, axis=-1, keepdims=True) + RMS_EPS) * gain_ref[:, cols]
            half = AXIS_DIM // 2
            rot = n * cos + pltpu.roll(n, HEAD_DIM - half, 1) * s1 + pltpu.roll(n, half, 1) * s2
            o_ref[:, cols] = rot.astype(o_ref.dtype)

    @pl.when(j >= n_qk_tiles)
    def _():
        o_ref[...] = acc.astype(o_ref.dtype)


def _rope_tables(seq, tm):
    rows = seq // GRID_W
    r = jnp.repeat(jnp.arange(rows, dtype=F32), GRID_W)
    col = jnp.tile(jnp.arange(GRID_W, dtype=F32), rows)
    inv = ROPE_THETA ** (-jnp.arange(0, AXIS_DIM, 2, dtype=F32) / AXIS_DIM)
    ang_r = r[:, None] * inv
    ang_c = col[:, None] * inv
    zero = jnp.zeros_like(ang_r)
    cos = jnp.concatenate([jnp.cos(ang_r), jnp.cos(ang_r), jnp.cos(ang_c), jnp.cos(ang_c)], axis=-1)
    s1 = jnp.concatenate([-jnp.sin(ang_r), zero, -jnp.sin(ang_c), zero], axis=-1)
    s2 = jnp.concatenate([zero, jnp.sin(ang_r), zero, jnp.sin(ang_c)], axis=-1)
    ident = jnp.ones((tm, HEAD_DIM), F32)
    nul = jnp.zeros((tm, HEAD_DIM), F32)
    return (jnp.concatenate([cos, ident]), jnp.concatenate([s1, nul]), jnp.concatenate([s2, nul]))


def _qkv_proj(st, h, w, q_gain, k_gain):
    n_rows, d = h.shape
    n = w.shape[1]
    kv = (n - d) // 2
    tm = st.row_tile(1024)
    tn = _pick(kv, (1024, 512, 256, 128))
    scale = HEAD_DIM ** -0.5 * LOG2E
    gain = jnp.concatenate([jnp.tile(q_gain * scale, d // HEAD_DIM), jnp.tile(k_gain, kv // HEAD_DIM),
                            jnp.ones((kv,), F32)]).reshape(1, n)
    cos, s1, s2 = _rope_tables(st.seq, tm)
    per_seq = st.seq // tm

    def tab_map(i, j):
        return (jnp.where(i * tm < st.n_lat, i % per_seq, per_seq), 0)

    tab = pl.BlockSpec((tm, HEAD_DIM), tab_map)
    return pl.pallas_call(
        functools.partial(_qkv_kernel, (d + kv) // tn),
        out_shape=jax.ShapeDtypeStruct((n_rows, n), BF16),
        grid=(n_rows // tm, n // tn),
        in_specs=[pl.BlockSpec((tm, d), lambda i, j: (i, 0)), pl.BlockSpec((d, tn), lambda i, j: (0, j)),
                  pl.BlockSpec((1, tn), lambda i, j: (0, j)), tab, tab, tab],
        out_specs=pl.BlockSpec((tm, tn), lambda i, j: (i, j)),
        compiler_params=_params("arbitrary", "arbitrary"),
        name="qkv_proj",
    )(h, w, gain, cos, s1, s2)


def _gelu_tanh(x):
    return 0.5 * x * (1.0 + jnp.tanh(0.7978845608028654 * (x + 0.044715 * (x * x * x))))


def _lru_in_kernel(n_gelu_tiles, a_ref, b_ref, o_ref):
    j = pl.program_id(1)
    acc = jnp.dot(a_ref[...], b_ref[...], preferred_element_type=F32)

    @pl.when(j < n_gelu_tiles)
    def _():
        o_ref[...] = _gelu_tanh(acc)

    @pl.when(j >= n_gelu_tiles)
    def _():
        o_ref[...] = acc


def _lru_in_proj(h, w):
    n_rows, d = h.shape
    n = w.shape[1]
    tm = _pick(n_rows, (1024, 512, 256, 128, 64))
    tn = _pick(n // 2, (1024, 512, 256, 128))
    return pl.pallas_call(
        functools.partial(_lru_in_kernel, (n // 2) // tn),
        out_shape=jax.ShapeDtypeStruct((n_rows, n), F32),
        grid=(n_rows // tm, n // tn),
        in_specs=[pl.BlockSpec((tm, d), lambda i, j: (i, 0)), pl.BlockSpec((d, tn), lambda i, j: (0, j))],
        out_specs=pl.BlockSpec((tm, tn), lambda i, j: (i, j)),
        compiler_params=_params("arbitrary", "arbitrary"),
        name="lru_in_proj",
    )(h, w)


def _swiglu_kernel(a_ref, wg_ref, wu_ref, o_ref):
    a = a_ref[...]
    gate = jnp.dot(a, wg_ref[0], preferred_element_type=F32)
    up = jnp.dot(a, wu_ref[0], preferred_element_type=F32)
    o_ref[...] = (gate * jax.nn.sigmoid(gate) * up).astype(o_ref.dtype)


def _swiglu_up(h, w, n_rows):
    e, d, f2 = w.shape
    f = f2 // 2
    tm = _pick(n_rows, (1024, 512, 256, 128, 64))
    tn = _pick(f, (512, 256, 128))
    nj = f // tn
    return pl.pallas_call(
        _swiglu_kernel,
        out_shape=jax.ShapeDtypeStruct((n_rows, e * f), BF16),
        grid=(n_rows // tm, e, nj),
        in_specs=[pl.BlockSpec((tm, d), lambda i, x, j: (i, 0)),
                  pl.BlockSpec((1, d, tn), lambda i, x, j: (x, 0, j)),
                  pl.BlockSpec((1, d, tn), lambda i, x, j: (x, 0, nj + j))],
        out_specs=pl.BlockSpec((tm, tn), lambda i, x, j: (i, x * nj + j)),
        compiler_params=_params("arbitrary", "arbitrary", "arbitrary"),
        name="swiglu_up",
    )(h, w, w)


def _deepnorm_epilogue(alpha, y, x_ref, gate_ref, lng_ref, lnb_ref, next_refs, xo_ref, ho_ref, rows=slice(None)):
    xn = _ln_rows(alpha * x_ref[rows, :] + gate_ref[0] * y, lng_ref[...], lnb_ref[...])
    xo_ref[rows, :] = xn
    if ho_ref is not None:
        shn_ref, scn_ref = next_refs
        ho_ref[rows, :] = (xn * (1.0 + scn_ref[0]) + shn_ref[0]).astype(BF16)


MM_LN_COL_CHUNK = 1024
MM_LN_ROW_CHUNK = 128
MM_LN_X_SLOTS = 4
MM_LN_OUT_SLOTS = 2


def _mm_ln_kernel(alpha, nk, has_next, *refs):
    a_ref, b_ref, gate_ref, lng_ref, lnb_ref = refs[:5]
    next_refs = refs[5:7] if has_next else None
    refs = refs[7:] if has_next else refs[5:]
    if has_next:
        x_hbm, xo_hbm, ho_hbm, acc_ref, xbuf, obuf, hbuf, xsem, osem, hsem = refs
    else:
        x_hbm, xo_hbm, acc_ref, xbuf, obuf, xsem, osem = refs
        ho_hbm = hbuf = hsem = None
    i, k = pl.program_id(0), pl.program_id(1)
    tm, d = acc_ref.shape
    x_slots, rc = xbuf.shape[:2]
    o_slots = obuf.shape[0]
    n_chunks = tm // rc

    def hbm_rows(ref, c):
        return ref.at[pl.ds(pl.multiple_of(i * tm + c * rc, rc), rc), :]

    def x_copy(c, slot):
        return pltpu.make_async_copy(hbm_rows(x_hbm, c), xbuf.at[slot], xsem.at[slot])

    def o_copy(c, slot):
        return pltpu.make_async_copy(obuf.at[slot], hbm_rows(xo_hbm, c), osem.at[slot])

    def h_copy(c, slot):
        return pltpu.make_async_copy(hbuf.at[slot], hbm_rows(ho_hbm, c), hsem.at[slot])

    def wait_out(c, slot):
        o_copy(c, slot).wait()
        if has_next:
            h_copy(c, slot).wait()

    @pl.when(k == 0)
    def _():
        acc_ref[...] = jnp.zeros_like(acc_ref)

    @pl.when(k == nk - 1)
    def _():
        for c in range(min(x_slots, n_chunks)):
            x_copy(c, c).start()

    a = a_ref[...]
    cw = min(MM_LN_COL_CHUNK, d)
    for c in range(d // cw):
        cols = slice(c * cw, (c + 1) * cw)
        acc_ref[:, cols] += jnp.dot(a, b_ref[:, cols], preferred_element_type=F32)

    @pl.when(k == nk - 1)
    def _():
        def body(c, carry):
            xs, os_ = c % x_slots, c % o_slots
            x_copy(c, xs).wait()

            @pl.when(c >= o_slots)
            def _():
                wait_out(c - o_slots, os_)

            y = acc_ref[pl.ds(pl.multiple_of(c * rc, rc), rc), :]
            _deepnorm_epilogue(alpha, y, xbuf.at[xs], gate_ref, lng_ref, lnb_ref, next_refs, obuf.at[os_],
                               hbuf.at[os_] if has_next else None)
            o_copy(c, os_).start()
            if has_next:
                h_copy(c, os_).start()

            @pl.when(c + x_slots < n_chunks)
            def _():
                x_copy(c + x_slots, xs).start()

            return carry

        lax.fori_loop(0, n_chunks, body, 0)
        for c in range(max(n_chunks - o_slots, 0), n_chunks):
            wait_out(c, c % o_slots)


def _mm_ln(st, n_rows, alpha, a, b, x, mods, layer, gate_idx, lng, lnb, next_mod=None):
    kdim, d = b.shape
    tm = st.row_tile(1024)
    tk = _pick(kdim, (1024, 512, 256, 128))
    nk = kdim // tk
    rc = min(MM_LN_ROW_CHUNK, tm)
    has_next = next_mod is not None
    any_spec = pl.BlockSpec(memory_space=pl.ANY)
    in_specs = [pl.BlockSpec((tm, tk), lambda i, k: (i, k)), pl.BlockSpec((tk, d), lambda i, k: (k, 0)),
                st.mod_spec(layer, gate_idx, tm), _vec_spec(d), _vec_spec(d)]
    args = [a, b, mods, lng.reshape(1, d), lnb.reshape(1, d)]
    out_shape = [jax.ShapeDtypeStruct((n_rows, d), F32)]
    scratch = [pltpu.VMEM((tm, d), F32), pltpu.VMEM((MM_LN_X_SLOTS, rc, d), F32),
               pltpu.VMEM((MM_LN_OUT_SLOTS, rc, d), F32)]
    sems = [pltpu.SemaphoreType.DMA((MM_LN_X_SLOTS,)), pltpu.SemaphoreType.DMA((MM_LN_OUT_SLOTS,))]
    if has_next:
        nl, n_sh, n_sc = next_mod
        in_specs += [st.mod_spec(nl, n_sh, tm), st.mod_spec(nl, n_sc, tm)]
        args += [mods, mods]
        out_shape.append(jax.ShapeDtypeStruct((n_rows, d), BF16))
        scratch.append(pltpu.VMEM((MM_LN_OUT_SLOTS, rc, d), BF16))
        sems.append(pltpu.SemaphoreType.DMA((MM_LN_OUT_SLOTS,)))
    res = pl.pallas_call(
        functools.partial(_mm_ln_kernel, alpha, nk, has_next),
        out_shape=tuple(out_shape),
        grid=(n_rows // tm, nk),
        in_specs=in_specs + [any_spec],
        out_specs=tuple(any_spec for _ in out_shape),
        scratch_shapes=scratch + sems,
        compiler_params=_params("arbitrary", "arbitrary"),
        name="matmul_deepnorm",
    )(*args, x)
    return res if has_next else (res[0], None)


ATTN_KEY_ALIGN = 256
ATTN_KEY_RANGES = 2


def _attn_kernel(n_lat_tiles, group, q_ref, kl_ref, vl_ref, kc_ref, vc_ref, o_ref, k_all, v_all):
    qi = pl.program_id(2)
    seq, ctx_len = kl_ref.shape[0], kc_ref.shape[0]

    @pl.when(qi == 0)
    def _():
        k_all[pl.ds(0, seq), :] = kl_ref[...]
        k_all[pl.ds(seq, ctx_len), :] = kc_ref[...]
        v_all[pl.ds(0, seq), pl.ds(0, HEAD_DIM)] = vl_ref[...]
        v_all[pl.ds(seq, ctx_len), pl.ds(0, HEAD_DIM)] = vc_ref[...]
        v_all[:, pl.ds(HEAD_DIM, HEAD_DIM)] = jnp.ones((seq + ctx_len, HEAD_DIM), v_all.dtype)

    def partial_softmax(q, rows):
        s = lax.dot_general(q, k_all[rows, :], (((1,), (1,)), ((), ())), preferred_element_type=F32)
        m = jnp.max(s, axis=-1, keepdims=True)
        return jnp.dot(jnp.exp2(s - m).astype(v_all.dtype), v_all[rows, :], preferred_element_type=F32), m

    def attend(key_ranges):
        for g in range(group):
            cols = slice(g * HEAD_DIM, (g + 1) * HEAD_DIM)
            q = q_ref[:, cols]
            parts = [partial_softmax(q, rows) for rows in key_ranges]
            o, m = parts[0]
            for o2, m2 in parts[1:]:
                m_new = jnp.maximum(m, m2)
                o = o * jnp.exp2(m - m_new) + o2 * jnp.exp2(m2 - m_new)
                m = m_new
            o_ref[:, cols] = (o[:, :HEAD_DIM] / o[:, HEAD_DIM:]).astype(o_ref.dtype)

    @pl.when(qi < n_lat_tiles)
    def _():
        n_keys = seq + ctx_len
        step = _round_up(-(-n_keys // ATTN_KEY_RANGES), ATTN_KEY_ALIGN)
        attend([pl.ds(lo, min(step, n_keys - lo)) for lo in range(0, n_keys, step)])

    @pl.when(qi >= n_lat_tiles)
    def _():
        attend([pl.ds(seq, ctx_len)])


def _attention(st, qkv, d):
    n_rows, n = qkv.shape
    kv = (n - d) // 2
    n_kv = kv // HEAD_DIM
    group = d // kv
    tq = min(256, st.ctx_len)
    ctx_tiles = st.ctx_len // tq
    lat_tiles = st.seq // tq
    qw = group * HEAD_DIM
    k_col0 = d // HEAD_DIM
    v_col0 = (d + kv) // HEAD_DIM

    def q_map(b, h, qi):
        row = jnp.where(qi < lat_tiles, b * lat_tiles + qi, st.n_lat // tq + b * ctx_tiles + (qi - lat_tiles))
        return (row, h)

    ctx_blk0 = st.n_lat // st.ctx_len
    return pl.pallas_call(
        functools.partial(_attn_kernel, lat_tiles, group),
        out_shape=jax.ShapeDtypeStruct((n_rows, d), BF16),
        grid=(st.batch, n_kv, lat_tiles + ctx_tiles),
        in_specs=[pl.BlockSpec((tq, qw), q_map),
                  pl.BlockSpec((st.seq, HEAD_DIM), lambda b, h, qi: (b, k_col0 + h)),
                  pl.BlockSpec((st.seq, HEAD_DIM), lambda b, h, qi: (b, v_col0 + h)),
                  pl.BlockSpec((st.ctx_len, HEAD_DIM), lambda b, h, qi: (ctx_blk0 + b, k_col0 + h)),
                  pl.BlockSpec((st.ctx_len, HEAD_DIM), lambda b, h, qi: (ctx_blk0 + b, v_col0 + h))],
        out_specs=pl.BlockSpec((tq, qw), q_map),
        scratch_shapes=[pltpu.VMEM((st.seq + st.ctx_len, HEAD_DIM), qkv.dtype),
                        pltpu.VMEM((st.seq + st.ctx_len, 2 * HEAD_DIM), qkv.dtype)],
        compiler_params=_params("arbitrary", "arbitrary", "arbitrary"),
        name="gqa_attention",
    )(qkv, qkv, qkv, qkv, qkv)


def _sigmoid(x):
    return 0.5 * jnp.tanh(0.5 * x) + 0.5


def _scan8(a, b, reverse):
    row = lax.broadcasted_iota(jnp.int32, a.shape, 0)
    for s in (1, 2, 4):
        if reverse:
            keep = row < SUBLANES - s
            shift = SUBLANES - s
        else:
            keep = row >= s
            shift = s
        a_sh = jnp.where(keep, pltpu.roll(a, shift, 0), 1.0)
        b_sh = jnp.where(keep, pltpu.roll(b, shift, 0), 0.0)
        b = a * b_sh + b
        a = a * a_sh
    return a, b


def _lru_kernel(tm, lat_tiles, reverse, *refs):
    if reverse:
        xr_ref, gw_ref, gb_ref, lam_ref, recf_ref, gelu_ref, o_ref, a_ref, b_ref, carry_ref = refs
    else:
        (xp_ref, x_ref, xn_ref, cw_ref, cb_ref, gw_ref, gb_ref, lam_ref,
         o_ref, xr_ref, ext_ref, a_ref, b_ref, carry_ref) = refs
    s = pl.program_id(1)
    is_ctx = s == 0

    @pl.when(is_ctx)
    def _():
        carry_ref[...] = jnp.zeros_like(carry_ref)

    if not reverse:
        tile = s - 1
        first = jnp.logical_or(is_ctx, tile == 0)
        last = jnp.logical_or(is_ctx, tile == lat_tiles - 1)
        ext_ref[pl.ds(HALO, tm), :] = x_ref[...]
        ext_ref[pl.ds(0, HALO), :] = jnp.where(first, 0.0, xp_ref[...])
        ext_ref[pl.ds(HALO + tm, HALO), :] = jnp.where(last, 0.0, xn_ref[...])

    heads, bw, _ = gw_ref.shape
    lam = lam_ref[...]
    softplus = jnp.maximum(-lam, 0.0) + jnp.log1p(jnp.exp(-jnp.abs(lam)))
    for h in range(heads):
        cols = pl.ds(h * bw, bw)
        if reverse:
            xr = xr_ref[:, cols]
        else:
            xr = cb_ref[:, cols] + ext_ref[pl.ds(HALO - CONV_LEFT, tm), cols] * cw_ref[0:1, cols]
            for k in range(1, CONV_W):
                xr = xr + ext_ref[pl.ds(HALO - CONV_LEFT + k, tm), cols] * cw_ref[k:k + 1, cols]
            xr_ref[:, cols] = xr
        g = jnp.dot(xr.astype(BF16), gw_ref[h], preferred_element_type=F32)
        r = _sigmoid(g[:, :bw] + gb_ref[0:1, cols])
        i = _sigmoid(g[:, bw:] + gb_ref[1:2, cols])
        log_a = -LRU_C * r * softplus[:, h * bw:(h + 1) * bw]
        a = jnp.exp(log_a)
        a_ref[:, cols] = a
        b_ref[:, cols] = jnp.sqrt(1.0 - a * a) * (i * xr)

    n_chunks = tm // SUBLANES

    def body(c, carry):
        c = n_chunks - 1 - c if reverse else c
        rows = pl.ds(pl.multiple_of(c * SUBLANES, SUBLANES), SUBLANES)
        a_cum, b_cum = _scan8(a_ref[rows, :], b_ref[rows, :], reverse)
        hs = a_cum * carry + b_cum
        if reverse:
            o_ref[rows, :] = (gelu_ref[rows, :] * (recf_ref[rows, :] + hs)).astype(o_ref.dtype)
            edge = hs[0:1, :]
        else:
            o_ref[rows, :] = hs
            edge = hs[SUBLANES - 1:SUBLANES, :]
        return jnp.broadcast_to(edge, carry.shape)

    carry_ref[...] = lax.fori_loop(0, n_chunks, body, carry_ref[...])


def _lru_scan(st, yx, conv_w, conv_b, gate_w, gate_b, lam, reverse, fwd=None):
    n_rows = yx.shape[0]
    d = st.d
    tm = min(256, st.ctx_len)
    assert st.ctx_len == tm
    lat_tiles = st.seq // tm
    ctx_blk0 = st.n_lat // tm
    per = tm // HALO
    last_halo = n_rows // HALO - 1

    def blk(b, s):
        lat = b * lat_tiles + (lat_tiles - s if reverse else s - 1)
        return jnp.where(s == 0, ctx_blk0 + b, lat)

    heads = gate_w.shape[1]
    bw = d // heads
    gw = jnp.concatenate([gate_w[0], gate_w[1]], axis=-1).astype(BF16)
    tile = pl.BlockSpec((tm, d), lambda b, s: (blk(b, s), 0))
    gate_specs = [pl.BlockSpec((heads, bw, 2 * bw), lambda b, s: (0, 0, 0)),
                  pl.BlockSpec((2, d), lambda b, s: (0, 0)), _vec_spec(d)]
    gate_args = [gw, gate_b.reshape(2, d), lam.reshape(1, d)]
    scratch = [pltpu.VMEM((tm, d), F32), pltpu.VMEM((tm, d), F32), pltpu.VMEM((SUBLANES, d), F32)]
    if reverse:
        rec_f, xr = fwd
        in_specs = [tile] + gate_specs + [tile, tile]
        args = [xr] + gate_args + [rec_f, yx]
        out_shape = jax.ShapeDtypeStruct((n_rows, d), BF16)
        out_specs = tile
    else:
        in_specs = [pl.BlockSpec((HALO, d), lambda b, s: (jnp.maximum(blk(b, s) * per - 1, 0), 1)),
                    pl.BlockSpec((tm, d), lambda b, s: (blk(b, s), 1)),
                    pl.BlockSpec((HALO, d), lambda b, s: (jnp.minimum((blk(b, s) + 1) * per, last_halo), 1)),
                    pl.BlockSpec((CONV_W, d), lambda b, s: (0, 0)), _vec_spec(d)] + gate_specs
        args = [yx, yx, yx, conv_w, conv_b.reshape(1, d)] + gate_args
        out_shape = (jax.ShapeDtypeStruct((n_rows, d), F32), jax.ShapeDtypeStruct((n_rows, d), F32))
        out_specs = (tile, tile)
        scratch = [pltpu.VMEM((tm + 2 * HALO, d), F32)] + scratch
    return pl.pallas_call(
        functools.partial(_lru_kernel, tm, lat_tiles, reverse),
        out_shape=out_shape,
        grid=(st.batch, lat_tiles + 1),
        in_specs=in_specs,
        out_specs=out_specs,
        scratch_shapes=scratch,
        compiler_params=_params("arbitrary", "arbitrary"),
        name="rglru_reverse" if reverse else "rglru_forward",
    )(*args)


ROUTE_LANES = 8
MOE_ROW_TILE = 512


def _split_hi_lo(x):
    hi = lax.bitcast_convert_type(lax.bitcast_convert_type(x, jnp.uint32) & jnp.uint32(HI16), F32)
    return hi.astype(jnp.bfloat16), (x - hi).astype(jnp.bfloat16)


def _lane_pick(rec, k):
    lane = lax.broadcasted_iota(jnp.int32, rec.shape, 1)
    return jnp.sum(jnp.where(lane == k, rec, 0.0), axis=1, keepdims=True)


def _router_kernel(x_ref, sh_ref, sc_ref, w_ref, route_ref, cnt_ref, run_ref):
    @pl.when(pl.program_id(0) == 0)
    def _():
        run_ref[...] = jnp.zeros_like(run_ref)

    h = x_ref[...] * (1.0 + sc_ref[0]) + sh_ref[0]
    n_exp = w_ref.shape[1] // 2
    h_hi, h_lo = _split_hi_lo(h)
    t = jnp.dot(h_hi, w_ref[...], preferred_element_type=F32)
    u = jnp.dot(h_lo, w_ref[:, :n_exp], preferred_element_type=F32)
    logits = (t[:, n_exp:] + u) + t[:, :n_exp]
    tm = logits.shape[0]
    lane = lax.broadcasted_iota(jnp.int32, logits.shape, 1).astype(F32)
    m1 = jnp.max(logits, axis=-1, keepdims=True)
    i1 = jnp.min(jnp.where(logits == m1, lane, float(n_exp)), axis=-1, keepdims=True)
    pick1 = lane == i1
    rest = jnp.where(pick1, -jnp.inf, logits)
    m2 = jnp.max(rest, axis=-1, keepdims=True)
    i2 = jnp.min(jnp.where(rest == m2, lane, float(n_exp)), axis=-1, keepdims=True)
    pick2 = lane == i2
    e2 = jnp.exp(m2 - m1)
    den = 1.0 + e2
    onehot = jnp.where(pick1, 1.0, jnp.where(pick2, 1.0, 0.0))
    earlier = (lax.broadcasted_iota(jnp.int32, (tm, tm), 0) > lax.broadcasted_iota(jnp.int32, (tm, tm), 1))
    before = jnp.dot(jnp.where(earlier, 1.0, 0.0).astype(BF16), onehot.astype(BF16),
                     preferred_element_type=F32) + run_ref[...]
    rank1 = jnp.sum(jnp.where(pick1, before, 0.0), axis=-1, keepdims=True)
    rank2 = jnp.sum(jnp.where(pick2, before, 0.0), axis=-1, keepdims=True)
    rec = jnp.zeros((tm, ROUTE_LANES), F32)
    for k, v in enumerate((i1, i2, 1.0 / den, e2 / den, rank1, rank2)):
        rec = jnp.where(lax.broadcasted_iota(jnp.int32, rec.shape, 1) == k, v, rec)
    route_ref[...] = rec
    run_ref[...] += jnp.sum(onehot, axis=0, keepdims=True)
    cnt_ref[...] = run_ref[...]


def _router(st, n_rows, x, mods, layer, w):
    d, n_exp = w.shape
    tm = min(256, st.ctx_len)
    w_split = jnp.concatenate(_split_hi_lo(w), axis=1)
    return pl.pallas_call(
        _router_kernel,
        out_shape=(jax.ShapeDtypeStruct((n_rows, ROUTE_LANES), F32), jax.ShapeDtypeStruct((1, n_exp), F32)),
        grid=(n_rows // tm,),
        in_specs=[pl.BlockSpec((tm, d), lambda i: (i, 0)), st.mod_spec(layer, 3, tm), st.mod_spec(layer, 4, tm),
                  pl.BlockSpec((d, 2 * n_exp), lambda i: (0, 0))],
        out_specs=(pl.BlockSpec((tm, ROUTE_LANES), lambda i: (i, 0)), pl.BlockSpec((1, n_exp), lambda i: (0, 0))),
        scratch_shapes=[pltpu.VMEM((1, n_exp), F32)],
        compiler_params=_params("arbitrary"),
        name="moe_router",
    )(x, mods, mods, w_split)


def _dispatch_plan(route, counts, tm):
    n_tok = route.shape[0]
    n_exp = counts.shape[1]
    expert = route[:, 0:2].astype(jnp.int32)
    rank = route[:, 4:6].astype(jnp.int32)
    cnt = counts[0].astype(jnp.int32)
    padded = (cnt + tm - 1) // tm * tm
    ends = jnp.cumsum(padded)
    starts = ends - padded
    dest = (starts[expert] + rank).T.reshape(2 * n_tok)
    n_tiles = (2 * n_tok + n_exp * tm) // tm
    tile_row0 = jnp.arange(n_tiles, dtype=jnp.int32) * tm
    tile_expert = jnp.minimum(jnp.sum(tile_row0[:, None] >= ends[None, :], axis=1), n_exp - 1).astype(jnp.int32)
    n_used = (ends[-1:] // tm).astype(jnp.int32)
    return dest, tile_expert, n_used, n_tiles


HI16 = 0xFFFF0000
ROW_DMA_UNROLL = 8


def _pack_bf16_pairs(lo, hi):
    lo = lax.bitcast_convert_type(lo.astype(jnp.bfloat16).astype(F32), jnp.uint32)
    hi = lax.bitcast_convert_type(hi.astype(jnp.bfloat16).astype(F32), jnp.uint32)
    return (lo >> 16) | (hi & jnp.uint32(HI16))


def _unpack_bf16_pairs(p):
    return lax.bitcast_convert_type(p << 16, F32), lax.bitcast_convert_type(p & jnp.uint32(HI16), F32)


def _dispatch_kernel(tm, n_tok, dest_ref, x_ref, sh_ref, sc_ref, init_ref, o_ref, h_ref, sem):
    del init_ref
    base = pl.program_id(0) * tm
    h = x_ref[...] * (1.0 + sc_ref[0]) + sh_ref[0]
    half = h.shape[1] // 2
    h_ref[...] = _pack_bf16_pairs(h[:, :half], h[:, half:])

    def row_copy(r, d):
        return pltpu.make_async_copy(h_ref.at[pl.ds(r, 1), :], o_ref.at[pl.ds(d, 1), :], sem)

    def issue(r, carry):
        row_copy(r, dest_ref[base + r]).start()
        row_copy(r, dest_ref[n_tok + base + r]).start()
        return carry

    def drain(r, carry):
        row_copy(r, 0).wait()
        row_copy(r, 0).wait()
        return carry

    lax.fori_loop(0, tm, issue, 0, unroll=ROW_DMA_UNROLL)
    lax.fori_loop(0, tm, drain, 0, unroll=ROW_DMA_UNROLL)


def _dispatch(st, n_rows, x, mods, layer, dest, n_out):
    d = st.d
    tm = min(256, st.ctx_len)
    grid_spec = pltpu.PrefetchScalarGridSpec(
        num_scalar_prefetch=1,
        grid=(n_rows // tm,),
        in_specs=[pl.BlockSpec((tm, d), lambda i, dest: (i, 0)), st.mod_spec(layer, 3, tm), st.mod_spec(layer, 4, tm),
                  pl.BlockSpec(memory_space=pl.ANY)],
        out_specs=pl.BlockSpec(memory_space=pl.ANY),
        scratch_shapes=[pltpu.VMEM((tm, d // 2), jnp.uint32), pltpu.SemaphoreType.DMA],
    )
    return pl.pallas_call(
        functools.partial(_dispatch_kernel, tm, n_rows),
        out_shape=jax.ShapeDtypeStruct((n_out, d // 2), jnp.uint32),
        grid_spec=grid_spec,
        input_output_aliases={4: 0},
        compiler_params=_params("arbitrary"),
        name="moe_dispatch",
    )(dest, x, mods, mods, jnp.zeros((n_out, d // 2), jnp.uint32))


def _moe_up_kernel(te_ref, nu_ref, a_ref, wg_ref, wu_ref, o_ref, wbf_ref):
    i = pl.program_id(1)

    @pl.when(jnp.logical_or(i == 0, te_ref[i] != te_ref[jnp.maximum(i - 1, 0)]))
    def _():
        wbf_ref[0] = wg_ref[...].astype(wbf_ref.dtype)
        wbf_ref[1] = wu_ref[...].astype(wbf_ref.dtype)

    @pl.when(i < nu_ref[0])
    def _():
        lo, hi = (t.astype(jnp.bfloat16) for t in _unpack_bf16_pairs(a_ref[...]))
        half = lo.shape[1]

        def proj(which):
            return (jnp.dot(lo, wbf_ref[which, :half, :], preferred_element_type=F32)
                    + jnp.dot(hi, wbf_ref[which, half:, :], preferred_element_type=F32))

        gate, up = proj(0), proj(1)
        o_ref[...] = (gate * jax.nn.sigmoid(gate) * up).astype(o_ref.dtype)

    @pl.when(i >= nu_ref[0])
    def _():
        o_ref[...] = jnp.zeros_like(o_ref)


def _moe_up(xg, w, layer, tile_expert, n_used, tm):
    n_out = xg.shape[0]
    d, f2 = w.shape[2:]
    f = f2 // 2
    tn = _pick(f, (512, 256, 128))
    nj = f // tn
    grid_spec = pltpu.PrefetchScalarGridSpec(
        num_scalar_prefetch=2,
        grid=(nj, n_out // tm),
        in_specs=[pl.BlockSpec((tm, d // 2), lambda j, i, te, nu: (i, 0)),
                  pl.BlockSpec((None, None, d, tn), lambda j, i, te, nu: (layer, te[i], 0, j)),
                  pl.BlockSpec((None, None, d, tn), lambda j, i, te, nu: (layer, te[i], 0, nj + j))],
        out_specs=pl.BlockSpec((tm, tn), lambda j, i, te, nu: (i, j)),
        scratch_shapes=[pltpu.VMEM((2, d, tn), BF16)],
    )
    return pl.pallas_call(
        _moe_up_kernel,
        out_shape=jax.ShapeDtypeStruct((n_out, f), BF16),
        grid_spec=grid_spec,
        compiler_params=_params("arbitrary", "arbitrary"),
        name="moe_up",
    )(tile_expert, n_used, xg, w, w)


def _moe_down_kernel(te_ref, nu_ref, a_ref, w_ref, o_ref, wbf_ref):
    i = pl.program_id(1)

    @pl.when(jnp.logical_or(i == 0, te_ref[i] != te_ref[jnp.maximum(i - 1, 0)]))
    def _():
        wbf_ref[...] = w_ref[...].astype(wbf_ref.dtype)

    @pl.when(i < nu_ref[0])
    def _():
        a = a_ref[...]
        half = wbf_ref.shape[1] // 2
        o_ref[...] = _pack_bf16_pairs(jnp.dot(a, wbf_ref[:, :half], preferred_element_type=F32),
                                      jnp.dot(a, wbf_ref[:, half:], preferred_element_type=F32))

    @pl.when(i >= nu_ref[0])
    def _():
        o_ref[...] = jnp.zeros_like(o_ref)


MOE_DOWN_COLS = 2048


def _moe_down(act, w, layer, tile_expert, n_used, tm):
    n_out, f = act.shape
    d = w.shape[3]
    tn = min(MOE_DOWN_COLS, d)
    grid_spec = pltpu.PrefetchScalarGridSpec(
        num_scalar_prefetch=2,
        grid=(d // tn, n_out // tm),
        in_specs=[pl.BlockSpec((tm, f), lambda j, i, te, nu: (i, 0)),
                  pl.BlockSpec((None, None, f, tn), lambda j, i, te, nu: (layer, te[i], 0, j))],
        out_specs=pl.BlockSpec((tm, tn // 2), lambda j, i, te, nu: (i, j)),
        scratch_shapes=[pltpu.VMEM((f, tn), BF16)],
    )
    return pl.pallas_call(
        _moe_down_kernel,
        out_shape=jax.ShapeDtypeStruct((n_out, d // 2), jnp.uint32),
        grid_spec=grid_spec,
        compiler_params=_params("arbitrary", "arbitrary"),
        name="moe_down",
    )(tile_expert, n_used, act, w)


def _unpack_expert_rows(p, d):
    tn = min(MOE_DOWN_COLS, d)
    lo, hi = _unpack_bf16_pairs(p)
    half = tn // 2
    parts = []
    for j in range(d // tn):
        parts += [lo[:, j * half:(j + 1) * half], hi[:, j * half:(j + 1) * half]]
    return parts


COMBINE_ROW_CHUNK = 128


def _combine_kernel(tm, n_tok, alpha, has_next, dest_ref, *refs):
    route_ref, x_ref, gate_ref, lng_ref, lnb_ref = refs[:5]
    next_refs = refs[5:7] if has_next else None
    refs = refs[7:] if has_next else refs[5:]
    y_ref, xo_ref = refs[:2]
    ho_ref = refs[2] if has_next else None
    buf_ref, sem = refs[-2:]
    i = pl.program_id(0)
    n_steps = pl.num_programs(0)
    slot = i % 2

    def row_copy(slot, which, r, src):
        return pltpu.make_async_copy(y_ref.at[pl.ds(src, 1), :], buf_ref.at[slot, which, pl.ds(r, 1), :],
                                     sem.at[slot])

    def gather(tile, slot):
        def issue(r, carry):
            row_copy(slot, 0, r, dest_ref[tile * tm + r]).start()
            row_copy(slot, 1, r, dest_ref[n_tok + tile * tm + r]).start()
            return carry
        lax.fori_loop(0, tm, issue, 0, unroll=ROW_DMA_UNROLL)

    @pl.when(i == 0)
    def _():
        gather(0, 0)

    @pl.when(i + 1 < n_steps)
    def _():
        gather(i + 1, 1 - slot)

    def drain(r, carry):
        row_copy(slot, 0, r, 0).wait()
        row_copy(slot, 1, r, 0).wait()
        return carry

    lax.fori_loop(0, tm, drain, 0, unroll=ROW_DMA_UNROLL)
    rc = min(COMBINE_ROW_CHUNK, tm)

    def body(c, carry):
        rows = pl.ds(pl.multiple_of(c * rc, rc), rc)
        rec = route_ref[rows, :]
        g1, g2 = _lane_pick(rec, 2), _lane_pick(rec, 3)
        d = x_ref.shape[1]
        y1 = _unpack_expert_rows(buf_ref[slot, 0, rows, :], d)
        y2 = _unpack_expert_rows(buf_ref[slot, 1, rows, :], d)
        y = jnp.concatenate([g1 * p1 + g2 * p2 for p1, p2 in zip(y1, y2)], axis=1)
        _deepnorm_epilogue(alpha, y, x_ref, gate_ref, lng_ref, lnb_ref, next_refs, xo_ref, ho_ref, rows)
        return carry

    lax.fori_loop(0, tm // rc, body, 0)


def _combine(st, n_rows, alpha, yg, route, dest, x, mods, layer, lng, lnb, next_mod=None):
    d = st.d
    tm = min(256, st.ctx_len)
    has_next = next_mod is not None
    row = lambda i, dest: (i, 0)
    in_specs = [pl.BlockSpec((tm, ROUTE_LANES), row), pl.BlockSpec((tm, d), row), st.mod_spec(layer, 5, tm),
                _vec_spec(d), _vec_spec(d)]
    args = [route, x, mods, lng.reshape(1, d), lnb.reshape(1, d)]
    out_shape = [jax.ShapeDtypeStruct((n_rows, d), F32)]
    out_specs = [pl.BlockSpec((tm, d), row)]
    if has_next:
        nl, n_sh, n_sc = next_mod
        in_specs += [st.mod_spec(nl, n_sh, tm), st.mod_spec(nl, n_sc, tm)]
        args += [mods, mods]
        out_shape.append(jax.ShapeDtypeStruct((n_rows, d), BF16))
        out_specs.append(pl.BlockSpec((tm, d), row))
    in_specs.append(pl.BlockSpec(memory_space=pl.ANY))
    args.append(yg)
    grid_spec = pltpu.PrefetchScalarGridSpec(
        num_scalar_prefetch=1,
        grid=(n_rows // tm,),
        in_specs=in_specs,
        out_specs=tuple(out_specs),
        scratch_shapes=[pltpu.VMEM((2, 2, tm, d // 2), jnp.uint32), pltpu.SemaphoreType.DMA((2,))],
    )
    res = pl.pallas_call(
        functools.partial(_combine_kernel, tm, n_rows, alpha, has_next),
        out_shape=tuple(out_shape),
        grid_spec=grid_spec,
        compiler_params=_params("arbitrary"),
        name="moe_combine",
    )(dest, *args)
    return res if has_next else (res[0], None)


def _moe_layer(st, n_rows, alpha, x, mods, layer, router_w, w_in, moe_index, w_out, lng, lnb, next_mod):
    route, counts = _router(st, n_rows, x, mods, layer, router_w)
    dest, tile_expert, n_used, n_tiles = _dispatch_plan(route, counts, MOE_ROW_TILE)
    xg = _dispatch(st, n_rows, x, mods, layer, dest, n_tiles * MOE_ROW_TILE)
    act = _moe_up(xg, w_in, moe_index, tile_expert, n_used, MOE_ROW_TILE)
    yg = _moe_down(act, w_out, moe_index, tile_expert, n_used, MOE_ROW_TILE)
    return _combine(st, n_rows, alpha, yg, route, dest, x, mods, layer, lng, lnb, next_mod)


def _cast_kernel(is_pad, x_ref, o_ref):
    pad = is_pad(pl.program_id(0), pl.program_id(1))

    @pl.when(jnp.logical_not(pad))
    def _():
        o_ref[...] = x_ref[...].astype(o_ref.dtype)

    @pl.when(pad)
    def _():
        o_ref[...] = jnp.zeros_like(o_ref)


def _cast_blocks(w, layer, block, out_blocks, src_of, n_src):
    def in_map(i, j):
        bi, bj = src_of(i, j)
        return (layer, jnp.minimum(bi, n_src[0] - 1), jnp.minimum(bj, n_src[1] - 1))

    def is_pad(i, j):
        bi, bj = src_of(i, j)
        return jnp.logical_or(bi >= n_src[0], bj >= n_src[1])

    return pl.pallas_call(
        functools.partial(_cast_kernel, is_pad),
        out_shape=jax.ShapeDtypeStruct((out_blocks[0] * block[0], out_blocks[1] * block[1]), BF16),
        grid=out_blocks,
        in_specs=[pl.BlockSpec((None,) + block, in_map)],
        out_specs=pl.BlockSpec(block, lambda i, j: (i, j)),
        compiler_params=_params("arbitrary", "arbitrary"),
        name="cast_weights",
    )(w)


def _cast(w, layer):
    w3 = w.reshape(w.shape[0], -1, w.shape[-1])
    rows, cols = w3.shape[1:]
    block = (_pick(rows, (512, 256, 128, 64, 32, 16)), _pick(cols, (4096, 2048, 1024, 512, 256, 128)))
    n = (rows // block[0], cols // block[1])
    return _cast_blocks(w3, layer, block, n, lambda i, j: (i, j), n).reshape(w.shape[1:])


FFN_PAD_BLOCK = 256


def _cast_pad_ffn(w_in, w_out, layer, mult):
    d, f2 = w_in.shape[1:]
    f = f2 // 2
    blk = _pick(f, (FFN_PAD_BLOCK, 128))
    fp = _round_up(f, mult)
    nb, nbp = f // blk, fp // blk
    tr = _pick(d, (2048, 1024, 512, 256, 128))

    def in_src(i, j):
        up = j - nbp
        return (i, jnp.where(j < nbp, jnp.where(j < nb, j, 2 * nb), jnp.where(up < nb, up + nb, 2 * nb)))

    w_in_p = _cast_blocks(w_in, layer, (tr, blk), (d // tr, 2 * nbp), in_src, (d // tr, 2 * nb))
    dc = _pick(d, (4096, 2048, 1024, 512, 256, 128))
    w_out_p = _cast_blocks(w_out, layer, (blk, dc), (nbp, d // dc), lambda i, j: (i, j), (nb, d // dc))
    return w_in_p.reshape(1, d, 2 * fp), w_out_p


def kernel(x, c, ctx, c_ctx, ada_down, ada_up, ada_b, ln_g, ln_b, pool_w, pool_scale, attn_wqkv, attn_q_gain,
           attn_k_gain, attn_wo, lru_w_in, lru_conv_w, lru_conv_b, lru_gate_w, lru_gate_b, lru_lambda, lru_w_out,
           ffn_w_in, ffn_w_out, moe_router, moe_w_in, moe_w_out):
    batch, seq, d = x.shape
    ctx_len = ctx.shape[1]
    depth = ada_down.shape[0]
    assert batch < MOD_ROWS and seq % ctx_len == 0 and ctx_len % SUBLANES == 0
    st = _Stream(batch, seq, ctx_len, d)
    alpha = (2.0 * depth) ** 0.25
    mixers = [i % N_MIXERS for i in range(depth)]
    is_moe = [i % 2 == 1 for i in range(depth)]
    ctx_needed_after = [any(mixers[j] != 0 for j in range(i + 1, depth)) for i in range(depth)]

    cvec = jnp.concatenate([c, c_ctx[None], jnp.zeros((MOD_ROWS - batch - 1, d), F32)], axis=0)
    mods = _adaln(cvec, ada_down, ada_up, ada_b)

    xs = jnp.concatenate([x.reshape(batch * seq, d), ctx.reshape(batch * ctx_len, d)], axis=0)
    n_rows = st.n_all
    h = None
    for i in range(depth):
        mixer = mixers[i]
        mi = mixers[:i].count(mixer)
        fi = is_moe[:i].count(is_moe[i])
        lng, lnb = ln_g[i], ln_b[i]
        ffn_mod = None if is_moe[i] else (i, 3, 4)
        if not ctx_needed_after[i] and mixer == 0 and n_rows != st.n_lat:
            n_rows = st.n_lat
            xs = xs[:n_rows]

        if mixer == 0:
            xs, hf = _pool_layer(st, n_rows, i, alpha, xs, mods, _cast(pool_w, mi), pool_scale[mi],
                                 lng[0], lnb[0])
        elif mixer == 1:
            qkv = _qkv_proj(st, h, _cast(attn_wqkv, mi), attn_q_gain[mi], attn_k_gain[mi])
            o = _attention(st, qkv, d)
            rows_out = n_rows if ctx_needed_after[i] else st.n_lat
            xs, hf = _mm_ln(st, rows_out, alpha, o, _cast(attn_wo, mi), xs, mods, i, 2, lng[0], lnb[0],
                            next_mod=ffn_mod)
            n_rows = rows_out
        else:
            yx = _lru_in_proj(h, _cast(lru_w_in, mi))
            fwd = _lru_scan(st, yx, lru_conv_w[mi], lru_conv_b[mi], lru_gate_w[mi, 0], lru_gate_b[mi, 0],
                            lru_lambda[mi, 0], False)
            m = _lru_scan(st, yx, lru_conv_w[mi], lru_conv_b[mi], lru_gate_w[mi, 1], lru_gate_b[mi, 1],
                          lru_lambda[mi, 1], True, fwd)
            rows_out = n_rows if ctx_needed_after[i] else st.n_lat
            xs, hf = _mm_ln(st, rows_out, alpha, m, _cast(lru_w_out, mi), xs, mods, i, 2, lng[0], lnb[0],
                            next_mod=ffn_mod)
            n_rows = rows_out

        next_mod = (i + 1, 0, 1) if i + 1 < depth and mixers[i + 1] != 0 else None
        if is_moe[i]:
            xs, h = _moe_layer(st, n_rows, alpha, xs, mods, i, moe_router[fi], moe_w_in, fi, moe_w_out,
                               lng[1], lnb[1], next_mod)
        else:
            w_in_p, w_out_p = _cast_pad_ffn(ffn_w_in, ffn_w_out, fi, 1024)
            act = _swiglu_up(hf, w_in_p, n_rows)
            xs, h = _mm_ln(st, n_rows, alpha, act, w_out_p, xs, mods, i, 5, lng[1], lnb[1], next_mod=next_mod)
    return xs[:st.n_lat].reshape(batch, seq, d)
```

```python
import functools

import jax
import jax.numpy as jnp
from jax import lax
from jax.experimental import pallas as pl
from jax.experimental.pallas import tpu as pltpu

F32 = jnp.float32
BF16 = jnp.bfloat16
HIGHEST = lax.Precision.HIGHEST

HEAD_DIM = 128
AXIS_DIM = HEAD_DIM // 2
GRID_W = 64
ROPE_THETA = 10000.0
LN_EPS = 1e-6
RMS_EPS = 1e-6
LRU_C = 8.0
LOG2E = 1.4426950408889634
CONV_W = 4
CONV_LEFT = CONV_W // 2
POOL_WINDOWS = (2, 4, 8, 16)
N_MOD = 6
N_MIXERS = 3
MOD_ROWS = 8

SUBLANES = 8
LANES = 128
VMEM_LIMIT_BYTES = 56 * 1024 * 1024
HALO = SUBLANES


def _params(*sem):
    return pltpu.CompilerParams(dimension_semantics=sem, vmem_limit_bytes=VMEM_LIMIT_BYTES)


def _pick(n, prefs):
    for p in prefs:
        if n % p == 0:
            return p
    return n


def _round_up(n, m):
    return (n + m - 1) // m * m


def _ln_rows(z, g, b):
    mu = jnp.mean(z, axis=-1, keepdims=True)
    zc = z - mu
    var = jnp.mean(zc * zc, axis=-1, keepdims=True)
    return zc * lax.rsqrt(var + LN_EPS) * g + b


class _Stream:
    def __init__(self, batch, seq, ctx_len, d):
        self.batch, self.seq, self.ctx_len, self.d = batch, seq, ctx_len, d
        self.n_lat = batch * seq
        self.n_all = self.n_lat + batch * ctx_len

    def row_tile(self, largest):
        t = largest
        while self.seq % t or (self.batch * self.ctx_len) % t:
            t //= 2
        return t

    def group(self, row0):
        return jnp.minimum(row0 // self.seq, self.batch)

    def mod_spec(self, layer, which, tm, row_offset=0):
        def imap(i, *_):
            return ((layer * MOD_ROWS + self.group(row_offset + i * tm)) * N_MOD + which, 0, 0)
        return pl.BlockSpec((1, 1, self.d), imap)

    def seq_pos(self, row0):
        is_ctx = row0 >= self.n_lat
        pos = jnp.where(is_ctx, (row0 - self.n_lat) % self.ctx_len, row0 % self.seq)
        return pos, jnp.where(is_ctx, self.ctx_len, self.seq)


def _vec_spec(d):
    return pl.BlockSpec((1, d), lambda *_: (0, 0))


def _halo_specs(tm, width, n_rows, col_block=0):
    per = tm // HALO
    last = n_rows // HALO - 1
    prev = pl.BlockSpec((HALO, width), lambda i, *_: (jnp.maximum(i * per - 1, 0), col_block))
    cur = pl.BlockSpec((tm, width), lambda i, *_: (i, col_block))
    nxt = pl.BlockSpec((HALO, width), lambda i, *_: (jnp.minimum((i + 1) * per, last), col_block))
    return prev, cur, nxt


def _adaln_kernel(c_ref, down_ref, up_ref, b_ref, o_ref, t_ref):
    @pl.when(pl.program_id(1) == 0)
    def _():
        c = c_ref[...]
        t_ref[...] = jnp.dot(c * jax.nn.sigmoid(c), down_ref[0], preferred_element_type=F32, precision=HIGHEST)

    o_ref[0] = jnp.dot(t_ref[...], up_ref[0], preferred_element_type=F32, precision=HIGHEST) + b_ref[0]


def _adaln(cvec, ada_down, ada_up, ada_b):
    depth, d, rank = ada_down.shape
    n = ada_up.shape[-1]
    tn = _pick(n, (4096, 2048, 1024, 512, 256, 128))
    out = pl.pallas_call(
        _adaln_kernel,
        out_shape=jax.ShapeDtypeStruct((depth, MOD_ROWS, n), F32),
        grid=(depth, n // tn),
        in_specs=[
            pl.BlockSpec((MOD_ROWS, d), lambda l, j: (0, 0)),
            pl.BlockSpec((1, d, rank), lambda l, j: (l, 0, 0)),
            pl.BlockSpec((1, rank, tn), lambda l, j: (l, 0, j)),
            pl.BlockSpec((1, 1, tn), lambda l, j: (l, 0, j)),
        ],
        out_specs=pl.BlockSpec((1, MOD_ROWS, tn), lambda l, j: (l, 0, j)),
        scratch_shapes=[pltpu.VMEM((MOD_ROWS, rank), F32)],
        compiler_params=_params("arbitrary", "arbitrary"),
        name="adaln",
    )(cvec, ada_down, ada_up, ada_b.reshape(depth, 1, n))
    return out.reshape(depth * MOD_ROWS * N_MOD, 1, d)


POOL_ROW_CHUNK = 128


def _pool_kernel(st, tm, alpha, row_offset, xp_ref, x_ref, xn_ref, sh_ref, sc_ref, gm_ref, shf_ref, scf_ref,
                 w_ref, ps_ref, lng_ref, lnb_ref, *refs):
    xo_ref, ho_ref, ext_ref, z_ref = refs[-4:]
    row0 = row_offset + pl.program_id(0) * tm
    pos0, seqlen = st.seq_pos(row0)
    first = pos0 == 0
    last = pos0 + tm == seqlen
    one_sc = 1.0 + sc_ref[0]
    sh = sh_ref[0]
    x = x_ref[...]
    ext_ref[pl.ds(HALO, tm), :] = x * one_sc + sh
    ext_ref[pl.ds(0, HALO), :] = jnp.where(first, 0.0, xp_ref[...] * one_sc + sh)
    ext_ref[pl.ds(HALO + tm, HALO), :] = jnp.where(last, 0.0, xn_ref[...] * one_sc + sh)

    pos = pos0 + lax.broadcasted_iota(jnp.int32, (tm, 1), 0)
    groups = len(POOL_WINDOWS)
    gw = st.d // groups
    for g, k in enumerate(POOL_WINDOWS):
        cols = pl.ds(g * gw, gw)
        lo = jnp.maximum(pos - k // 2, 0)
        hi = jnp.minimum(pos + (k - k // 2), seqlen)
        cnt = (hi - lo).astype(F32)
        n_ext = tm + 2 * HALO
        win = ext_ref[:, cols]
        win = win + pltpu.roll(win, 1, 0)
        n = 2
        while n < k:
            win = pltpu.roll(win, n_ext - n // 2, 0) + pltpu.roll(win, n // 2, 0)
            n *= 2
        win = win[HALO:HALO + tm]
        diff = (win / cnt - ext_ref[pl.ds(HALO, tm), cols]).astype(BF16)
        z_ref[:, cols] = jnp.dot(diff, w_ref[g], preferred_element_type=F32) * ps_ref[:, cols]

    rc = min(POOL_ROW_CHUNK, tm)

    def body(c, carry):
        rows = pl.ds(pl.multiple_of(c * rc, rc), rc)
        _deepnorm_epilogue(alpha, z_ref[rows, :], x_ref, gm_ref, lng_ref, lnb_ref, (shf_ref, scf_ref), xo_ref, ho_ref,
                           rows)
        return carry

    lax.fori_loop(0, tm // rc, body, 0)


def _pool_layer(st, n_out, layer, alpha, x, mods, w, pscale, lng, lnb, row_offset=0, filled=None):
    d = st.d
    tm = min(256, st.ctx_len)
    n_src = x.shape[0]
    prev, cur, nxt = _halo_specs(tm, d, n_src)
    groups, gw, _ = w.shape
    blk0 = row_offset // tm
    in_specs = [prev, cur, nxt] + [st.mod_spec(layer, q, tm, row_offset) for q in (0, 1, 2, 3, 4)] + [
        pl.BlockSpec((groups, gw, gw), lambda i: (0, 0, 0)), _vec_spec(d), _vec_spec(d), _vec_spec(d)]
    args = [x, x, x, mods, mods, mods, mods, mods, w, pscale.reshape(1, d), lng.reshape(1, d), lnb.reshape(1, d)]
    aliases = {}
    if filled is not None:
        aliases = {len(args): 0, len(args) + 1: 1}
        in_specs += [pl.BlockSpec(memory_space=pl.ANY)] * 2
        args += list(filled)
    return pl.pallas_call(
        functools.partial(_pool_kernel, st, tm, alpha, row_offset),
        out_shape=(jax.ShapeDtypeStruct((n_out, d), F32), jax.ShapeDtypeStruct((n_out, d), BF16)),
        grid=(n_src // tm,),
        in_specs=in_specs,
        out_specs=(pl.BlockSpec((tm, d), lambda i: (blk0 + i, 0)), pl.BlockSpec((tm, d), lambda i: (blk0 + i, 0))),
        scratch_shapes=[pltpu.VMEM((tm + 2 * HALO, d), F32), pltpu.VMEM((tm, d), F32)],
        input_output_aliases=aliases,
        compiler_params=_params("arbitrary"),
        name="pool_mixer",
    )(*args)


def _qkv_kernel(n_qk_tiles, a_ref, b_ref, gain_ref, c_ref, s1_ref, s2_ref, o_ref):
    j = pl.program_id(1)
    acc = jnp.dot(a_ref[...], b_ref[...], preferred_element_type=F32)
    tn = acc.shape[1]

    @pl.when(j < n_qk_tiles)
    def _():
        cos, s1, s2 = c_ref[...], s1_ref[...], s2_ref[...]
        for hh in range(tn // HEAD_DIM):
            cols = slice(hh * HEAD_DIM, (hh + 1) * HEAD_DIM)
            xh = acc[:, cols]
            n = xh * lax.rsqrt(jnp.mean(xh * xh, axis=-1, keepdims=True) + RMS_EPS) * gain_ref[:, cols]
            half = AXIS_DIM // 2
            rot = n * cos + pltpu.roll(n, HEAD_DIM - half, 1) * s1 + pltpu.roll(n, half, 1) * s2
            o_ref[:, cols] = rot.astype(o_ref.dtype)

    @pl.when(j >= n_qk_tiles)
    def _():
        o_ref[...] = acc.astype(o_ref.dtype)


def _rope_tables(seq, tm):
    rows = seq // GRID_W
    r = jnp.repeat(jnp.arange(rows, dtype=F32), GRID_W)
    col = jnp.tile(jnp.arange(GRID_W, dtype=F32), rows)
    inv = ROPE_THETA ** (-jnp.arange(0, AXIS_DIM, 2, dtype=F32) / AXIS_DIM)
    ang_r = r[:, None] * inv
    ang_c = col[:, None] * inv
    zero = jnp.zeros_like(ang_r)
    cos = jnp.concatenate([jnp.cos(ang_r), jnp.cos(ang_r), jnp.cos(ang_c), jnp.cos(ang_c)], axis=-1)
    s1 = jnp.concatenate([-jnp.sin(ang_r), zero, -jnp.sin(ang_c), zero], axis=-1)
    s2 = jnp.concatenate([zero, jnp.sin(ang_r), zero, jnp.sin(ang_c)], axis=-1)
    ident = jnp.ones((tm, HEAD_DIM), F32)
    nul = jnp.zeros((tm, HEAD_DIM), F32)
    return (jnp.concatenate([cos, ident]), jnp.concatenate([s1, nul]), jnp.concatenate([s2, nul]))


def _qkv_proj(st, h, w, q_gain, k_gain):
    n_rows, d = h.shape
    n = w.shape[1]
    kv = (n - d) // 2
    tm = st.row_tile(1024)
    tn = _pick(kv, (1024, 512, 256, 128))
    scale = HEAD_DIM ** -0.5 * LOG2E
    gain = jnp.concatenate([jnp.tile(q_gain * scale, d // HEAD_DIM), jnp.tile(k_gain, kv // HEAD_DIM),
                            jnp.ones((kv,), F32)]).reshape(1, n)
    cos, s1, s2 = _rope_tables(st.seq, tm)
    per_seq = st.seq // tm

    def tab_map(i, j):
        return (jnp.where(i * tm < st.n_lat, i % per_seq, per_seq), 0)

    tab = pl.BlockSpec((tm, HEAD_DIM), tab_map)
    return pl.pallas_call(
        functools.partial(_qkv_kernel, (d + kv) // tn),
        out_shape=jax.ShapeDtypeStruct((n_rows, n), BF16),
        grid=(n_rows // tm, n // tn),
        in_specs=[pl.BlockSpec((tm, d), lambda i, j: (i, 0)), pl.BlockSpec((d, tn), lambda i, j: (0, j)),
                  pl.BlockSpec((1, tn), lambda i, j: (0, j)), tab, tab, tab],
        out_specs=pl.BlockSpec((tm, tn), lambda i, j: (i, j)),
        compiler_params=_params("arbitrary", "arbitrary"),
        name="qkv_proj",
    )(h, w, gain, cos, s1, s2)


def _gelu_tanh(x):
    return 0.5 * x * (1.0 + jnp.tanh(0.7978845608028654 * (x + 0.044715 * (x * x * x))))


def _lru_in_kernel(n_gelu_tiles, a_ref, b_ref, o_ref):
    j = pl.program_id(1)
    acc = jnp.dot(a_ref[...], b_ref[...], preferred_element_type=F32)

    @pl.when(j < n_gelu_tiles)
    def _():
        o_ref[...] = _gelu_tanh(acc)

    @pl.when(j >= n_gelu_tiles)
    def _():
        o_ref[...] = acc


def _lru_in_proj(h, w):
    n_rows, d = h.shape
    n = w.shape[1]
    tm = _pick(n_rows, (1024, 512, 256, 128, 64))
    tn = _pick(n // 2, (1024, 512, 256, 128))
    return pl.pallas_call(
        functools.partial(_lru_in_kernel, (n // 2) // tn),
        out_shape=jax.ShapeDtypeStruct((n_rows, n), F32),
        grid=(n_rows // tm, n // tn),
        in_specs=[pl.BlockSpec((tm, d), lambda i, j: (i, 0)), pl.BlockSpec((d, tn), lambda i, j: (0, j))],
        out_specs=pl.BlockSpec((tm, tn), lambda i, j: (i, j)),
        compiler_params=_params("arbitrary", "arbitrary"),
        name="lru_in_proj",
    )(h, w)


def _swiglu_kernel(a_ref, wg_ref, wu_ref, o_ref):
    a = a_ref[...]
    gate = jnp.dot(a, wg_ref[0], preferred_element_type=F32)
    up = jnp.dot(a, wu_ref[0], preferred_element_type=F32)
    o_ref[...] = (gate * jax.nn.sigmoid(gate) * up).astype(o_ref.dtype)


def _swiglu_up(h, w, n_rows):
    e, d, f2 = w.shape
    f = f2 // 2
    tm = _pick(n_rows, (1024, 512, 256, 128, 64))
    tn = _pick(f, (512, 256, 128))
    nj = f // tn
    return pl.pallas_call(
        _swiglu_kernel,
        out_shape=jax.ShapeDtypeStruct((n_rows, e * f), BF16),
        grid=(n_rows // tm, e, nj),
        in_specs=[pl.BlockSpec((tm, d), lambda i, x, j: (i, 0)),
                  pl.BlockSpec((1, d, tn), lambda i, x, j: (x, 0, j)),
                  pl.BlockSpec((1, d, tn), lambda i, x, j: (x, 0, nj + j))],
        out_specs=pl.BlockSpec((tm, tn), lambda i, x, j: (i, x * nj + j)),
        compiler_params=_params("arbitrary", "arbitrary", "arbitrary"),
        name="swiglu_up",
    )(h, w, w)


def _deepnorm_epilogue(alpha, y, x_ref, gate_ref, lng_ref, lnb_ref, next_refs, xo_ref, ho_ref, rows=slice(None)):
    xn = _ln_rows(alpha * x_ref[rows, :] + gate_ref[0] * y, lng_ref[...], lnb_ref[...])
    xo_ref[rows, :] = xn
    if ho_ref is not None:
        shn_ref, scn_ref = next_refs
        ho_ref[rows, :] = (xn * (1.0 + scn_ref[0]) + shn_ref[0]).astype(BF16)


MM_LN_COL_CHUNK = 1024
MM_LN_ROW_CHUNK = 128
MM_LN_X_SLOTS = 4
MM_LN_OUT_SLOTS = 2


def _mm_ln_kernel(alpha, nk, has_next, *refs):
    a_ref, b_ref, gate_ref, lng_ref, lnb_ref = refs[:5]
    next_refs = refs[5:7] if has_next else None
    refs = refs[7:] if has_next else refs[5:]
    if has_next:
        x_hbm, xo_hbm, ho_hbm, acc_ref, xbuf, obuf, hbuf, xsem, osem, hsem = refs
    else:
        x_hbm, xo_hbm, acc_ref, xbuf, obuf, xsem, osem = refs
        ho_hbm = hbuf = hsem = None
    i, k = pl.program_id(0), pl.program_id(1)
    tm, d = acc_ref.shape
    x_slots, rc = xbuf.shape[:2]
    o_slots = obuf.shape[0]
    n_chunks = tm // rc

    def hbm_rows(ref, c):
        return ref.at[pl.ds(pl.multiple_of(i * tm + c * rc, rc), rc), :]

    def x_copy(c, slot):
        return pltpu.make_async_copy(hbm_rows(x_hbm, c), xbuf.at[slot], xsem.at[slot])

    def o_copy(c, slot):
        return pltpu.make_async_copy(obuf.at[slot], hbm_rows(xo_hbm, c), osem.at[slot])

    def h_copy(c, slot):
        return pltpu.make_async_copy(hbuf.at[slot], hbm_rows(ho_hbm, c), hsem.at[slot])

    def wait_out(c, slot):
        o_copy(c, slot).wait()
        if has_next:
            h_copy(c, slot).wait()

    @pl.when(k == 0)
    def _():
        acc_ref[...] = jnp.zeros_like(acc_ref)

    @pl.when(k == nk - 1)
    def _():
        for c in range(min(x_slots, n_chunks)):
            x_copy(c, c).start()

    a = a_ref[...]
    cw = min(MM_LN_COL_CHUNK, d)
    for c in range(d // cw):
        cols = slice(c * cw, (c + 1) * cw)
        acc_ref[:, cols] += jnp.dot(a, b_ref[:, cols], preferred_element_type=F32)

    @pl.when(k == nk - 1)
    def _():
        def body(c, carry):
            xs, os_ = c % x_slots, c % o_slots
            x_copy(c, xs).wait()

            @pl.when(c >= o_slots)
            def _():
                wait_out(c - o_slots, os_)

            y = acc_ref[pl.ds(pl.multiple_of(c * rc, rc), rc), :]
            _deepnorm_epilogue(alpha, y, xbuf.at[xs], gate_ref, lng_ref, lnb_ref, next_refs, obuf.at[os_],
                               hbuf.at[os_] if has_next else None)
            o_copy(c, os_).start()
            if has_next:
                h_copy(c, os_).start()

            @pl.when(c + x_slots < n_chunks)
            def _():
                x_copy(c + x_slots, xs).start()

            return carry

        lax.fori_loop(0, n_chunks, body, 0)
        for c in range(max(n_chunks - o_slots, 0), n_chunks):
            wait_out(c, c % o_slots)


def _mm_ln(st, n_rows, alpha, a, b, x, mods, layer, gate_idx, lng, lnb, next_mod=None):
    kdim, d = b.shape
    tm = st.row_tile(1024)
    tk = _pick(kdim, (1024, 512, 256, 128))
    nk = kdim // tk
    rc = min(MM_LN_ROW_CHUNK, tm)
    has_next = next_mod is not None
    any_spec = pl.BlockSpec(memory_space=pl.ANY)
    in_specs = [pl.BlockSpec((tm, tk), lambda i, k: (i, k)), pl.BlockSpec((tk, d), lambda i, k: (k, 0)),
                st.mod_spec(layer, gate_idx, tm), _vec_spec(d), _vec_spec(d)]
    args = [a, b, mods, lng.reshape(1, d), lnb.reshape(1, d)]
    out_shape = [jax.ShapeDtypeStruct((n_rows, d), F32)]
    scratch = [pltpu.VMEM((tm, d), F32), pltpu.VMEM((MM_LN_X_SLOTS, rc, d), F32),
               pltpu.VMEM((MM_LN_OUT_SLOTS, rc, d), F32)]
    sems = [pltpu.SemaphoreType.DMA((MM_LN_X_SLOTS,)), pltpu.SemaphoreType.DMA((MM_LN_OUT_SLOTS,))]
    if has_next:
        nl, n_sh, n_sc = next_mod
        in_specs += [st.mod_spec(nl, n_sh, tm), st.mod_spec(nl, n_sc, tm)]
        args += [mods, mods]
        out_shape.append(jax.ShapeDtypeStruct((n_rows, d), BF16))
        scratch.append(pltpu.VMEM((MM_LN_OUT_SLOTS, rc, d), BF16))
        sems.append(pltpu.SemaphoreType.DMA((MM_LN_OUT_SLOTS,)))
    res = pl.pallas_call(
        functools.partial(_mm_ln_kernel, alpha, nk, has_next),
        out_shape=tuple(out_shape),
        grid=(n_rows // tm, nk),
        in_specs=in_specs + [any_spec],
        out_specs=tuple(any_spec for _ in out_shape),
        scratch_shapes=scratch + sems,
        compiler_params=_params("arbitrary", "arbitrary"),
        name="matmul_deepnorm",
    )(*args, x)
    return res if has_next else (res[0], None)


ATTN_KEY_ALIGN = 256
ATTN_KEY_RANGES = 2


def _attn_kernel(n_lat_tiles, group, q_ref, kl_ref, vl_ref, kc_ref, vc_ref, o_ref, k_all, v_all):
    qi = pl.program_id(2)
    seq, ctx_len = kl_ref.shape[0], kc_ref.shape[0]

    @pl.when(qi == 0)
    def _():
        k_all[pl.ds(0, seq), :] = kl_ref[...]
        k_all[pl.ds(seq, ctx_len), :] = kc_ref[...]
        v_all[pl.ds(0, seq), pl.ds(0, HEAD_DIM)] = vl_ref[...]
        v_all[pl.ds(seq, ctx_len), pl.ds(0, HEAD_DIM)] = vc_ref[...]
        v_all[:, pl.ds(HEAD_DIM, HEAD_DIM)] = jnp.ones((seq + ctx_len, HEAD_DIM), v_all.dtype)

    def partial_softmax(q, rows):
        s = lax.dot_general(q, k_all[rows, :], (((1,), (1,)), ((), ())), preferred_element_type=F32)
        m = jnp.max(s, axis=-1, keepdims=True)
        return jnp.dot(jnp.exp2(s - m).astype(v_all.dtype), v_all[rows, :], preferred_element_type=F32), m

    def attend(key_ranges):
        for g in range(group):
            cols = slice(g * HEAD_DIM, (g + 1) * HEAD_DIM)
            q = q_ref[:, cols]
            parts = [partial_softmax(q, rows) for rows in key_ranges]
            o, m = parts[0]
            for o2, m2 in parts[1:]:
                m_new = jnp.maximum(m, m2)
                o = o * jnp.exp2(m - m_new) + o2 * jnp.exp2(m2 - m_new)
                m = m_new
            o_ref[:, cols] = (o[:, :HEAD_DIM] / o[:, HEAD_DIM:]).astype(o_ref.dtype)

    @pl.when(qi < n_lat_tiles)
    def _():
        n_keys = seq + ctx_len
        step = _round_up(-(-n_keys // ATTN_KEY_RANGES), ATTN_KEY_ALIGN)
        attend([pl.ds(lo, min(step, n_keys - lo)) for lo in range(0, n_keys, step)])

    @pl.when(qi >= n_lat_tiles)
    def _():
        attend([pl.ds(seq, ctx_len)])


def _attention(st, qkv, d):
    n_rows, n = qkv.shape
    kv = (n - d) // 2
    n_kv = kv // HEAD_DIM
    group = d // kv
    tq = min(256, st.ctx_len)
    ctx_tiles = st.ctx_len // tq
    lat_tiles = st.seq // tq
    qw = group * HEAD_DIM
    k_col0 = d // HEAD_DIM
    v_col0 = (d + kv) // HEAD_DIM

    def q_map(b, h, qi):
        row = jnp.where(qi < lat_tiles, b * lat_tiles + qi, st.n_lat // tq + b * ctx_tiles + (qi - lat_tiles))
        return (row, h)

    ctx_blk0 = st.n_lat // st.ctx_len
    return pl.pallas_call(
        functools.partial(_attn_kernel, lat_tiles, group),
        out_shape=jax.ShapeDtypeStruct((n_rows, d), BF16),
        grid=(st.batch, n_kv, lat_tiles + ctx_tiles),
        in_specs=[pl.BlockSpec((tq, qw), q_map),
                  pl.BlockSpec((st.seq, HEAD_DIM), lambda b, h, qi: (b, k_col0 + h)),
                  pl.BlockSpec((st.seq, HEAD_DIM), lambda b, h, qi: (b, v_col0 + h)),
                  pl.BlockSpec((st.ctx_len, HEAD_DIM), lambda b, h, qi: (ctx_blk0 + b, k_col0 + h)),
                  pl.BlockSpec((st.ctx_len, HEAD_DIM), lambda b, h, qi: (ctx_blk0 + b, v_col0 + h))],
        out_specs=pl.BlockSpec((tq, qw), q_map),
        scratch_shapes=[pltpu.VMEM((st.seq + st.ctx_len, HEAD_DIM), qkv.dtype),
                        pltpu.VMEM((st.seq + st.ctx_len, 2 * HEAD_DIM), qkv.dtype)],
        compiler_params=_params("arbitrary", "arbitrary", "arbitrary"),
        name="gqa_attention",
    )(qkv, qkv, qkv, qkv, qkv)


def _sigmoid(x):
    return 0.5 * jnp.tanh(0.5 * x) + 0.5


def _scan8(a, b, reverse):
    row = lax.broadcasted_iota(jnp.int32, a.shape, 0)
    for s in (1, 2, 4):
        if reverse:
            keep = row < SUBLANES - s
            shift = SUBLANES - s
        else:
            keep = row >= s
            shift = s
        a_sh = jnp.where(keep, pltpu.roll(a, shift, 0), 1.0)
        b_sh = jnp.where(keep, pltpu.roll(b, shift, 0), 0.0)
        b = a * b_sh + b
        a = a * a_sh
    return a, b


def _lru_kernel(tm, lat_tiles, reverse, *refs):
    if reverse:
        xr_ref, gw_ref, gb_ref, lam_ref, recf_ref, gelu_ref, o_ref, a_ref, b_ref, carry_ref = refs
    else:
        (xp_ref, x_ref, xn_ref, cw_ref, cb_ref, gw_ref, gb_ref, lam_ref,
         o_ref, xr_ref, ext_ref, a_ref, b_ref, carry_ref) = refs
    s = pl.program_id(1)
    is_ctx = s == 0

    @pl.when(is_ctx)
    def _():
        carry_ref[...] = jnp.zeros_like(carry_ref)

    if not reverse:
        tile = s - 1
        first = jnp.logical_or(is_ctx, tile == 0)
        last = jnp.logical_or(is_ctx, tile == lat_tiles - 1)
        ext_ref[pl.ds(HALO, tm), :] = x_ref[...]
        ext_ref[pl.ds(0, HALO), :] = jnp.where(first, 0.0, xp_ref[...])
        ext_ref[pl.ds(HALO + tm, HALO), :] = jnp.where(last, 0.0, xn_ref[...])

    heads, bw, _ = gw_ref.shape
    lam = lam_ref[...]
    softplus = jnp.maximum(-lam, 0.0) + jnp.log1p(jnp.exp(-jnp.abs(lam)))
    for h in range(heads):
        cols = pl.ds(h * bw, bw)
        if reverse:
            xr = xr_ref[:, cols]
        else:
            xr = cb_ref[:, cols] + ext_ref[pl.ds(HALO - CONV_LEFT, tm), cols] * cw_ref[0:1, cols]
            for k in range(1, CONV_W):
                xr = xr + ext_ref[pl.ds(HALO - CONV_LEFT + k, tm), cols] * cw_ref[k:k + 1, cols]
            xr_ref[:, cols] = xr
        g = jnp.dot(xr.astype(BF16), gw_ref[h], preferred_element_type=F32)
        r = _sigmoid(g[:, :bw] + gb_ref[0:1, cols])
        i = _sigmoid(g[:, bw:] + gb_ref[1:2, cols])
        log_a = -LRU_C * r * softplus[:, h * bw:(h + 1) * bw]
        a = jnp.exp(log_a)
        a_ref[:, cols] = a
        b_ref[:, cols] = jnp.sqrt(1.0 - a * a) * (i * xr)

    n_chunks = tm // SUBLANES

    def body(c, carry):
        c = n_chunks - 1 - c if reverse else c
        rows = pl.ds(pl.multiple_of(c * SUBLANES, SUBLANES), SUBLANES)
        a_cum, b_cum = _scan8(a_ref[rows, :], b_ref[rows, :], reverse)
        hs = a_cum * carry + b_cum
        if reverse:
            o_ref[rows, :] = (gelu_ref[rows, :] * (recf_ref[rows, :] + hs)).astype(o_ref.dtype)
            edge = hs[0:1, :]
        else:
            o_ref[rows, :] = hs
            edge = hs[SUBLANES - 1:SUBLANES, :]
        return jnp.broadcast_to(edge, carry.shape)

    carry_ref[...] = lax.fori_loop(0, n_chunks, body, carry_ref[...])


def _lru_scan(st, yx, conv_w, conv_b, gate_w, gate_b, lam, reverse, fwd=None):
    n_rows = yx.shape[0]
    d = st.d
    tm = min(256, st.ctx_len)
    assert st.ctx_len == tm
    lat_tiles = st.seq // tm
    ctx_blk0 = st.n_lat // tm
    per = tm // HALO
    last_halo = n_rows // HALO - 1

    def blk(b, s):
        lat = b * lat_tiles + (lat_tiles - s if reverse else s - 1)
        return jnp.where(s == 0, ctx_blk0 + b, lat)

    heads = gate_w.shape[1]
    bw = d // heads
    gw = jnp.concatenate([gate_w[0], gate_w[1]], axis=-1).astype(BF16)
    tile = pl.BlockSpec((tm, d), lambda b, s: (blk(b, s), 0))
    gate_specs = [pl.BlockSpec((heads, bw, 2 * bw), lambda b, s: (0, 0, 0)),
                  pl.BlockSpec((2, d), lambda b, s: (0, 0)), _vec_spec(d)]
    gate_args = [gw, gate_b.reshape(2, d), lam.reshape(1, d)]
    scratch = [pltpu.VMEM((tm, d), F32), pltpu.VMEM((tm, d), F32), pltpu.VMEM((SUBLANES, d), F32)]
    if reverse:
        rec_f, xr = fwd
        in_specs = [tile] + gate_specs + [tile, tile]
        args = [xr] + gate_args + [rec_f, yx]
        out_shape = jax.ShapeDtypeStruct((n_rows, d), BF16)
        out_specs = tile
    else:
        in_specs = [pl.BlockSpec((HALO, d), lambda b, s: (jnp.maximum(blk(b, s) * per - 1, 0), 1)),
                    pl.BlockSpec((tm, d), lambda b, s: (blk(b, s), 1)),
                    pl.BlockSpec((HALO, d), lambda b, s: (jnp.minimum((blk(b, s) + 1) * per, last_halo), 1)),
                    pl.BlockSpec((CONV_W, d), lambda b, s: (0, 0)), _vec_spec(d)] + gate_specs
        args = [yx, yx, yx, conv_w, conv_b.reshape(1, d)] + gate_args
        out_shape = (jax.ShapeDtypeStruct((n_rows, d), F32), jax.ShapeDtypeStruct((n_rows, d), F32))
        out_specs = (tile, tile)
        scratch = [pltpu.VMEM((tm + 2 * HALO, d), F32)] + scratch
    return pl.pallas_call(
        functools.partial(_lru_kernel, tm, lat_tiles, reverse),
        out_shape=out_shape,
        grid=(st.batch, lat_tiles + 1),
        in_specs=in_specs,
        out_specs=out_specs,
        scratch_shapes=scratch,
        compiler_params=_params("arbitrary", "arbitrary"),
        name="rglru_reverse" if reverse else "rglru_forward",
    )(*args)


ROUTE_LANES = 8
MOE_ROW_TILE = 512


def _split_hi_lo(x):
    hi = lax.bitcast_convert_type(lax.bitcast_convert_type(x, jnp.uint32) & jnp.uint32(HI16), F32)
    return hi.astype(jnp.bfloat16), (x - hi).astype(jnp.bfloat16)


def _lane_pick(rec, k):
    lane = lax.broadcasted_iota(jnp.int32, rec.shape, 1)
    return jnp.sum(jnp.where(lane == k, rec, 0.0), axis=1, keepdims=True)


def _router_kernel(x_ref, sh_ref, sc_ref, w_ref, route_ref, cnt_ref, run_ref):
    @pl.when(pl.program_id(0) == 0)
    def _():
        run_ref[...] = jnp.zeros_like(run_ref)

    h = x_ref[...] * (1.0 + sc_ref[0]) + sh_ref[0]
    n_exp = w_ref.shape[1] // 2
    h_hi, h_lo = _split_hi_lo(h)
    t = jnp.dot(h_hi, w_ref[...], preferred_element_type=F32)
    u = jnp.dot(h_lo, w_ref[:, :n_exp], preferred_element_type=F32)
    logits = (t[:, n_exp:] + u) + t[:, :n_exp]
    tm = logits.shape[0]
    lane = lax.broadcasted_iota(jnp.int32, logits.shape, 1).astype(F32)
    m1 = jnp.max(logits, axis=-1, keepdims=True)
    i1 = jnp.min(jnp.where(logits == m1, lane, float(n_exp)), axis=-1, keepdims=True)
    pick1 = lane == i1
    rest = jnp.where(pick1, -jnp.inf, logits)
    m2 = jnp.max(rest, axis=-1, keepdims=True)
    i2 = jnp.min(jnp.where(rest == m2, lane, float(n_exp)), axis=-1, keepdims=True)
    pick2 = lane == i2
    e2 = jnp.exp(m2 - m1)
    den = 1.0 + e2
    onehot = jnp.where(pick1, 1.0, jnp.where(pick2, 1.0, 0.0))
    earlier = (lax.broadcasted_iota(jnp.int32, (tm, tm), 0) > lax.broadcasted_iota(jnp.int32, (tm, tm), 1))
    before = jnp.dot(jnp.where(earlier, 1.0, 0.0).astype(BF16), onehot.astype(BF16),
                     preferred_element_type=F32) + run_ref[...]
    rank1 = jnp.sum(jnp.where(pick1, before, 0.0), axis=-1, keepdims=True)
    rank2 = jnp.sum(jnp.where(pick2, before, 0.0), axis=-1, keepdims=True)
    rec = jnp.zeros((tm, ROUTE_LANES), F32)
    for k, v in enumerate((i1, i2, 1.0 / den, e2 / den, rank1, rank2)):
        rec = jnp.where(lax.broadcasted_iota(jnp.int32, rec.shape, 1) == k, v, rec)
    route_ref[...] = rec
    run_ref[...] += jnp.sum(onehot, axis=0, keepdims=True)
    cnt_ref[...] = run_ref[...]


def _router(st, n_rows, x, mods, layer, w):
    d, n_exp = w.shape
    tm = min(256, st.ctx_len)
    w_split = jnp.concatenate(_split_hi_lo(w), axis=1)
    return pl.pallas_call(
        _router_kernel,
        out_shape=(jax.ShapeDtypeStruct((n_rows, ROUTE_LANES), F32), jax.ShapeDtypeStruct((1, n_exp), F32)),
        grid=(n_rows // tm,),
        in_specs=[pl.BlockSpec((tm, d), lambda i: (i, 0)), st.mod_spec(layer, 3, tm), st.mod_spec(layer, 4, tm),
                  pl.BlockSpec((d, 2 * n_exp), lambda i: (0, 0))],
        out_specs=(pl.BlockSpec((tm, ROUTE_LANES), lambda i: (i, 0)), pl.BlockSpec((1, n_exp), lambda i: (0, 0))),
        scratch_shapes=[pltpu.VMEM((1, n_exp), F32)],
        compiler_params=_params("arbitrary"),
        name="moe_router",
    )(x, mods, mods, w_split)


def _dispatch_plan(route, counts, tm):
    n_tok = route.shape[0]
    n_exp = counts.shape[1]
    expert = route[:, 0:2].astype(jnp.int32)
    rank = route[:, 4:6].astype(jnp.int32)
    cnt = counts[0].astype(jnp.int32)
    padded = (cnt + tm - 1) // tm * tm
    ends = jnp.cumsum(padded)
    starts = ends - padded
    dest = (starts[expert] + rank).T.reshape(2 * n_tok)
    n_tiles = (2 * n_tok + n_exp * tm) // tm
    tile_row0 = jnp.arange(n_tiles, dtype=jnp.int32) * tm
    tile_expert = jnp.minimum(jnp.sum(tile_row0[:, None] >= ends[None, :], axis=1), n_exp - 1).astype(jnp.int32)
    n_used = (ends[-1:] // tm).astype(jnp.int32)
    return dest, tile_expert, n_used, n_tiles


HI16 = 0xFFFF0000
ROW_DMA_UNROLL = 8


def _pack_bf16_pairs(lo, hi):
    lo = lax.bitcast_convert_type(lo.astype(jnp.bfloat16).astype(F32), jnp.uint32)
    hi = lax.bitcast_convert_type(hi.astype(jnp.bfloat16).astype(F32), jnp.uint32)
    return (lo >> 16) | (hi & jnp.uint32(HI16))


def _unpack_bf16_pairs(p):
    return lax.bitcast_convert_type(p << 16, F32), lax.bitcast_convert_type(p & jnp.uint32(HI16), F32)


def _dispatch_kernel(tm, n_tok, dest_ref, x_ref, sh_ref, sc_ref, init_ref, o_ref, h_ref, sem):
    del init_ref
    base = pl.program_id(0) * tm
    h = x_ref[...] * (1.0 + sc_ref[0]) + sh_ref[0]
    half = h.shape[1] // 2
    h_ref[...] = _pack_bf16_pairs(h[:, :half], h[:, half:])

    def row_copy(r, d):
        return pltpu.make_async_copy(h_ref.at[pl.ds(r, 1), :], o_ref.at[pl.ds(d, 1), :], sem)

    def issue(r, carry):
        row_copy(r, dest_ref[base + r]).start()
        row_copy(r, dest_ref[n_tok + base + r]).start()
        return carry

    def drain(r, carry):
        row_copy(r, 0).wait()
        row_copy(r, 0).wait()
        return carry

    lax.fori_loop(0, tm, issue, 0, unroll=ROW_DMA_UNROLL)
    lax.fori_loop(0, tm, drain, 0, unroll=ROW_DMA_UNROLL)


def _dispatch(st, n_rows, x, mods, layer, dest, n_out):
    d = st.d
    tm = min(256, st.ctx_len)
    grid_spec = pltpu.PrefetchScalarGridSpec(
        num_scalar_prefetch=1,
        grid=(n_rows // tm,),
        in_specs=[pl.BlockSpec((tm, d), lambda i, dest: (i, 0)), st.mod_spec(layer, 3, tm), st.mod_spec(layer, 4, tm),
                  pl.BlockSpec(memory_space=pl.ANY)],
        out_specs=pl.BlockSpec(memory_space=pl.ANY),
        scratch_shapes=[pltpu.VMEM((tm, d // 2), jnp.uint32), pltpu.SemaphoreType.DMA],
    )
    return pl.pallas_call(
        functools.partial(_dispatch_kernel, tm, n_rows),
        out_shape=jax.ShapeDtypeStruct((n_out, d // 2), jnp.uint32),
        grid_spec=grid_spec,
        input_output_aliases={4: 0},
        compiler_params=_params("arbitrary"),
        name="moe_dispatch",
    )(dest, x, mods, mods, jnp.zeros((n_out, d // 2), jnp.uint32))


def _moe_up_kernel(te_ref, nu_ref, a_ref, wg_ref, wu_ref, o_ref, wbf_ref):
    i = pl.program_id(1)

    @pl.when(jnp.logical_or(i == 0, te_ref[i] != te_ref[jnp.maximum(i - 1, 0)]))
    def _():
        wbf_ref[0] = wg_ref[...].astype(wbf_ref.dtype)
        wbf_ref[1] = wu_ref[...].astype(wbf_ref.dtype)

    @pl.when(i < nu_ref[0])
    def _():
        lo, hi = (t.astype(jnp.bfloat16) for t in _unpack_bf16_pairs(a_ref[...]))
        half = lo.shape[1]

        def proj(which):
            return (jnp.dot(lo, wbf_ref[which, :half, :], preferred_element_type=F32)
                    + jnp.dot(hi, wbf_ref[which, half:, :], preferred_element_type=F32))

        gate, up = proj(0), proj(1)
        o_ref[...] = (gate * jax.nn.sigmoid(gate) * up).astype(o_ref.dtype)

    @pl.when(i >= nu_ref[0])
    def _():
        o_ref[...] = jnp.zeros_like(o_ref)


def _moe_up(xg, w, layer, tile_expert, n_used, tm):
    n_out = xg.shape[0]
    d, f2 = w.shape[2:]
    f = f2 // 2
    tn = _pick(f, (512, 256, 128))
    nj = f // tn
    grid_spec = pltpu.PrefetchScalarGridSpec(
        num_scalar_prefetch=2,
        grid=(nj, n_out // tm),
        in_specs=[pl.BlockSpec((tm, d // 2), lambda j, i, te, nu: (i, 0)),
                  pl.BlockSpec((None, None, d, tn), lambda j, i, te, nu: (layer, te[i], 0, j)),
                  pl.BlockSpec((None, None, d, tn), lambda j, i, te, nu: (layer, te[i], 0, nj + j))],
        out_specs=pl.BlockSpec((tm, tn), lambda j, i, te, nu: (i, j)),
        scratch_shapes=[pltpu.VMEM((2, d, tn), BF16)],
    )
    return pl.pallas_call(
        _moe_up_kernel,
        out_shape=jax.ShapeDtypeStruct((n_out, f), BF16),
        grid_spec=grid_spec,
        compiler_params=_params("arbitrary", "arbitrary"),
        name="moe_up",
    )(tile_expert, n_used, xg, w, w)


def _moe_down_kernel(te_ref, nu_ref, a_ref, w_ref, o_ref, wbf_ref):
    i = pl.program_id(1)

    @pl.when(jnp.logical_or(i == 0, te_ref[i] != te_ref[jnp.maximum(i - 1, 0)]))
    def _():
        wbf_ref[...] = w_ref[...].astype(wbf_ref.dtype)

    @pl.when(i < nu_ref[0])
    def _():
        a = a_ref[...]
        half = wbf_ref.shape[1] // 2
        o_ref[...] = _pack_bf16_pairs(jnp.dot(a, wbf_ref[:, :half], preferred_element_type=F32),
                                      jnp.dot(a, wbf_ref[:, half:], preferred_element_type=F32))

    @pl.when(i >= nu_ref[0])
    def _():
        o_ref[...] = jnp.zeros_like(o_ref)


MOE_DOWN_COLS = 2048


def _moe_down(act, w, layer, tile_expert, n_used, tm):
    n_out, f = act.shape
    d = w.shape[3]
    tn = min(MOE_DOWN_COLS, d)
    grid_spec = pltpu.PrefetchScalarGridSpec(
        num_scalar_prefetch=2,
        grid=(d // tn, n_out // tm),
        in_specs=[pl.BlockSpec((tm, f), lambda j, i, te, nu: (i, 0)),
                  pl.BlockSpec((None, None, f, tn), lambda j, i, te, nu: (layer, te[i], 0, j))],
        out_specs=pl.BlockSpec((tm, tn // 2), lambda j, i, te, nu: (i, j)),
        scratch_shapes=[pltpu.VMEM((f, tn), BF16)],
    )
    return pl.pallas_call(
        _moe_down_kernel,
        out_shape=jax.ShapeDtypeStruct((n_out, d // 2), jnp.uint32),
        grid_spec=grid_spec,
        compiler_params=_params("arbitrary", "arbitrary"),
        name="moe_down",
    )(tile_expert, n_used, act, w)


def _unpack_expert_rows(p, d):
    tn = min(MOE_DOWN_COLS, d)
    lo, hi = _unpack_bf16_pairs(p)
    half = tn // 2
    parts = []
    for j in range(d // tn):
        parts += [lo[:, j * half:(j + 1) * half], hi[:, j * half:(j + 1) * half]]
    return parts


COMBINE_ROW_CHUNK = 128


def _combine_kernel(tm, n_tok, alpha, has_next, dest_ref, *refs):
    route_ref, x_ref, gate_ref, lng_ref, lnb_ref = refs[:5]
    next_refs = refs[5:7] if has_next else None
    refs = refs[7:] if has_next else refs[5:]
    y_ref, xo_ref = refs[:2]
    ho_ref = refs[2] if has_next else None
    buf_ref, sem = refs[-2:]
    i = pl.program_id(0)
    n_steps = pl.num_programs(0)
    slot = i % 2

    def row_copy(slot, which, r, src):
        return pltpu.make_async_copy(y_ref.at[pl.ds(src, 1), :], buf_ref.at[slot, which, pl.ds(r, 1), :],
                                     sem.at[slot])

    def gather(tile, slot):
        def issue(r, carry):
            row_copy(slot, 0, r, dest_ref[tile * tm + r]).start()
            row_copy(slot, 1, r, dest_ref[n_tok + tile * tm + r]).start()
            return carry
        lax.fori_loop(0, tm, issue, 0, unroll=ROW_DMA_UNROLL)

    @pl.when(i == 0)
    def _():
        gather(0, 0)

    @pl.when(i + 1 < n_steps)
    def _():
        gather(i + 1, 1 - slot)

    def drain(r, carry):
        row_copy(slot, 0, r, 0).wait()
        row_copy(slot, 1, r, 0).wait()
        return carry

    lax.fori_loop(0, tm, drain, 0, unroll=ROW_DMA_UNROLL)
    rc = min(COMBINE_ROW_CHUNK, tm)

    def body(c, carry):
        rows = pl.ds(pl.multiple_of(c * rc, rc), rc)
        rec = route_ref[rows, :]
        g1, g2 = _lane_pick(rec, 2), _lane_pick(rec, 3)
        d = x_ref.shape[1]
        y1 = _unpack_expert_rows(buf_ref[slot, 0, rows, :], d)
        y2 = _unpack_expert_rows(buf_ref[slot, 1, rows, :], d)
        y = jnp.concatenate([g1 * p1 + g2 * p2 for p1, p2 in zip(y1, y2)], axis=1)
        _deepnorm_epilogue(alpha, y, x_ref, gate_ref, lng_ref, lnb_ref, next_refs, xo_ref, ho_ref, rows)
        return carry

    lax.fori_loop(0, tm // rc, body, 0)


def _combine(st, n_rows, alpha, yg, route, dest, x, mods, layer, lng, lnb, next_mod=None):
    d = st.d
    tm = min(256, st.ctx_len)
    has_next = next_mod is not None
    row = lambda i, dest: (i, 0)
    in_specs = [pl.BlockSpec((tm, ROUTE_LANES), row), pl.BlockSpec((tm, d), row), st.mod_spec(layer, 5, tm),
                _vec_spec(d), _vec_spec(d)]
    args = [route, x, mods, lng.reshape(1, d), lnb.reshape(1, d)]
    out_shape = [jax.ShapeDtypeStruct((n_rows, d), F32)]
    out_specs = [pl.BlockSpec((tm, d), row)]
    if has_next:
        nl, n_sh, n_sc = next_mod
        in_specs += [st.mod_spec(nl, n_sh, tm), st.mod_spec(nl, n_sc, tm)]
        args += [mods, mods]
        out_shape.append(jax.ShapeDtypeStruct((n_rows, d), BF16))
        out_specs.append(pl.BlockSpec((tm, d), row))
    in_specs.append(pl.BlockSpec(memory_space=pl.ANY))
    args.append(yg)
    grid_spec = pltpu.PrefetchScalarGridSpec(
        num_scalar_prefetch=1,
        grid=(n_rows // tm,),
        in_specs=in_specs,
        out_specs=tuple(out_specs),
        scratch_shapes=[pltpu.VMEM((2, 2, tm, d // 2), jnp.uint32), pltpu.SemaphoreType.DMA((2,))],
    )
    res = pl.pallas_call(
        functools.partial(_combine_kernel, tm, n_rows, alpha, has_next),
        out_shape=tuple(out_shape),
        grid_spec=grid_spec,
        compiler_params=_params("arbitrary"),
        name="moe_combine",
    )(dest, *args)
    return res if has_next else (res[0], None)


def _moe_layer(st, n_rows, alpha, x, mods, layer, router_w, w_in, moe_index, w_out, lng, lnb, next_mod):
    route, counts = _router(st, n_rows, x, mods, layer, router_w)
    dest, tile_expert, n_used, n_tiles = _dispatch_plan(route, counts, MOE_ROW_TILE)
    xg = _dispatch(st, n_rows, x, mods, layer, dest, n_tiles * MOE_ROW_TILE)
    act = _moe_up(xg, w_in, moe_index, tile_expert, n_used, MOE_ROW_TILE)
    yg = _moe_down(act, w_out, moe_index, tile_expert, n_used, MOE_ROW_TILE)
    return _combine(st, n_rows, alpha, yg, route, dest, x, mods, layer, lng, lnb, next_mod)


def _cast_kernel(is_pad, x_ref, o_ref):
    pad = is_pad(pl.program_id(0), pl.program_id(1))

    @pl.when(jnp.logical_not(pad))
    def _():
        o_ref[...] = x_ref[...].astype(o_ref.dtype)

    @pl.when(pad)
    def _():
        o_ref[...] = jnp.zeros_like(o_ref)


def _cast_blocks(w, layer, block, out_blocks, src_of, n_src):
    def in_map(i, j):
        bi, bj = src_of(i, j)
        return (layer, jnp.minimum(bi, n_src[0] - 1), jnp.minimum(bj, n_src[1] - 1))

    def is_pad(i, j):
        bi, bj = src_of(i, j)
        return jnp.logical_or(bi >= n_src[0], bj >= n_src[1])

    return pl.pallas_call(
        functools.partial(_cast_kernel, is_pad),
        out_shape=jax.ShapeDtypeStruct((out_blocks[0] * block[0], out_blocks[1] * block[1]), BF16),
        grid=out_blocks,
        in_specs=[pl.BlockSpec((None,) + block, in_map)],
        out_specs=pl.BlockSpec(block, lambda i, j: (i, j)),
        compiler_params=_params("arbitrary", "arbitrary"),
        name="cast_weights",
    )(w)


def _cast(w, layer):
    w3 = w.reshape(w.shape[0], -1, w.shape[-1])
    rows, cols = w3.shape[1:]
    block = (_pick(rows, (512, 256, 128, 64, 32, 16)), _pick(cols, (4096, 2048, 1024, 512, 256, 128)))
    n = (rows // block[0], cols // block[1])
    return _cast_blocks(w3, layer, block, n, lambda i, j: (i, j), n).reshape(w.shape[1:])


FFN_PAD_BLOCK = 256


def _cast_pad_ffn(w_in, w_out, layer, mult):
    d, f2 = w_in.shape[1:]
    f = f2 // 2
    blk = _pick(f, (FFN_PAD_BLOCK, 128))
    fp = _round_up(f, mult)
    nb, nbp = f // blk, fp // blk
    tr = _pick(d, (2048, 1024, 512, 256, 128))

    def in_src(i, j):
        up = j - nbp
        return (i, jnp.where(j < nbp, jnp.where(j < nb, j, 2 * nb), jnp.where(up < nb, up + nb, 2 * nb)))

    w_in_p = _cast_blocks(w_in, layer, (tr, blk), (d // tr, 2 * nbp), in_src, (d // tr, 2 * nb))
    dc = _pick(d, (4096, 2048, 1024, 512, 256, 128))
    w_out_p = _cast_blocks(w_out, layer, (blk, dc), (nbp, d // dc), lambda i, j: (i, j), (nb, d // dc))
    return w_in_p.reshape(1, d, 2 * fp), w_out_p


def kernel(x, c, ctx, c_ctx, ada_down, ada_up, ada_b, ln_g, ln_b, pool_w, pool_scale, attn_wqkv, attn_q_gain,
           attn_k_gain, attn_wo, lru_w_in, lru_conv_w, lru_conv_b, lru_gate_w, lru_gate_b, lru_lambda, lru_w_out,
           ffn_w_in, ffn_w_out, moe_router, moe_w_in, moe_w_out):
    batch, seq, d = x.shape
    ctx_len = ctx.shape[1]
    depth = ada_down.shape[0]
    assert batch < MOD_ROWS and seq % ctx_len == 0 and ctx_len % SUBLANES == 0
    st = _Stream(batch, seq, ctx_len, d)
    alpha = (2.0 * depth) ** 0.25
    mixers = [i % N_MIXERS for i in range(depth)]
    is_moe = [i % 2 == 1 for i in range(depth)]
    ctx_needed_after = [any(mixers[j] != 0 for j in range(i + 1, depth)) for i in range(depth)]

    cvec = jnp.concatenate([c, c_ctx[None], jnp.zeros((MOD_ROWS - batch - 1, d), F32)], axis=0)
    mods = _adaln(cvec, ada_down, ada_up, ada_b)

    xs = None
    n_rows = st.n_all
    h = None
    for i in range(depth):
        mixer = mixers[i]
        mi = mixers[:i].count(mixer)
        fi = is_moe[:i].count(is_moe[i])
        lng, lnb = ln_g[i], ln_b[i]
        ffn_mod = None if is_moe[i] else (i, 3, 4)
        if not ctx_needed_after[i] and mixer == 0 and n_rows != st.n_lat:
            n_rows = st.n_lat
            xs = None if xs is None else xs[:n_rows]

        if mixer == 0 and xs is None:
            pool = functools.partial(_pool_layer, st, n_rows, i, alpha, mods=mods, w=_cast(pool_w, mi),
                                     pscale=pool_scale[mi], lng=lng[0], lnb=lnb[0])
            xs, hf = pool(x=x.reshape(st.n_lat, d))
            if n_rows > st.n_lat:
                xs, hf = pool(x=ctx.reshape(batch * ctx_len, d), row_offset=st.n_lat, filled=(xs, hf))
        elif mixer == 0:
            xs, hf = _pool_layer(st, n_rows, i, alpha, xs, mods, _cast(pool_w, mi), pool_scale[mi],
                                 lng[0], lnb[0])
        elif mixer == 1:
            qkv = _qkv_proj(st, h, _cast(attn_wqkv, mi), attn_q_gain[mi], attn_k_gain[mi])
            o = _attention(st, qkv, d)
            rows_out = n_rows if ctx_needed_after[i] else st.n_lat
            xs, hf = _mm_ln(st, rows_out, alpha, o, _cast(attn_wo, mi), xs, mods, i, 2, lng[0], lnb[0],
                            next_mod=ffn_mod)
            n_rows = rows_out
        else:
            yx = _lru_in_proj(h, _cast(lru_w_in, mi))
            fwd = _lru_scan(st, yx, lru_conv_w[mi], lru_conv_b[mi], lru_gate_w[mi, 0], lru_gate_b[mi, 0],
                            lru_lambda[mi, 0], False)
            m = _lru_scan(st, yx, lru_conv_w[mi], lru_conv_b[mi], lru_gate_w[mi, 1], lru_gate_b[mi, 1],
                          lru_lambda[mi, 1], True, fwd)
            rows_out = n_rows if ctx_needed_after[i] else st.n_lat
            xs, hf = _mm_ln(st, rows_out, alpha, m, _cast(lru_w_out, mi), xs, mods, i, 2, lng[0], lnb[0],
                            next_mod=ffn_mod)
            n_rows = rows_out

        next_mod = (i + 1, 0, 1) if i + 1 < depth and mixers[i + 1] != 0 else None
        if is_moe[i]:
            xs, h = _moe_layer(st, n_rows, alpha, xs, mods, i, moe_router[fi], moe_w_in, fi, moe_w_out,
                               lng[1], lnb[1], next_mod)
        else:
            w_in_p, w_out_p = _cast_pad_ffn(ffn_w_in, ffn_w_out, fi, 1024)
            act = _swiglu_up(hf, w_in_p, n_rows)
            xs, h = _mm_ln(st, n_rows, alpha, act, w_out_p, xs, mods, i, 5, lng[1], lnb[1], next_mod=next_mod)
    return xs[:st.n_lat].reshape(batch, seq, d)
```

```python
import functools

import jax
import jax.numpy as jnp
from jax import lax
from jax.experimental import pallas as pl
from jax.experimental.pallas import tpu as pltpu

F32 = jnp.float32
BF16 = jnp.bfloat16
HIGHEST = lax.Precision.HIGHEST

HEAD_DIM = 128
AXIS_DIM = HEAD_DIM // 2
GRID_W = 64
ROPE_THETA = 10000.0
LN_EPS = 1e-6
RMS_EPS = 1e-6
LRU_C = 8.0
LOG2E = 1.4426950408889634
CONV_W = 4
CONV_LEFT = CONV_W // 2
POOL_WINDOWS = (2, 4, 8, 16)
N_MOD = 6
N_MIXERS = 3
MOD_ROWS = 8

SUBLANES = 8
VMEM_LIMIT_BYTES = 56 * 1024 * 1024
HALO = SUBLANES


def _params(*sem):
    return pltpu.CompilerParams(dimension_semantics=sem, vmem_limit_bytes=VMEM_LIMIT_BYTES)


def _pick(n, prefs):
    for p in prefs:
        if n % p == 0:
            return p
    return n


def _round_up(n, m):
    return (n + m - 1) // m * m


def _ln_rows(z, g, b):
    mu = jnp.mean(z, axis=-1, keepdims=True)
    zc = z - mu
    var = jnp.mean(zc * zc, axis=-1, keepdims=True)
    return zc * lax.rsqrt(var + LN_EPS) * g + b


class _Stream:
    def __init__(self, batch, seq, ctx_len, d):
        self.batch, self.seq, self.ctx_len, self.d = batch, seq, ctx_len, d
        self.n_lat = batch * seq
        self.n_all = self.n_lat + batch * ctx_len

    def row_tile(self, largest):
        t = largest
        while self.seq % t or (self.batch * self.ctx_len) % t:
            t //= 2
        return t

    def group(self, row0):
        return jnp.minimum(row0 // self.seq, self.batch)

    def mod_spec(self, layer, which, tm, row_offset=0):
        def imap(i, *_):
            return ((layer * MOD_ROWS + self.group(row_offset + i * tm)) * N_MOD + which, 0, 0)
        return pl.BlockSpec((1, 1, self.d), imap)

    def seq_pos(self, row0):
        is_ctx = row0 >= self.n_lat
        pos = jnp.where(is_ctx, (row0 - self.n_lat) % self.ctx_len, row0 % self.seq)
        return pos, jnp.where(is_ctx, self.ctx_len, self.seq)


def _vec_spec(d):
    return pl.BlockSpec((1, d), lambda *_: (0, 0))


def _halo_specs(tm, width, n_rows, col_block=0):
    per = tm // HALO
    last = n_rows // HALO - 1
    prev = pl.BlockSpec((HALO, width), lambda i, *_: (jnp.maximum(i * per - 1, 0), col_block))
    cur = pl.BlockSpec((tm, width), lambda i, *_: (i, col_block))
    nxt = pl.BlockSpec((HALO, width), lambda i, *_: (jnp.minimum((i + 1) * per, last), col_block))
    return prev, cur, nxt


def _adaln_kernel(c_ref, down_ref, up_ref, b_ref, o_ref, t_ref):
    @pl.when(pl.program_id(1) == 0)
    def _():
        c = c_ref[...]
        t_ref[...] = jnp.dot(c * jax.nn.sigmoid(c), down_ref[0], preferred_element_type=F32, precision=HIGHEST)

    o_ref[0] = jnp.dot(t_ref[...], up_ref[0], preferred_element_type=F32, precision=HIGHEST) + b_ref[0]


def _adaln(cvec, ada_down, ada_up, ada_b):
    depth, d, rank = ada_down.shape
    n = ada_up.shape[-1]
    tn = _pick(n, (4096, 2048, 1024, 512, 256, 128))
    out = pl.pallas_call(
        _adaln_kernel,
        out_shape=jax.ShapeDtypeStruct((depth, MOD_ROWS, n), F32),
        grid=(depth, n // tn),
        in_specs=[
            pl.BlockSpec((MOD_ROWS, d), lambda l, j: (0, 0)),
            pl.BlockSpec((1, d, rank), lambda l, j: (l, 0, 0)),
            pl.BlockSpec((1, rank, tn), lambda l, j: (l, 0, j)),
            pl.BlockSpec((1, 1, tn), lambda l, j: (l, 0, j)),
        ],
        out_specs=pl.BlockSpec((1, MOD_ROWS, tn), lambda l, j: (l, 0, j)),
        scratch_shapes=[pltpu.VMEM((MOD_ROWS, rank), F32)],
        compiler_params=_params("arbitrary", "arbitrary"),
        name="adaln",
    )(cvec, ada_down, ada_up, ada_b.reshape(depth, 1, n))
    return out.reshape(depth * MOD_ROWS * N_MOD, 1, d)


POOL_ROW_CHUNK = 128


def _pool_kernel(st, tm, alpha, row_offset, want_h, xp_ref, x_ref, xn_ref, sh_ref, sc_ref, gm_ref, shf_ref, scf_ref,
                 w_ref, ps_ref, lng_ref, lnb_ref, *refs):
    ext_ref, z_ref = refs[-2:]
    xo_ref, ho_ref = refs[-4:-2] if want_h else (refs[-3], None)
    row0 = row_offset + pl.program_id(0) * tm
    pos0, seqlen = st.seq_pos(row0)
    first = pos0 == 0
    last = pos0 + tm == seqlen
    one_sc = 1.0 + sc_ref[0]
    sh = sh_ref[0]
    x = x_ref[...]
    ext_ref[pl.ds(HALO, tm), :] = x * one_sc + sh
    ext_ref[pl.ds(0, HALO), :] = jnp.where(first, 0.0, xp_ref[...] * one_sc + sh)
    ext_ref[pl.ds(HALO + tm, HALO), :] = jnp.where(last, 0.0, xn_ref[...] * one_sc + sh)

    pos = pos0 + lax.broadcasted_iota(jnp.int32, (tm, 1), 0)
    groups = len(POOL_WINDOWS)
    gw = st.d // groups
    for g, k in enumerate(POOL_WINDOWS):
        cols = pl.ds(g * gw, gw)
        lo = jnp.maximum(pos - k // 2, 0)
        hi = jnp.minimum(pos + (k - k // 2), seqlen)
        cnt = (hi - lo).astype(F32)
        n_ext = tm + 2 * HALO
        win = ext_ref[:, cols]
        win = win + pltpu.roll(win, 1, 0)
        n = 2
        while n < k:
            win = pltpu.roll(win, n_ext - n // 2, 0) + pltpu.roll(win, n // 2, 0)
            n *= 2
        win = win[HALO:HALO + tm]
        diff = (win / cnt - ext_ref[pl.ds(HALO, tm), cols]).astype(BF16)
        z_ref[:, cols] = jnp.dot(diff, w_ref[g], preferred_element_type=F32) * ps_ref[:, cols]

    rc = min(POOL_ROW_CHUNK, tm)

    def body(c, carry):
        rows = pl.ds(pl.multiple_of(c * rc, rc), rc)
        _deepnorm_epilogue(alpha, z_ref[rows, :], x_ref, gm_ref, lng_ref, lnb_ref, (shf_ref, scf_ref), xo_ref, ho_ref,
                           rows)
        return carry

    lax.fori_loop(0, tm // rc, body, 0)


def _pool_layer(st, n_out, layer, alpha, x, mods, w, pscale, lng, lnb, row_offset=0, filled=None, want_h=True):
    assert want_h or filled is None
    d = st.d
    tm = min(256, st.ctx_len)
    n_src = x.shape[0]
    prev, cur, nxt = _halo_specs(tm, d, n_src)
    groups, gw, _ = w.shape
    blk0 = row_offset // tm
    in_specs = [prev, cur, nxt] + [st.mod_spec(layer, q, tm, row_offset) for q in (0, 1, 2, 3, 4)] + [
        pl.BlockSpec((groups, gw, gw), lambda i: (0, 0, 0)), _vec_spec(d), _vec_spec(d), _vec_spec(d)]
    args = [x, x, x, mods, mods, mods, mods, mods, w, pscale.reshape(1, d), lng.reshape(1, d), lnb.reshape(1, d)]
    aliases = {}
    if filled is not None:
        aliases = {len(args): 0, len(args) + 1: 1}
        in_specs += [pl.BlockSpec(memory_space=pl.ANY)] * 2
        args += list(filled)
    n_outs = 2 if want_h else 1
    res = pl.pallas_call(
        functools.partial(_pool_kernel, st, tm, alpha, row_offset, want_h),
        out_shape=(jax.ShapeDtypeStruct((n_out, d), F32), jax.ShapeDtypeStruct((n_out, d), BF16))[:n_outs],
        grid=(n_src // tm,),
        in_specs=in_specs,
        out_specs=tuple(pl.BlockSpec((tm, d), lambda i: (blk0 + i, 0)) for _ in range(n_outs)),
        scratch_shapes=[pltpu.VMEM((tm + 2 * HALO, d), F32), pltpu.VMEM((tm, d), F32)],
        input_output_aliases=aliases,
        compiler_params=_params("arbitrary"),
        name="pool_mixer",
    )(*args)
    return res if want_h else (res[0], None)


def _qkv_kernel(n_qk_tiles, a_ref, b_ref, gain_ref, c_ref, s1_ref, s2_ref, o_ref):
    j = pl.program_id(1)
    acc = jnp.dot(a_ref[...], b_ref[...], preferred_element_type=F32)
    tn = acc.shape[1]

    @pl.when(j < n_qk_tiles)
    def _():
        cos, s1, s2 = c_ref[...], s1_ref[...], s2_ref[...]
        for hh in range(tn // HEAD_DIM):
            cols = slice(hh * HEAD_DIM, (hh + 1) * HEAD_DIM)
            xh = acc[:, cols]
            n = xh * lax.rsqrt(jnp.mean(xh * xh, axis=-1, keepdims=True) + RMS_EPS) * gain_ref[:, cols]
            half = AXIS_DIM // 2
            rot = n * cos + pltpu.roll(n, HEAD_DIM - half, 1) * s1 + pltpu.roll(n, half, 1) * s2
            o_ref[:, cols] = rot.astype(o_ref.dtype)

    @pl.when(j >= n_qk_tiles)
    def _():
        o_ref[...] = acc.astype(o_ref.dtype)


def _rope_tables(seq, tm):
    rows = seq // GRID_W
    r = jnp.repeat(jnp.arange(rows, dtype=F32), GRID_W)
    col = jnp.tile(jnp.arange(GRID_W, dtype=F32), rows)
    inv = ROPE_THETA ** (-jnp.arange(0, AXIS_DIM, 2, dtype=F32) / AXIS_DIM)
    ang_r = r[:, None] * inv
    ang_c = col[:, None] * inv
    zero = jnp.zeros_like(ang_r)
    cos = jnp.concatenate([jnp.cos(ang_r), jnp.cos(ang_r), jnp.cos(ang_c), jnp.cos(ang_c)], axis=-1)
    s1 = jnp.concatenate([-jnp.sin(ang_r), zero, -jnp.sin(ang_c), zero], axis=-1)
    s2 = jnp.concatenate([zero, jnp.sin(ang_r), zero, jnp.sin(ang_c)], axis=-1)
    ident = jnp.ones((tm, HEAD_DIM), F32)
    nul = jnp.zeros((tm, HEAD_DIM), F32)
    return (jnp.concatenate([cos, ident]), jnp.concatenate([s1, nul]), jnp.concatenate([s2, nul]))


def _qkv_proj(st, h, w, q_gain, k_gain):
    n_rows, d = h.shape
    n = w.shape[1]
    kv = (n - d) // 2
    tm = st.row_tile(1024)
    tn = _pick(kv, (1024, 512, 256, 128))
    scale = HEAD_DIM ** -0.5 * LOG2E
    gain = jnp.concatenate([jnp.tile(q_gain * scale, d // HEAD_DIM), jnp.tile(k_gain, kv // HEAD_DIM),
                            jnp.ones((kv,), F32)]).reshape(1, n)
    cos, s1, s2 = _rope_tables(st.seq, tm)
    per_seq = st.seq // tm

    def tab_map(i, j):
        return (jnp.where(i * tm < st.n_lat, i % per_seq, per_seq), 0)

    tab = pl.BlockSpec((tm, HEAD_DIM), tab_map)
    return pl.pallas_call(
        functools.partial(_qkv_kernel, (d + kv) // tn),
        out_shape=jax.ShapeDtypeStruct((n_rows, n), BF16),
        grid=(n_rows // tm, n // tn),
        in_specs=[pl.BlockSpec((tm, d), lambda i, j: (i, 0)), pl.BlockSpec((d, tn), lambda i, j: (0, j)),
                  pl.BlockSpec((1, tn), lambda i, j: (0, j)), tab, tab, tab],
        out_specs=pl.BlockSpec((tm, tn), lambda i, j: (i, j)),
        compiler_params=_params("arbitrary", "arbitrary"),
        name="qkv_proj",
    )(h, w, gain, cos, s1, s2)


def _gelu_tanh(x):
    return 0.5 * x * (1.0 + jnp.tanh(0.7978845608028654 * (x + 0.044715 * (x * x * x))))


def _lru_in_kernel(n_gelu_tiles, a_ref, b_ref, o_ref):
    j = pl.program_id(1)
    acc = jnp.dot(a_ref[...], b_ref[...], preferred_element_type=F32)

    @pl.when(j < n_gelu_tiles)
    def _():
        o_ref[...] = _gelu_tanh(acc)

    @pl.when(j >= n_gelu_tiles)
    def _():
        o_ref[...] = acc


def _lru_in_proj(h, w):
    n_rows, d = h.shape
    n = w.shape[1]
    tm = _pick(n_rows, (1024, 512, 256, 128, 64))
    tn = _pick(n // 2, (1024, 512, 256, 128))
    return pl.pallas_call(
        functools.partial(_lru_in_kernel, (n // 2) // tn),
        out_shape=jax.ShapeDtypeStruct((n_rows, n), F32),
        grid=(n_rows // tm, n // tn),
        in_specs=[pl.BlockSpec((tm, d), lambda i, j: (i, 0)), pl.BlockSpec((d, tn), lambda i, j: (0, j))],
        out_specs=pl.BlockSpec((tm, tn), lambda i, j: (i, j)),
        compiler_params=_params("arbitrary", "arbitrary"),
        name="lru_in_proj",
    )(h, w)


def _swiglu_kernel(a_ref, wg_ref, wu_ref, o_ref):
    a = a_ref[...]
    gate = jnp.dot(a, wg_ref[0], preferred_element_type=F32)
    up = jnp.dot(a, wu_ref[0], preferred_element_type=F32)
    o_ref[...] = (gate * jax.nn.sigmoid(gate) * up).astype(o_ref.dtype)


def _swiglu_up(h, w, n_rows):
    e, d, f2 = w.shape
    f = f2 // 2
    tm = _pick(n_rows, (1024, 512, 256, 128, 64))
    tn = _pick(f, (512, 256, 128))
    nj = f // tn
    return pl.pallas_call(
        _swiglu_kernel,
        out_shape=jax.ShapeDtypeStruct((n_rows, e * f), BF16),
        grid=(n_rows // tm, e, nj),
        in_specs=[pl.BlockSpec((tm, d), lambda i, x, j: (i, 0)),
                  pl.BlockSpec((1, d, tn), lambda i, x, j: (x, 0, j)),
                  pl.BlockSpec((1, d, tn), lambda i, x, j: (x, 0, nj + j))],
        out_specs=pl.BlockSpec((tm, tn), lambda i, x, j: (i, x * nj + j)),
        compiler_params=_params("arbitrary", "arbitrary", "arbitrary"),
        name="swiglu_up",
    )(h, w, w)


def _deepnorm_epilogue(alpha, y, x_ref, gate_ref, lng_ref, lnb_ref, next_refs, xo_ref, ho_ref, rows=slice(None)):
    xn = _ln_rows(alpha * x_ref[rows, :] + gate_ref[0] * y, lng_ref[...], lnb_ref[...])
    xo_ref[rows, :] = xn
    if ho_ref is not None:
        shn_ref, scn_ref = next_refs
        ho_ref[rows, :] = (xn * (1.0 + scn_ref[0]) + shn_ref[0]).astype(BF16)


MM_LN_COL_CHUNK = 1024
MM_LN_ROW_CHUNK = 128
MM_LN_X_SLOTS = 4
MM_LN_OUT_SLOTS = 2


def _mm_ln_kernel(alpha, nk, has_next, *refs):
    a_ref, b_ref, gate_ref, lng_ref, lnb_ref = refs[:5]
    next_refs = refs[5:7] if has_next else None
    refs = refs[7:] if has_next else refs[5:]
    if has_next:
        x_hbm, xo_hbm, ho_hbm, acc_ref, xbuf, obuf, hbuf, xsem, osem, hsem = refs
    else:
        x_hbm, xo_hbm, acc_ref, xbuf, obuf, xsem, osem = refs
        ho_hbm = hbuf = hsem = None
    i, k = pl.program_id(0), pl.program_id(1)
    tm, d = acc_ref.shape
    x_slots, rc = xbuf.shape[:2]
    o_slots = obuf.shape[0]
    n_chunks = tm // rc

    def hbm_rows(ref, c):
        return ref.at[pl.ds(pl.multiple_of(i * tm + c * rc, rc), rc), :]

    def x_copy(c, slot):
        return pltpu.make_async_copy(hbm_rows(x_hbm, c), xbuf.at[slot], xsem.at[slot])

    def o_copy(c, slot):
        return pltpu.make_async_copy(obuf.at[slot], hbm_rows(xo_hbm, c), osem.at[slot])

    def h_copy(c, slot):
        return pltpu.make_async_copy(hbuf.at[slot], hbm_rows(ho_hbm, c), hsem.at[slot])

    def wait_out(c, slot):
        o_copy(c, slot).wait()
        if has_next:
            h_copy(c, slot).wait()

    @pl.when(k == 0)
    def _():
        acc_ref[...] = jnp.zeros_like(acc_ref)

    @pl.when(k == nk - 1)
    def _():
        for c in range(min(x_slots, n_chunks)):
            x_copy(c, c).start()

    a = a_ref[...]
    cw = min(MM_LN_COL_CHUNK, d)
    for c in range(d // cw):
        cols = slice(c * cw, (c + 1) * cw)
        acc_ref[:, cols] += jnp.dot(a, b_ref[:, cols], preferred_element_type=F32)

    @pl.when(k == nk - 1)
    def _():
        def body(c, carry):
            xs, os_ = c % x_slots, c % o_slots
            x_copy(c, xs).wait()

            @pl.when(c >= o_slots)
            def _():
                wait_out(c - o_slots, os_)

            y = acc_ref[pl.ds(pl.multiple_of(c * rc, rc), rc), :]
            _deepnorm_epilogue(alpha, y, xbuf.at[xs], gate_ref, lng_ref, lnb_ref, next_refs, obuf.at[os_],
                               hbuf.at[os_] if has_next else None)
            o_copy(c, os_).start()
            if has_next:
                h_copy(c, os_).start()

            @pl.when(c + x_slots < n_chunks)
            def _():
                x_copy(c + x_slots, xs).start()

            return carry

        lax.fori_loop(0, n_chunks, body, 0)
        for c in range(max(n_chunks - o_slots, 0), n_chunks):
            wait_out(c, c % o_slots)


def _mm_ln(st, n_rows, alpha, a, b, x, mods, layer, gate_idx, lng, lnb, next_mod=None):
    kdim, d = b.shape
    tm = st.row_tile(1024)
    tk = _pick(kdim, (1024, 512, 256, 128))
    nk = kdim // tk
    rc = min(MM_LN_ROW_CHUNK, tm)
    has_next = next_mod is not None
    any_spec = pl.BlockSpec(memory_space=pl.ANY)
    in_specs = [pl.BlockSpec((tm, tk), lambda i, k: (i, k)), pl.BlockSpec((tk, d), lambda i, k: (k, 0)),
                st.mod_spec(layer, gate_idx, tm), _vec_spec(d), _vec_spec(d)]
    args = [a, b, mods, lng.reshape(1, d), lnb.reshape(1, d)]
    out_shape = [jax.ShapeDtypeStruct((n_rows, d), F32)]
    scratch = [pltpu.VMEM((tm, d), F32), pltpu.VMEM((MM_LN_X_SLOTS, rc, d), F32),
               pltpu.VMEM((MM_LN_OUT_SLOTS, rc, d), F32)]
    sems = [pltpu.SemaphoreType.DMA((MM_LN_X_SLOTS,)), pltpu.SemaphoreType.DMA((MM_LN_OUT_SLOTS,))]
    if has_next:
        nl, n_sh, n_sc = next_mod
        in_specs += [st.mod_spec(nl, n_sh, tm), st.mod_spec(nl, n_sc, tm)]
        args += [mods, mods]
        out_shape.append(jax.ShapeDtypeStruct((n_rows, d), BF16))
        scratch.append(pltpu.VMEM((MM_LN_OUT_SLOTS, rc, d), BF16))
        sems.append(pltpu.SemaphoreType.DMA((MM_LN_OUT_SLOTS,)))
    res = pl.pallas_call(
        functools.partial(_mm_ln_kernel, alpha, nk, has_next),
        out_shape=tuple(out_shape),
        grid=(n_rows // tm, nk),
        in_specs=in_specs + [any_spec],
        out_specs=tuple(any_spec for _ in out_shape),
        scratch_shapes=scratch + sems,
        compiler_params=_params("arbitrary", "arbitrary"),
        name="matmul_deepnorm",
    )(*args, x)
    return res if has_next else (res[0], None)


ATTN_KEY_ALIGN = 256
ATTN_KEY_RANGES = 2


def _attn_kernel(n_lat_tiles, group, q_ref, kl_ref, vl_ref, kc_ref, vc_ref, o_ref, k_all, v_all):
    qi = pl.program_id(2)
    seq, ctx_len = kl_ref.shape[0], kc_ref.shape[0]

    @pl.when(qi == 0)
    def _():
        k_all[pl.ds(0, seq), :] = kl_ref[...]
        k_all[pl.ds(seq, ctx_len), :] = kc_ref[...]
        v_all[pl.ds(0, seq), pl.ds(0, HEAD_DIM)] = vl_ref[...]
        v_all[pl.ds(seq, ctx_len), pl.ds(0, HEAD_DIM)] = vc_ref[...]
        v_all[:, pl.ds(HEAD_DIM, HEAD_DIM)] = jnp.ones((seq + ctx_len, HEAD_DIM), v_all.dtype)

    def partial_softmax(q, rows):
        s = lax.dot_general(q, k_all[rows, :], (((1,), (1,)), ((), ())), preferred_element_type=F32)
        m = jnp.max(s, axis=-1, keepdims=True)
        return jnp.dot(jnp.exp2(s - m).astype(v_all.dtype), v_all[rows, :], preferred_element_type=F32), m

    def attend(key_ranges):
        for g in range(group):
            cols = slice(g * HEAD_DIM, (g + 1) * HEAD_DIM)
            q = q_ref[:, cols]
            parts = [partial_softmax(q, rows) for rows in key_ranges]
            o, m = parts[0]
            for o2, m2 in parts[1:]:
                m_new = jnp.maximum(m, m2)
                o = o * jnp.exp2(m - m_new) + o2 * jnp.exp2(m2 - m_new)
                m = m_new
            o_ref[:, cols] = (o[:, :HEAD_DIM] / o[:, HEAD_DIM:]).astype(o_ref.dtype)

    @pl.when(qi < n_lat_tiles)
    def _():
        n_keys = seq + ctx_len
        step = _round_up(-(-n_keys // ATTN_KEY_RANGES), ATTN_KEY_ALIGN)
        attend([pl.ds(lo, min(step, n_keys - lo)) for lo in range(0, n_keys, step)])

    @pl.when(qi >= n_lat_tiles)
    def _():
        attend([pl.ds(seq, ctx_len)])


def _attention(st, qkv, d):
    n_rows, n = qkv.shape
    kv = (n - d) // 2
    n_kv = kv // HEAD_DIM
    group = d // kv
    tq = min(256, st.ctx_len)
    ctx_tiles = st.ctx_len // tq
    lat_tiles = st.seq // tq
    qw = group * HEAD_DIM
    k_col0 = d // HEAD_DIM
    v_col0 = (d + kv) // HEAD_DIM

    def q_map(b, h, qi):
        row = jnp.where(qi < lat_tiles, b * lat_tiles + qi, st.n_lat // tq + b * ctx_tiles + (qi - lat_tiles))
        return (row, h)

    ctx_blk0 = st.n_lat // st.ctx_len
    return pl.pallas_call(
        functools.partial(_attn_kernel, lat_tiles, group),
        out_shape=jax.ShapeDtypeStruct((n_rows, d), BF16),
        grid=(st.batch, n_kv, lat_tiles + ctx_tiles),
        in_specs=[pl.BlockSpec((tq, qw), q_map),
                  pl.BlockSpec((st.seq, HEAD_DIM), lambda b, h, qi: (b, k_col0 + h)),
                  pl.BlockSpec((st.seq, HEAD_DIM), lambda b, h, qi: (b, v_col0 + h)),
                  pl.BlockSpec((st.ctx_len, HEAD_DIM), lambda b, h, qi: (ctx_blk0 + b, k_col0 + h)),
                  pl.BlockSpec((st.ctx_len, HEAD_DIM), lambda b, h, qi: (ctx_blk0 + b, v_col0 + h))],
        out_specs=pl.BlockSpec((tq, qw), q_map),
        scratch_shapes=[pltpu.VMEM((st.seq + st.ctx_len, HEAD_DIM), qkv.dtype),
                        pltpu.VMEM((st.seq + st.ctx_len, 2 * HEAD_DIM), qkv.dtype)],
        compiler_params=_params("arbitrary", "arbitrary", "arbitrary"),
        name="gqa_attention",
    )(qkv, qkv, qkv, qkv, qkv)


def _sigmoid(x):
    return 0.5 * jnp.tanh(0.5 * x) + 0.5


def _scan8(a, b, reverse):
    row = lax.broadcasted_iota(jnp.int32, a.shape, 0)
    for s in (1, 2, 4):
        if reverse:
            keep = row < SUBLANES - s
            shift = SUBLANES - s
        else:
            keep = row >= s
            shift = s
        a_sh = jnp.where(keep, pltpu.roll(a, shift, 0), 1.0)
        b_sh = jnp.where(keep, pltpu.roll(b, shift, 0), 0.0)
        b = a * b_sh + b
        a = a * a_sh
    return a, b


def _lru_kernel(tm, lat_tiles, reverse, *refs):
    if reverse:
        xr_ref, gw_ref, gb_ref, lam_ref, recf_ref, gelu_ref, o_ref, a_ref, b_ref, carry_ref = refs
    else:
        (xp_ref, x_ref, xn_ref, cw_ref, cb_ref, gw_ref, gb_ref, lam_ref,
         o_ref, xr_ref, ext_ref, a_ref, b_ref, carry_ref) = refs
    s = pl.program_id(1)
    is_ctx = s == 0

    @pl.when(is_ctx)
    def _():
        carry_ref[...] = jnp.zeros_like(carry_ref)

    if not reverse:
        tile = s - 1
        first = jnp.logical_or(is_ctx, tile == 0)
        last = jnp.logical_or(is_ctx, tile == lat_tiles - 1)
        ext_ref[pl.ds(HALO, tm), :] = x_ref[...]
        ext_ref[pl.ds(0, HALO), :] = jnp.where(first, 0.0, xp_ref[...])
        ext_ref[pl.ds(HALO + tm, HALO), :] = jnp.where(last, 0.0, xn_ref[...])

    heads, bw, _ = gw_ref.shape
    lam = lam_ref[...]
    softplus = jnp.maximum(-lam, 0.0) + jnp.log1p(jnp.exp(-jnp.abs(lam)))
    for h in range(heads):
        cols = pl.ds(h * bw, bw)
        if reverse:
            xr = xr_ref[:, cols]
        else:
            xr = cb_ref[:, cols] + ext_ref[pl.ds(HALO - CONV_LEFT, tm), cols] * cw_ref[0:1, cols]
            for k in range(1, CONV_W):
                xr = xr + ext_ref[pl.ds(HALO - CONV_LEFT + k, tm), cols] * cw_ref[k:k + 1, cols]
            xr_ref[:, cols] = xr
        g = jnp.dot(xr.astype(BF16), gw_ref[h], preferred_element_type=F32)
        r = _sigmoid(g[:, :bw] + gb_ref[0:1, cols])
        i = _sigmoid(g[:, bw:] + gb_ref[1:2, cols])
        log_a = -LRU_C * r * softplus[:, h * bw:(h + 1) * bw]
        a = jnp.exp(log_a)
        a_ref[:, cols] = a
        om = 1.0 - a * a
        b_ref[:, cols] = jnp.where(om > 0.0, om * lax.rsqrt(om), 0.0) * (i * xr)

    n_chunks = tm // SUBLANES

    def body(c, carry):
        c = n_chunks - 1 - c if reverse else c
        rows = pl.ds(pl.multiple_of(c * SUBLANES, SUBLANES), SUBLANES)
        a_cum, b_cum = _scan8(a_ref[rows, :], b_ref[rows, :], reverse)
        hs = a_cum * carry + b_cum
        if reverse:
            o_ref[rows, :] = (gelu_ref[rows, :] * (recf_ref[rows, :] + hs)).astype(o_ref.dtype)
            edge = hs[0:1, :]
        else:
            o_ref[rows, :] = hs
            edge = hs[SUBLANES - 1:SUBLANES, :]
        return jnp.broadcast_to(edge, carry.shape)

    carry_ref[...] = lax.fori_loop(0, n_chunks, body, carry_ref[...])


def _lru_scan(st, yx, conv_w, conv_b, gate_w, gate_b, lam, reverse, fwd=None):
    n_rows = yx.shape[0]
    d = st.d
    tm = min(256, st.ctx_len)
    assert st.ctx_len == tm
    lat_tiles = st.seq // tm
    ctx_blk0 = st.n_lat // tm
    per = tm // HALO
    last_halo = n_rows // HALO - 1

    def blk(b, s):
        lat = b * lat_tiles + (lat_tiles - s if reverse else s - 1)
        return jnp.where(s == 0, ctx_blk0 + b, lat)

    heads = gate_w.shape[1]
    bw = d // heads
    gw = jnp.concatenate([gate_w[0], gate_w[1]], axis=-1).astype(BF16)
    tile = pl.BlockSpec((tm, d), lambda b, s: (blk(b, s), 0))
    gate_specs = [pl.BlockSpec((heads, bw, 2 * bw), lambda b, s: (0, 0, 0)),
                  pl.BlockSpec((2, d), lambda b, s: (0, 0)), _vec_spec(d)]
    gate_args = [gw, gate_b.reshape(2, d), lam.reshape(1, d)]
    scratch = [pltpu.VMEM((tm, d), F32), pltpu.VMEM((tm, d), F32), pltpu.VMEM((SUBLANES, d), F32)]
    if reverse:
        rec_f, xr = fwd
        in_specs = [tile] + gate_specs + [tile, tile]
        args = [xr] + gate_args + [rec_f, yx]
        out_shape = jax.ShapeDtypeStruct((n_rows, d), BF16)
        out_specs = tile
    else:
        in_specs = [pl.BlockSpec((HALO, d), lambda b, s: (jnp.maximum(blk(b, s) * per - 1, 0), 1)),
                    pl.BlockSpec((tm, d), lambda b, s: (blk(b, s), 1)),
                    pl.BlockSpec((HALO, d), lambda b, s: (jnp.minimum((blk(b, s) + 1) * per, last_halo), 1)),
                    pl.BlockSpec((CONV_W, d), lambda b, s: (0, 0)), _vec_spec(d)] + gate_specs
        args = [yx, yx, yx, conv_w, conv_b.reshape(1, d)] + gate_args
        out_shape = (jax.ShapeDtypeStruct((n_rows, d), F32), jax.ShapeDtypeStruct((n_rows, d), F32))
        out_specs = (tile, tile)
        scratch = [pltpu.VMEM((tm + 2 * HALO, d), F32)] + scratch
    return pl.pallas_call(
        functools.partial(_lru_kernel, tm, lat_tiles, reverse),
        out_shape=out_shape,
        grid=(st.batch, lat_tiles + 1),
        in_specs=in_specs,
        out_specs=out_specs,
        scratch_shapes=scratch,
        compiler_params=_params("arbitrary", "arbitrary"),
        name="rglru_reverse" if reverse else "rglru_forward",
    )(*args)


ROUTE_LANES = 8
MOE_ROW_TILE = 512


def _split_hi_lo(x):
    hi = lax.bitcast_convert_type(lax.bitcast_convert_type(x, jnp.uint32) & jnp.uint32(HI16), F32)
    return hi.astype(jnp.bfloat16), (x - hi).astype(jnp.bfloat16)


def _lane_pick(rec, k):
    lane = lax.broadcasted_iota(jnp.int32, rec.shape, 1)
    return jnp.sum(jnp.where(lane == k, rec, 0.0), axis=1, keepdims=True)


def _router_kernel(x_ref, sh_ref, sc_ref, w_ref, route_ref, cnt_ref, run_ref):
    @pl.when(pl.program_id(0) == 0)
    def _():
        run_ref[...] = jnp.zeros_like(run_ref)

    h = x_ref[...] * (1.0 + sc_ref[0]) + sh_ref[0]
    n_exp = w_ref.shape[1] // 2
    h_hi, h_lo = _split_hi_lo(h)
    t = jnp.dot(h_hi, w_ref[...], preferred_element_type=F32)
    u = jnp.dot(h_lo, w_ref[:, :n_exp], preferred_element_type=F32)
    logits = (t[:, n_exp:] + u) + t[:, :n_exp]
    tm = logits.shape[0]
    lane = lax.broadcasted_iota(jnp.int32, logits.shape, 1).astype(F32)
    m1 = jnp.max(logits, axis=-1, keepdims=True)
    i1 = jnp.min(jnp.where(logits == m1, lane, float(n_exp)), axis=-1, keepdims=True)
    pick1 = lane == i1
    rest = jnp.where(pick1, -jnp.inf, logits)
    m2 = jnp.max(rest, axis=-1, keepdims=True)
    i2 = jnp.min(jnp.where(rest == m2, lane, float(n_exp)), axis=-1, keepdims=True)
    pick2 = lane == i2
    e2 = jnp.exp(m2 - m1)
    den = 1.0 + e2
    onehot = jnp.where(pick1, 1.0, jnp.where(pick2, 1.0, 0.0))
    earlier = (lax.broadcasted_iota(jnp.int32, (tm, tm), 0) > lax.broadcasted_iota(jnp.int32, (tm, tm), 1))
    before = jnp.dot(jnp.where(earlier, 1.0, 0.0).astype(BF16), onehot.astype(BF16),
                     preferred_element_type=F32) + run_ref[...]
    rank1 = jnp.sum(jnp.where(pick1, before, 0.0), axis=-1, keepdims=True)
    rank2 = jnp.sum(jnp.where(pick2, before, 0.0), axis=-1, keepdims=True)
    rec = jnp.zeros((tm, ROUTE_LANES), F32)
    for k, v in enumerate((i1, i2, 1.0 / den, e2 / den, rank1, rank2)):
        rec = jnp.where(lax.broadcasted_iota(jnp.int32, rec.shape, 1) == k, v, rec)
    route_ref[...] = rec
    run_ref[...] += jnp.sum(onehot, axis=0, keepdims=True)
    cnt_ref[...] = run_ref[...]


def _router(st, n_rows, x, mods, layer, w):
    d, n_exp = w.shape
    tm = min(256, st.ctx_len)
    w_split = jnp.concatenate(_split_hi_lo(w), axis=1)
    return pl.pallas_call(
        _router_kernel,
        out_shape=(jax.ShapeDtypeStruct((n_rows, ROUTE_LANES), F32), jax.ShapeDtypeStruct((1, n_exp), F32)),
        grid=(n_rows // tm,),
        in_specs=[pl.BlockSpec((tm, d), lambda i: (i, 0)), st.mod_spec(layer, 3, tm), st.mod_spec(layer, 4, tm),
                  pl.BlockSpec((d, 2 * n_exp), lambda i: (0, 0))],
        out_specs=(pl.BlockSpec((tm, ROUTE_LANES), lambda i: (i, 0)), pl.BlockSpec((1, n_exp), lambda i: (0, 0))),
        scratch_shapes=[pltpu.VMEM((1, n_exp), F32)],
        compiler_params=_params("arbitrary"),
        name="moe_router",
    )(x, mods, mods, w_split)


def _dispatch_plan(route, counts, tm):
    n_tok = route.shape[0]
    n_exp = counts.shape[1]
    expert = route[:, 0:2].astype(jnp.int32)
    rank = route[:, 4:6].astype(jnp.int32)
    cnt = counts[0].astype(jnp.int32)
    padded = (cnt + tm - 1) // tm * tm
    ends = jnp.cumsum(padded)
    starts = ends - padded
    dest = (starts[expert] + rank).T.reshape(2 * n_tok)
    n_tiles = (2 * n_tok + n_exp * tm) // tm
    tile_row0 = jnp.arange(n_tiles, dtype=jnp.int32) * tm
    tile_expert = jnp.minimum(jnp.sum(tile_row0[:, None] >= ends[None, :], axis=1), n_exp - 1).astype(jnp.int32)
    n_used = (ends[-1:] // tm).astype(jnp.int32)
    return dest, tile_expert, n_used, n_tiles


HI16 = 0xFFFF0000
ROW_DMA_UNROLL = 8


def _pack_bf16_pairs(lo, hi):
    lo = lax.bitcast_convert_type(lo.astype(jnp.bfloat16).astype(F32), jnp.uint32)
    hi = lax.bitcast_convert_type(hi.astype(jnp.bfloat16).astype(F32), jnp.uint32)
    return (lo >> 16) | (hi & jnp.uint32(HI16))


def _unpack_bf16_pairs(p):
    return lax.bitcast_convert_type(p << 16, F32), lax.bitcast_convert_type(p & jnp.uint32(HI16), F32)


def _zeros_kernel(o_ref):
    o_ref[...] = jnp.zeros_like(o_ref)


def _zeros(rows, cols, dtype):
    tr = _pick(rows, (1024, 512, 256, 128, 64, 32, 16, 8))
    return pl.pallas_call(
        _zeros_kernel,
        out_shape=jax.ShapeDtypeStruct((rows, cols), dtype),
        grid=(rows // tr,),
        out_specs=pl.BlockSpec((tr, cols), lambda i: (i, 0)),
        compiler_params=_params("arbitrary"),
        name="zero_fill",
    )()


def _dispatch_kernel(tm, n_tok, dest_ref, x_ref, sh_ref, sc_ref, init_ref, o_ref, h_ref, sem):
    del init_ref
    base = pl.program_id(0) * tm
    h = x_ref[...] * (1.0 + sc_ref[0]) + sh_ref[0]
    half = h.shape[1] // 2
    h_ref[...] = _pack_bf16_pairs(h[:, :half], h[:, half:])

    def row_copy(r, d):
        return pltpu.make_async_copy(h_ref.at[pl.ds(r, 1), :], o_ref.at[pl.ds(d, 1), :], sem)

    def issue(r, carry):
        row_copy(r, dest_ref[base + r]).start()
        row_copy(r, dest_ref[n_tok + base + r]).start()
        return carry

    def drain(r, carry):
        row_copy(r, 0).wait()
        row_copy(r, 0).wait()
        return carry

    lax.fori_loop(0, tm, issue, 0, unroll=ROW_DMA_UNROLL)
    lax.fori_loop(0, tm, drain, 0, unroll=ROW_DMA_UNROLL)


def _dispatch(st, n_rows, x, mods, layer, dest, n_out):
    d = st.d
    tm = min(256, st.ctx_len)
    grid_spec = pltpu.PrefetchScalarGridSpec(
        num_scalar_prefetch=1,
        grid=(n_rows // tm,),
        in_specs=[pl.BlockSpec((tm, d), lambda i, dest: (i, 0)), st.mod_spec(layer, 3, tm), st.mod_spec(layer, 4, tm),
                  pl.BlockSpec(memory_space=pl.ANY)],
        out_specs=pl.BlockSpec(memory_space=pl.ANY),
        scratch_shapes=[pltpu.VMEM((tm, d // 2), jnp.uint32), pltpu.SemaphoreType.DMA],
    )
    return pl.pallas_call(
        functools.partial(_dispatch_kernel, tm, n_rows),
        out_shape=jax.ShapeDtypeStruct((n_out, d // 2), jnp.uint32),
        grid_spec=grid_spec,
        input_output_aliases={4: 0},
        compiler_params=_params("arbitrary"),
        name="moe_dispatch",
    )(dest, x, mods, mods, _zeros(n_out, d // 2, jnp.uint32))


def _moe_up_kernel(te_ref, nu_ref, a_ref, wg_ref, wu_ref, o_ref, wbf_ref):
    i = pl.program_id(1)

    @pl.when(jnp.logical_or(i == 0, te_ref[i] != te_ref[jnp.maximum(i - 1, 0)]))
    def _():
        wbf_ref[0] = wg_ref[...].astype(wbf_ref.dtype)
        wbf_ref[1] = wu_ref[...].astype(wbf_ref.dtype)

    @pl.when(i < nu_ref[0])
    def _():
        lo, hi = (t.astype(jnp.bfloat16) for t in _unpack_bf16_pairs(a_ref[...]))
        half = lo.shape[1]

        def proj(which):
            return (jnp.dot(lo, wbf_ref[which, :half, :], preferred_element_type=F32)
                    + jnp.dot(hi, wbf_ref[which, half:, :], preferred_element_type=F32))

        gate, up = proj(0), proj(1)
        o_ref[...] = (gate * jax.nn.sigmoid(gate) * up).astype(o_ref.dtype)

    @pl.when(i >= nu_ref[0])
    def _():
        o_ref[...] = jnp.zeros_like(o_ref)


def _moe_up(xg, w, layer, tile_expert, n_used, tm):
    n_out = xg.shape[0]
    d, f2 = w.shape[2:]
    f = f2 // 2
    tn = _pick(f, (512, 256, 128))
    nj = f // tn
    grid_spec = pltpu.PrefetchScalarGridSpec(
        num_scalar_prefetch=2,
        grid=(nj, n_out // tm),
        in_specs=[pl.BlockSpec((tm, d // 2), lambda j, i, te, nu: (i, 0)),
                  pl.BlockSpec((None, None, d, tn), lambda j, i, te, nu: (layer, te[i], 0, j)),
                  pl.BlockSpec((None, None, d, tn), lambda j, i, te, nu: (layer, te[i], 0, nj + j))],
        out_specs=pl.BlockSpec((tm, tn), lambda j, i, te, nu: (i, j)),
        scratch_shapes=[pltpu.VMEM((2, d, tn), BF16)],
    )
    return pl.pallas_call(
        _moe_up_kernel,
        out_shape=jax.ShapeDtypeStruct((n_out, f), BF16),
        grid_spec=grid_spec,
        compiler_params=_params("arbitrary", "arbitrary"),
        name="moe_up",
    )(tile_expert, n_used, xg, w, w)


def _moe_down_kernel(te_ref, nu_ref, a_ref, w_ref, o_ref, wbf_ref):
    i = pl.program_id(1)

    @pl.when(jnp.logical_or(i == 0, te_ref[i] != te_ref[jnp.maximum(i - 1, 0)]))
    def _():
        wbf_ref[...] = w_ref[...].astype(wbf_ref.dtype)

    @pl.when(i < nu_ref[0])
    def _():
        a = a_ref[...]
        half = wbf_ref.shape[1] // 2
        o_ref[...] = _pack_bf16_pairs(jnp.dot(a, wbf_ref[:, :half], preferred_element_type=F32),
                                      jnp.dot(a, wbf_ref[:, half:], preferred_element_type=F32))

    @pl.when(i >= nu_ref[0])
    def _():
        o_ref[...] = jnp.zeros_like(o_ref)


MOE_DOWN_COLS = 2048


def _moe_down(act, w, layer, tile_expert, n_used, tm):
    n_out, f = act.shape
    d = w.shape[3]
    tn = min(MOE_DOWN_COLS, d)
    grid_spec = pltpu.PrefetchScalarGridSpec(
        num_scalar_prefetch=2,
        grid=(d // tn, n_out // tm),
        in_specs=[pl.BlockSpec((tm, f), lambda j, i, te, nu: (i, 0)),
                  pl.BlockSpec((None, None, f, tn), lambda j, i, te, nu: (layer, te[i], 0, j))],
        out_specs=pl.BlockSpec((tm, tn // 2), lambda j, i, te, nu: (i, j)),
        scratch_shapes=[pltpu.VMEM((f, tn), BF16)],
    )
    return pl.pallas_call(
        _moe_down_kernel,
        out_shape=jax.ShapeDtypeStruct((n_out, d // 2), jnp.uint32),
        grid_spec=grid_spec,
        compiler_params=_params("arbitrary", "arbitrary"),
        name="moe_down",
    )(tile_expert, n_used, act, w)


def _unpack_expert_rows(p, d):
    tn = min(MOE_DOWN_COLS, d)
    lo, hi = _unpack_bf16_pairs(p)
    half = tn // 2
    parts = []
    for j in range(d // tn):
        parts += [lo[:, j * half:(j + 1) * half], hi[:, j * half:(j + 1) * half]]
    return parts


COMBINE_ROW_CHUNK = 128


def _combine_kernel(tm, n_tok, alpha, has_next, dest_ref, *refs):
    route_ref, x_ref, gate_ref, lng_ref, lnb_ref = refs[:5]
    next_refs = refs[5:7] if has_next else None
    refs = refs[7:] if has_next else refs[5:]
    y_ref, xo_ref = refs[:2]
    ho_ref = refs[2] if has_next else None
    buf_ref, sem = refs[-2:]
    i = pl.program_id(0)
    n_steps = pl.num_programs(0)
    slot = i % 2

    def row_copy(slot, which, r, src):
        return pltpu.make_async_copy(y_ref.at[pl.ds(src, 1), :], buf_ref.at[slot, which, pl.ds(r, 1), :],
                                     sem.at[slot])

    def gather(tile, slot):
        def issue(r, carry):
            row_copy(slot, 0, r, dest_ref[tile * tm + r]).start()
            row_copy(slot, 1, r, dest_ref[n_tok + tile * tm + r]).start()
            return carry
        lax.fori_loop(0, tm, issue, 0, unroll=ROW_DMA_UNROLL)

    @pl.when(i == 0)
    def _():
        gather(0, 0)

    @pl.when(i + 1 < n_steps)
    def _():
        gather(i + 1, 1 - slot)

    def drain(r, carry):
        row_copy(slot, 0, r, 0).wait()
        row_copy(slot, 1, r, 0).wait()
        return carry

    lax.fori_loop(0, tm, drain, 0, unroll=ROW_DMA_UNROLL)
    rc = min(COMBINE_ROW_CHUNK, tm)

    def body(c, carry):
        rows = pl.ds(pl.multiple_of(c * rc, rc), rc)
        rec = route_ref[rows, :]
        g1, g2 = _lane_pick(rec, 2), _lane_pick(rec, 3)
        d = x_ref.shape[1]
        y1 = _unpack_expert_rows(buf_ref[slot, 0, rows, :], d)
        y2 = _unpack_expert_rows(buf_ref[slot, 1, rows, :], d)
        y = jnp.concatenate([g1 * p1 + g2 * p2 for p1, p2 in zip(y1, y2)], axis=1)
        _deepnorm_epilogue(alpha, y, x_ref, gate_ref, lng_ref, lnb_ref, next_refs, xo_ref, ho_ref, rows)
        return carry

    lax.fori_loop(0, tm // rc, body, 0)


def _combine(st, n_rows, alpha, yg, route, dest, x, mods, layer, lng, lnb, next_mod=None):
    d = st.d
    tm = min(256, st.ctx_len)
    has_next = next_mod is not None
    row = lambda i, dest: (i, 0)
    in_specs = [pl.BlockSpec((tm, ROUTE_LANES), row), pl.BlockSpec((tm, d), row), st.mod_spec(layer, 5, tm),
                _vec_spec(d), _vec_spec(d)]
    args = [route, x, mods, lng.reshape(1, d), lnb.reshape(1, d)]
    out_shape = [jax.ShapeDtypeStruct((n_rows, d), F32)]
    out_specs = [pl.BlockSpec((tm, d), row)]
    if has_next:
        nl, n_sh, n_sc = next_mod
        in_specs += [st.mod_spec(nl, n_sh, tm), st.mod_spec(nl, n_sc, tm)]
        args += [mods, mods]
        out_shape.append(jax.ShapeDtypeStruct((n_rows, d), BF16))
        out_specs.append(pl.BlockSpec((tm, d), row))
    in_specs.append(pl.BlockSpec(memory_space=pl.ANY))
    args.append(yg)
    grid_spec = pltpu.PrefetchScalarGridSpec(
        num_scalar_prefetch=1,
        grid=(n_rows // tm,),
        in_specs=in_specs,
        out_specs=tuple(out_specs),
        scratch_shapes=[pltpu.VMEM((2, 2, tm, d // 2), jnp.uint32), pltpu.SemaphoreType.DMA((2,))],
    )
    res = pl.pallas_call(
        functools.partial(_combine_kernel, tm, n_rows, alpha, has_next),
        out_shape=tuple(out_shape),
        grid_spec=grid_spec,
        compiler_params=_params("arbitrary"),
        name="moe_combine",
    )(dest, *args)
    return res if has_next else (res[0], None)


def _moe_layer(st, n_rows, alpha, x, mods, layer, router_w, w_in, moe_index, w_out, lng, lnb, next_mod):
    route, counts = _router(st, n_rows, x, mods, layer, router_w)
    dest, tile_expert, n_used, n_tiles = _dispatch_plan(route, counts, MOE_ROW_TILE)
    xg = _dispatch(st, n_rows, x, mods, layer, dest, n_tiles * MOE_ROW_TILE)
    act = _moe_up(xg, w_in, moe_index, tile_expert, n_used, MOE_ROW_TILE)
    yg = _moe_down(act, w_out, moe_index, tile_expert, n_used, MOE_ROW_TILE)
    return _combine(st, n_rows, alpha, yg, route, dest, x, mods, layer, lng, lnb, next_mod)


def _cast_kernel(is_pad, x_ref, o_ref):
    pad = is_pad(pl.program_id(0), pl.program_id(1))

    @pl.when(jnp.logical_not(pad))
    def _():
        o_ref[...] = x_ref[...].astype(o_ref.dtype)

    @pl.when(pad)
    def _():
        o_ref[...] = jnp.zeros_like(o_ref)


def _cast_blocks(w, layer, block, out_blocks, src_of, n_src):
    def in_map(i, j):
        bi, bj = src_of(i, j)
        return (layer, jnp.minimum(bi, n_src[0] - 1), jnp.minimum(bj, n_src[1] - 1))

    def is_pad(i, j):
        bi, bj = src_of(i, j)
        return jnp.logical_or(bi >= n_src[0], bj >= n_src[1])

    return pl.pallas_call(
        functools.partial(_cast_kernel, is_pad),
        out_shape=jax.ShapeDtypeStruct((out_blocks[0] * block[0], out_blocks[1] * block[1]), BF16),
        grid=out_blocks,
        in_specs=[pl.BlockSpec((None,) + block, in_map)],
        out_specs=pl.BlockSpec(block, lambda i, j: (i, j)),
        compiler_params=_params("arbitrary", "arbitrary"),
        name="cast_weights",
    )(w)


def _cast(w, layer):
    w3 = w.reshape(w.shape[0], -1, w.shape[-1])
    rows, cols = w3.shape[1:]
    block = (_pick(rows, (512, 256, 128, 64, 32, 16)), _pick(cols, (4096, 2048, 1024, 512, 256, 128)))
    n = (rows // block[0], cols // block[1])
    return _cast_blocks(w3, layer, block, n, lambda i, j: (i, j), n).reshape(w.shape[1:])


FFN_PAD_BLOCK = 256


def _cast_pad_ffn(w_in, w_out, layer, mult):
    d, f2 = w_in.shape[1:]
    f = f2 // 2
    blk = _pick(f, (FFN_PAD_BLOCK, 128))
    fp = _round_up(f, mult)
    nb, nbp = f // blk, fp // blk
    tr = _pick(d, (2048, 1024, 512, 256, 128))

    def in_src(i, j):
        up = j - nbp
        return (i, jnp.where(j < nbp, jnp.where(j < nb, j, 2 * nb), jnp.where(up < nb, up + nb, 2 * nb)))

    w_in_p = _cast_blocks(w_in, layer, (tr, blk), (d // tr, 2 * nbp), in_src, (d // tr, 2 * nb))
    dc = _pick(d, (4096, 2048, 1024, 512, 256, 128))
    w_out_p = _cast_blocks(w_out, layer, (blk, dc), (nbp, d // dc), lambda i, j: (i, j), (nb, d // dc))
    return w_in_p.reshape(1, d, 2 * fp), w_out_p


def kernel(x, c, ctx, c_ctx, ada_down, ada_up, ada_b, ln_g, ln_b, pool_w, pool_scale, attn_wqkv, attn_q_gain,
           attn_k_gain, attn_wo, lru_w_in, lru_conv_w, lru_conv_b, lru_gate_w, lru_gate_b, lru_lambda, lru_w_out,
           ffn_w_in, ffn_w_out, moe_router, moe_w_in, moe_w_out):
    batch, seq, d = x.shape
    ctx_len = ctx.shape[1]
    depth = ada_down.shape[0]
    assert batch < MOD_ROWS and seq % ctx_len == 0 and ctx_len % SUBLANES == 0
    st = _Stream(batch, seq, ctx_len, d)
    alpha = (2.0 * depth) ** 0.25
    mixers = [i % N_MIXERS for i in range(depth)]
    is_moe = [i % 2 == 1 for i in range(depth)]
    ctx_needed_after = [any(mixers[j] != 0 for j in range(i + 1, depth)) for i in range(depth)]

    cvec = jnp.concatenate([c, c_ctx[None], jnp.zeros((MOD_ROWS - batch - 1, d), F32)], axis=0)
    mods = _adaln(cvec, ada_down, ada_up, ada_b)

    xs = None
    n_rows = st.n_all
    h = None
    for i in range(depth):
        mixer = mixers[i]
        mi = mixers[:i].count(mixer)
        fi = is_moe[:i].count(is_moe[i])
        lng, lnb = ln_g[i], ln_b[i]
        ffn_mod = None if is_moe[i] else (i, 3, 4)
        if not ctx_needed_after[i] and mixer == 0 and n_rows != st.n_lat:
            n_rows = st.n_lat
            xs = None if xs is None else xs[:n_rows]

        if mixer == 0 and xs is None:
            pool = functools.partial(_pool_layer, st, n_rows, i, alpha, mods=mods, w=_cast(pool_w, mi),
                                     pscale=pool_scale[mi], lng=lng[0], lnb=lnb[0])
            xs, hf = pool(x=x.reshape(st.n_lat, d))
            if n_rows > st.n_lat:
                xs, hf = pool(x=ctx.reshape(batch * ctx_len, d), row_offset=st.n_lat, filled=(xs, hf))
        elif mixer == 0:
            xs, hf = _pool_layer(st, n_rows, i, alpha, xs, mods, _cast(pool_w, mi), pool_scale[mi],
                                 lng[0], lnb[0], want_h=not is_moe[i])
        elif mixer == 1:
            qkv = _qkv_proj(st, h, _cast(attn_wqkv, mi), attn_q_gain[mi], attn_k_gain[mi])
            o = _attention(st, qkv, d)
            rows_out = n_rows if ctx_needed_after[i] else st.n_lat
            xs, hf = _mm_ln(st, rows_out, alpha, o, _cast(attn_wo, mi), xs, mods, i, 2, lng[0], lnb[0],
                            next_mod=ffn_mod)
            n_rows = rows_out
        else:
            yx = _lru_in_proj(h, _cast(lru_w_in, mi))
            fwd = _lru_scan(st, yx, lru_conv_w[mi], lru_conv_b[mi], lru_gate_w[mi, 0], lru_gate_b[mi, 0],
                            lru_lambda[mi, 0], False)
            m = _lru_scan(st, yx, lru_conv_w[mi], lru_conv_b[mi], lru_gate_w[mi, 1], lru_gate_b[mi, 1],
                          lru_lambda[mi, 1], True, fwd)
            rows_out = n_rows if ctx_needed_after[i] else st.n_lat
            xs, hf = _mm_ln(st, rows_out, alpha, m, _cast(lru_w_out, mi), xs, mods, i, 2, lng[0], lnb[0],
                            next_mod=ffn_mod)
            n_rows = rows_out

        next_mod = (i + 1, 0, 1) if i + 1 < depth and mixers[i + 1] != 0 else None
        if is_moe[i]:
            xs, h = _moe_layer(st, n_rows, alpha, xs, mods, i, moe_router[fi], moe_w_in, fi, moe_w_out,
                               lng[1], lnb[1], next_mod)
        else:
            w_in_p, w_out_p = _cast_pad_ffn(ffn_w_in, ffn_w_out, fi, 1024)
            act = _swiglu_up(hf, w_in_p, n_rows)
            xs, h = _mm_ln(st, n_rows, alpha, act, w_out_p, xs, mods, i, 5, lng[1], lnb[1], next_mod=next_mod)
    return xs[:st.n_lat].reshape(batch, seq, d)
```

```python
import functools

import jax
import jax.numpy as jnp
from jax import lax
from jax.experimental import pallas as pl
from jax.experimental.pallas import tpu as pltpu

F32 = jnp.float32
BF16 = jnp.bfloat16
HIGHEST = lax.Precision.HIGHEST

HEAD_DIM = 128
AXIS_DIM = HEAD_DIM // 2
GRID_W = 64
ROPE_THETA = 10000.0
LN_EPS = 1e-6
RMS_EPS = 1e-6
LRU_C = 8.0
LOG2E = 1.4426950408889634
CONV_W = 4
CONV_LEFT = CONV_W // 2
POOL_WINDOWS = (2, 4, 8, 16)
N_MOD = 6
N_MIXERS = 3
MOD_ROWS = 8

SUBLANES = 8
VMEM_LIMIT_BYTES = 56 * 1024 * 1024
HALO = SUBLANES


def _params(*sem):
    return pltpu.CompilerParams(dimension_semantics=sem, vmem_limit_bytes=VMEM_LIMIT_BYTES)


def _pick(n, prefs):
    for p in prefs:
        if n % p == 0:
            return p
    return n


def _round_up(n, m):
    return (n + m - 1) // m * m


def _ln_rows(z, g, b):
    mu = jnp.mean(z, axis=-1, keepdims=True)
    zc = z - mu
    var = jnp.mean(zc * zc, axis=-1, keepdims=True)
    return zc * lax.rsqrt(var + LN_EPS) * g + b


class _Stream:
    def __init__(self, batch, seq, ctx_len, d):
        self.batch, self.seq, self.ctx_len, self.d = batch, seq, ctx_len, d
        self.n_lat = batch * seq
        self.n_all = self.n_lat + batch * ctx_len

    def row_tile(self, largest):
        t = largest
        while self.seq % t or (self.batch * self.ctx_len) % t:
            t //= 2
        return t

    def group(self, row0):
        return jnp.minimum(row0 // self.seq, self.batch)

    def mod_spec(self, layer, which, tm, row_offset=0):
        def imap(i, *_):
            return ((layer * MOD_ROWS + self.group(row_offset + i * tm)) * N_MOD + which, 0, 0)
        return pl.BlockSpec((1, 1, self.d), imap)

    def seq_pos(self, row0):
        is_ctx = row0 >= self.n_lat
        pos = jnp.where(is_ctx, (row0 - self.n_lat) % self.ctx_len, row0 % self.seq)
        return pos, jnp.where(is_ctx, self.ctx_len, self.seq)


def _vec_spec(d):
    return pl.BlockSpec((1, d), lambda *_: (0, 0))


def _halo_specs(tm, width, n_rows, col_block=0):
    per = tm // HALO
    last = n_rows // HALO - 1
    prev = pl.BlockSpec((HALO, width), lambda i, *_: (jnp.maximum(i * per - 1, 0), col_block))
    cur = pl.BlockSpec((tm, width), lambda i, *_: (i, col_block))
    nxt = pl.BlockSpec((HALO, width), lambda i, *_: (jnp.minimum((i + 1) * per, last), col_block))
    return prev, cur, nxt


def _adaln_kernel(c_ref, down_ref, up_ref, b_ref, o_ref, t_ref):
    @pl.when(pl.program_id(1) == 0)
    def _():
        c = c_ref[...]
        t_ref[...] = jnp.dot(c * jax.nn.sigmoid(c), down_ref[0], preferred_element_type=F32, precision=HIGHEST)

    o_ref[0] = jnp.dot(t_ref[...], up_ref[0], preferred_element_type=F32, precision=HIGHEST) + b_ref[0]


def _adaln(cvec, ada_down, ada_up, ada_b):
    depth, d, rank = ada_down.shape
    n = ada_up.shape[-1]
    tn = _pick(n, (4096, 2048, 1024, 512, 256, 128))
    out = pl.pallas_call(
        _adaln_kernel,
        out_shape=jax.ShapeDtypeStruct((depth, MOD_ROWS, n), F32),
        grid=(depth, n // tn),
        in_specs=[
            pl.BlockSpec((MOD_ROWS, d), lambda l, j: (0, 0)),
            pl.BlockSpec((1, d, rank), lambda l, j: (l, 0, 0)),
            pl.BlockSpec((1, rank, tn), lambda l, j: (l, 0, j)),
            pl.BlockSpec((1, 1, tn), lambda l, j: (l, 0, j)),
        ],
        out_specs=pl.BlockSpec((1, MOD_ROWS, tn), lambda l, j: (l, 0, j)),
        scratch_shapes=[pltpu.VMEM((MOD_ROWS, rank), F32)],
        compiler_params=_params("arbitrary", "arbitrary"),
        name="adaln",
    )(cvec, ada_down, ada_up, ada_b.reshape(depth, 1, n))
    return out.reshape(depth * MOD_ROWS * N_MOD, 1, d)


POOL_ROW_CHUNK = 128


def _pool_kernel(st, tm, alpha, row_offset, want_h, xp_ref, x_ref, xn_ref, sh_ref, sc_ref, gm_ref, shf_ref, scf_ref,
                 w_ref, ps_ref, lng_ref, lnb_ref, *refs):
    ext_ref, z_ref = refs[-2:]
    xo_ref, ho_ref = refs[-4:-2] if want_h else (refs[-3], None)
    row0 = row_offset + pl.program_id(0) * tm
    pos0, seqlen = st.seq_pos(row0)
    first = pos0 == 0
    last = pos0 + tm == seqlen
    one_sc = 1.0 + sc_ref[0]
    sh = sh_ref[0]
    x = x_ref[...]
    ext_ref[pl.ds(HALO, tm), :] = x * one_sc + sh
    ext_ref[pl.ds(0, HALO), :] = jnp.where(first, 0.0, xp_ref[...] * one_sc + sh)
    ext_ref[pl.ds(HALO + tm, HALO), :] = jnp.where(last, 0.0, xn_ref[...] * one_sc + sh)

    pos = pos0 + lax.broadcasted_iota(jnp.int32, (tm, 1), 0)
    groups = len(POOL_WINDOWS)
    gw = st.d // groups
    for g, k in enumerate(POOL_WINDOWS):
        cols = pl.ds(g * gw, gw)
        lo = jnp.maximum(pos - k // 2, 0)
        hi = jnp.minimum(pos + (k - k // 2), seqlen)
        cnt = (hi - lo).astype(F32)
        n_ext = tm + 2 * HALO
        win = ext_ref[:, cols]
        win = win + pltpu.roll(win, 1, 0)
        n = 2
        while n < k:
            win = pltpu.roll(win, n_ext - n // 2, 0) + pltpu.roll(win, n // 2, 0)
            n *= 2
        win = win[HALO:HALO + tm]
        diff = (win / cnt - ext_ref[pl.ds(HALO, tm), cols]).astype(BF16)
        z_ref[:, cols] = jnp.dot(diff, w_ref[g], preferred_element_type=F32) * ps_ref[:, cols]

    rc = min(POOL_ROW_CHUNK, tm)

    def body(c, carry):
        rows = pl.ds(pl.multiple_of(c * rc, rc), rc)
        _deepnorm_epilogue(alpha, z_ref[rows, :], x_ref, gm_ref, lng_ref, lnb_ref, (shf_ref, scf_ref), xo_ref, ho_ref,
                           rows)
        return carry

    lax.fori_loop(0, tm // rc, body, 0)


def _pool_layer(st, n_out, layer, alpha, x, mods, w, pscale, lng, lnb, row_offset=0, filled=None, want_h=True):
    assert want_h or filled is None
    d = st.d
    tm = min(256, st.ctx_len)
    n_src = x.shape[0]
    prev, cur, nxt = _halo_specs(tm, d, n_src)
    groups, gw, _ = w.shape
    blk0 = row_offset // tm
    in_specs = [prev, cur, nxt] + [st.mod_spec(layer, q, tm, row_offset) for q in (0, 1, 2, 3, 4)] + [
        pl.BlockSpec((groups, gw, gw), lambda i: (0, 0, 0)), _vec_spec(d), _vec_spec(d), _vec_spec(d)]
    args = [x, x, x, mods, mods, mods, mods, mods, w, pscale.reshape(1, d), lng.reshape(1, d), lnb.reshape(1, d)]
    aliases = {}
    if filled is not None:
        aliases = {len(args): 0, len(args) + 1: 1}
        in_specs += [pl.BlockSpec(memory_space=pl.ANY)] * 2
        args += list(filled)
    n_outs = 2 if want_h else 1
    res = pl.pallas_call(
        functools.partial(_pool_kernel, st, tm, alpha, row_offset, want_h),
        out_shape=(jax.ShapeDtypeStruct((n_out, d), F32), jax.ShapeDtypeStruct((n_out, d), BF16))[:n_outs],
        grid=(n_src // tm,),
        in_specs=in_specs,
        out_specs=tuple(pl.BlockSpec((tm, d), lambda i: (blk0 + i, 0)) for _ in range(n_outs)),
        scratch_shapes=[pltpu.VMEM((tm + 2 * HALO, d), F32), pltpu.VMEM((tm, d), F32)],
        input_output_aliases=aliases,
        compiler_params=_params("arbitrary"),
        name="pool_mixer",
    )(*args)
    return res if want_h else (res[0], None)


def _qkv_kernel(n_qk_tiles, a_ref, b_ref, gain_ref, c_ref, s1_ref, s2_ref, o_ref):
    j = pl.program_id(1)
    acc = jnp.dot(a_ref[...], b_ref[...], preferred_element_type=F32)
    tn = acc.shape[1]

    @pl.when(j < n_qk_tiles)
    def _():
        cos, s1, s2 = c_ref[...], s1_ref[...], s2_ref[...]
        for hh in range(tn // HEAD_DIM):
            cols = slice(hh * HEAD_DIM, (hh + 1) * HEAD_DIM)
            xh = acc[:, cols]
            n = xh * lax.rsqrt(jnp.mean(xh * xh, axis=-1, keepdims=True) + RMS_EPS) * gain_ref[:, cols]
            half = AXIS_DIM // 2
            rot = n * cos + pltpu.roll(n, HEAD_DIM - half, 1) * s1 + pltpu.roll(n, half, 1) * s2
            o_ref[:, cols] = rot.astype(o_ref.dtype)

    @pl.when(j >= n_qk_tiles)
    def _():
        o_ref[...] = acc.astype(o_ref.dtype)


def _rope_tables(seq, tm):
    rows = seq // GRID_W
    r = jnp.repeat(jnp.arange(rows, dtype=F32), GRID_W)
    col = jnp.tile(jnp.arange(GRID_W, dtype=F32), rows)
    inv = ROPE_THETA ** (-jnp.arange(0, AXIS_DIM, 2, dtype=F32) / AXIS_DIM)
    ang_r = r[:, None] * inv
    ang_c = col[:, None] * inv
    zero = jnp.zeros_like(ang_r)
    cos = jnp.concatenate([jnp.cos(ang_r), jnp.cos(ang_r), jnp.cos(ang_c), jnp.cos(ang_c)], axis=-1)
    s1 = jnp.concatenate([-jnp.sin(ang_r), zero, -jnp.sin(ang_c), zero], axis=-1)
    s2 = jnp.concatenate([zero, jnp.sin(ang_r), zero, jnp.sin(ang_c)], axis=-1)
    ident = jnp.ones((tm, HEAD_DIM), F32)
    nul = jnp.zeros((tm, HEAD_DIM), F32)
    return (jnp.concatenate([cos, ident]), jnp.concatenate([s1, nul]), jnp.concatenate([s2, nul]))


def _qkv_proj(st, h, w, q_gain, k_gain):
    n_rows, d = h.shape
    n = w.shape[1]
    kv = (n - d) // 2
    tm = st.row_tile(1024)
    tn = _pick(kv, (1024, 512, 256, 128))
    scale = HEAD_DIM ** -0.5 * LOG2E
    gain = jnp.concatenate([jnp.tile(q_gain * scale, d // HEAD_DIM), jnp.tile(k_gain, kv // HEAD_DIM),
                            jnp.ones((kv,), F32)]).reshape(1, n)
    cos, s1, s2 = _rope_tables(st.seq, tm)
    per_seq = st.seq // tm

    def tab_map(i, j):
        return (jnp.where(i * tm < st.n_lat, i % per_seq, per_seq), 0)

    tab = pl.BlockSpec((tm, HEAD_DIM), tab_map)
    return pl.pallas_call(
        functools.partial(_qkv_kernel, (d + kv) // tn),
        out_shape=jax.ShapeDtypeStruct((n_rows, n), BF16),
        grid=(n_rows // tm, n // tn),
        in_specs=[pl.BlockSpec((tm, d), lambda i, j: (i, 0)), pl.BlockSpec((d, tn), lambda i, j: (0, j)),
                  pl.BlockSpec((1, tn), lambda i, j: (0, j)), tab, tab, tab],
        out_specs=pl.BlockSpec((tm, tn), lambda i, j: (i, j)),
        compiler_params=_params("arbitrary", "arbitrary"),
        name="qkv_proj",
    )(h, w, gain, cos, s1, s2)


def _gelu_tanh(x):
    return 0.5 * x * (1.0 + jnp.tanh(0.7978845608028654 * (x + 0.044715 * (x * x * x))))


def _lru_in_kernel(n_gelu_tiles, a_ref, b_ref, o_ref):
    j = pl.program_id(1)
    acc = jnp.dot(a_ref[...], b_ref[...], preferred_element_type=F32)

    @pl.when(j < n_gelu_tiles)
    def _():
        o_ref[...] = _gelu_tanh(acc)

    @pl.when(j >= n_gelu_tiles)
    def _():
        o_ref[...] = acc


def _lru_in_proj(h, w):
    n_rows, d = h.shape
    n = w.shape[1]
    tm = _pick(n_rows, (1024, 512, 256, 128, 64))
    tn = _pick(n // 2, (1024, 512, 256, 128))
    return pl.pallas_call(
        functools.partial(_lru_in_kernel, (n // 2) // tn),
        out_shape=jax.ShapeDtypeStruct((n_rows, n), F32),
        grid=(n_rows // tm, n // tn),
        in_specs=[pl.BlockSpec((tm, d), lambda i, j: (i, 0)), pl.BlockSpec((d, tn), lambda i, j: (0, j))],
        out_specs=pl.BlockSpec((tm, tn), lambda i, j: (i, j)),
        compiler_params=_params("arbitrary", "arbitrary"),
        name="lru_in_proj",
    )(h, w)


def _swiglu_kernel(a_ref, wg_ref, wu_ref, o_ref):
    a = a_ref[...]
    gate = jnp.dot(a, wg_ref[0], preferred_element_type=F32)
    up = jnp.dot(a, wu_ref[0], preferred_element_type=F32)
    o_ref[...] = (gate * jax.nn.sigmoid(gate) * up).astype(o_ref.dtype)


def _swiglu_up(h, w, n_rows):
    e, d, f2 = w.shape
    f = f2 // 2
    tm = _pick(n_rows, (1024, 512, 256, 128, 64))
    tn = _pick(f, (512, 256, 128))
    nj = f // tn
    return pl.pallas_call(
        _swiglu_kernel,
        out_shape=jax.ShapeDtypeStruct((n_rows, e * f), BF16),
        grid=(n_rows // tm, e, nj),
        in_specs=[pl.BlockSpec((tm, d), lambda i, x, j: (i, 0)),
                  pl.BlockSpec((1, d, tn), lambda i, x, j: (x, 0, j)),
                  pl.BlockSpec((1, d, tn), lambda i, x, j: (x, 0, nj + j))],
        out_specs=pl.BlockSpec((tm, tn), lambda i, x, j: (i, x * nj + j)),
        compiler_params=_params("arbitrary", "arbitrary", "arbitrary"),
        name="swiglu_up",
    )(h, w, w)


def _deepnorm_epilogue(alpha, y, x_ref, gate_ref, lng_ref, lnb_ref, next_refs, xo_ref, ho_ref, rows=slice(None)):
    xn = _ln_rows(alpha * x_ref[rows, :] + gate_ref[0] * y, lng_ref[...], lnb_ref[...])
    xo_ref[rows, :] = xn
    if ho_ref is not None:
        shn_ref, scn_ref = next_refs
        ho_ref[rows, :] = (xn * (1.0 + scn_ref[0]) + shn_ref[0]).astype(BF16)


MM_LN_COL_CHUNK = 1024
MM_LN_ROW_CHUNK = 128
MM_LN_X_SLOTS = 4
MM_LN_OUT_SLOTS = 2


def _mm_ln_kernel(alpha, nk, has_next, *refs):
    a_ref, b_ref, gate_ref, lng_ref, lnb_ref = refs[:5]
    next_refs = refs[5:7] if has_next else None
    refs = refs[7:] if has_next else refs[5:]
    if has_next:
        x_hbm, xo_hbm, ho_hbm, acc_ref, xbuf, obuf, hbuf, xsem, osem, hsem = refs
    else:
        x_hbm, xo_hbm, acc_ref, xbuf, obuf, xsem, osem = refs
        ho_hbm = hbuf = hsem = None
    i, k = pl.program_id(0), pl.program_id(1)
    tm, d = acc_ref.shape
    x_slots, rc = xbuf.shape[:2]
    o_slots = obuf.shape[0]
    n_chunks = tm // rc

    def hbm_rows(ref, c):
        return ref.at[pl.ds(pl.multiple_of(i * tm + c * rc, rc), rc), :]

    def x_copy(c, slot):
        return pltpu.make_async_copy(hbm_rows(x_hbm, c), xbuf.at[slot], xsem.at[slot])

    def o_copy(c, slot):
        return pltpu.make_async_copy(obuf.at[slot], hbm_rows(xo_hbm, c), osem.at[slot])

    def h_copy(c, slot):
        return pltpu.make_async_copy(hbuf.at[slot], hbm_rows(ho_hbm, c), hsem.at[slot])

    def wait_out(c, slot):
        o_copy(c, slot).wait()
        if has_next:
            h_copy(c, slot).wait()

    @pl.when(k == 0)
    def _():
        acc_ref[...] = jnp.zeros_like(acc_ref)

    @pl.when(k == nk - 1)
    def _():
        for c in range(min(x_slots, n_chunks)):
            x_copy(c, c).start()

    a = a_ref[...]
    cw = min(MM_LN_COL_CHUNK, d)
    for c in range(d // cw):
        cols = slice(c * cw, (c + 1) * cw)
        acc_ref[:, cols] += jnp.dot(a, b_ref[:, cols], preferred_element_type=F32)

    @pl.when(k == nk - 1)
    def _():
        def body(c, carry):
            xs, os_ = c % x_slots, c % o_slots
            x_copy(c, xs).wait()

            @pl.when(c >= o_slots)
            def _():
                wait_out(c - o_slots, os_)

            y = acc_ref[pl.ds(pl.multiple_of(c * rc, rc), rc), :]
            _deepnorm_epilogue(alpha, y, xbuf.at[xs], gate_ref, lng_ref, lnb_ref, next_refs, obuf.at[os_],
                               hbuf.at[os_] if has_next else None)
            o_copy(c, os_).start()
            if has_next:
                h_copy(c, os_).start()

            @pl.when(c + x_slots < n_chunks)
            def _():
                x_copy(c + x_slots, xs).start()

            return carry

        lax.fori_loop(0, n_chunks, body, 0)
        for c in range(max(n_chunks - o_slots, 0), n_chunks):
            wait_out(c, c % o_slots)


def _mm_ln(st, n_rows, alpha, a, b, x, mods, layer, gate_idx, lng, lnb, next_mod=None):
    kdim, d = b.shape
    tm = st.row_tile(1024)
    tk = _pick(kdim, (1024, 512, 256, 128))
    nk = kdim // tk
    rc = min(MM_LN_ROW_CHUNK, tm)
    has_next = next_mod is not None
    any_spec = pl.BlockSpec(memory_space=pl.ANY)
    in_specs = [pl.BlockSpec((tm, tk), lambda i, k: (i, k)), pl.BlockSpec((tk, d), lambda i, k: (k, 0)),
                st.mod_spec(layer, gate_idx, tm), _vec_spec(d), _vec_spec(d)]
    args = [a, b, mods, lng.reshape(1, d), lnb.reshape(1, d)]
    out_shape = [jax.ShapeDtypeStruct((n_rows, d), F32)]
    scratch = [pltpu.VMEM((tm, d), F32), pltpu.VMEM((MM_LN_X_SLOTS, rc, d), F32),
               pltpu.VMEM((MM_LN_OUT_SLOTS, rc, d), F32)]
    sems = [pltpu.SemaphoreType.DMA((MM_LN_X_SLOTS,)), pltpu.SemaphoreType.DMA((MM_LN_OUT_SLOTS,))]
    if has_next:
        nl, n_sh, n_sc = next_mod
        in_specs += [st.mod_spec(nl, n_sh, tm), st.mod_spec(nl, n_sc, tm)]
        args += [mods, mods]
        out_shape.append(jax.ShapeDtypeStruct((n_rows, d), BF16))
        scratch.append(pltpu.VMEM((MM_LN_OUT_SLOTS, rc, d), BF16))
        sems.append(pltpu.SemaphoreType.DMA((MM_LN_OUT_SLOTS,)))
    res = pl.pallas_call(
        functools.partial(_mm_ln_kernel, alpha, nk, has_next),
        out_shape=tuple(out_shape),
        grid=(n_rows // tm, nk),
        in_specs=in_specs + [any_spec],
        out_specs=tuple(any_spec for _ in out_shape),
        scratch_shapes=scratch + sems,
        compiler_params=_params("arbitrary", "arbitrary"),
        name="matmul_deepnorm",
    )(*args, x)
    return res if has_next else (res[0], None)


ATTN_KEY_ALIGN = 256
ATTN_KEY_RANGES = 2


def _attn_kernel(n_lat_tiles, group, q_ref, kl_ref, vl_ref, kc_ref, vc_ref, o_ref, k_all, v_all):
    qi = pl.program_id(2)
    seq, ctx_len = kl_ref.shape[0], kc_ref.shape[0]

    @pl.when(qi == 0)
    def _():
        k_all[pl.ds(0, seq), :] = kl_ref[...]
        k_all[pl.ds(seq, ctx_len), :] = kc_ref[...]
        v_all[pl.ds(0, seq), pl.ds(0, HEAD_DIM)] = vl_ref[...]
        v_all[pl.ds(seq, ctx_len), pl.ds(0, HEAD_DIM)] = vc_ref[...]
        v_all[:, pl.ds(HEAD_DIM, HEAD_DIM)] = jnp.ones((seq + ctx_len, HEAD_DIM), v_all.dtype)

    def partial_softmax(q, rows):
        s = lax.dot_general(q, k_all[rows, :], (((1,), (1,)), ((), ())), preferred_element_type=F32)
        m = jnp.max(s, axis=-1, keepdims=True)
        return jnp.dot(jnp.exp2(s - m).astype(v_all.dtype), v_all[rows, :], preferred_element_type=F32), m

    def attend(key_ranges):
        for g in range(group):
            cols = slice(g * HEAD_DIM, (g + 1) * HEAD_DIM)
            q = q_ref[:, cols]
            parts = [partial_softmax(q, rows) for rows in key_ranges]
            o, m = parts[0]
            for o2, m2 in parts[1:]:
                m_new = jnp.maximum(m, m2)
                o = o * jnp.exp2(m - m_new) + o2 * jnp.exp2(m2 - m_new)
                m = m_new
            o_ref[:, cols] = (o[:, :HEAD_DIM] / o[:, HEAD_DIM:]).astype(o_ref.dtype)

    @pl.when(qi < n_lat_tiles)
    def _():
        n_keys = seq + ctx_len
        step = _round_up(-(-n_keys // ATTN_KEY_RANGES), ATTN_KEY_ALIGN)
        attend([pl.ds(lo, min(step, n_keys - lo)) for lo in range(0, n_keys, step)])

    @pl.when(qi >= n_lat_tiles)
    def _():
        attend([pl.ds(seq, ctx_len)])


def _attention(st, qkv, d):
    n_rows, n = qkv.shape
    kv = (n - d) // 2
    n_kv = kv // HEAD_DIM
    group = d // kv
    tq = min(256, st.ctx_len)
    ctx_tiles = st.ctx_len // tq
    lat_tiles = st.seq // tq
    qw = group * HEAD_DIM
    k_col0 = d // HEAD_DIM
    v_col0 = (d + kv) // HEAD_DIM

    def q_map(b, h, qi):
        row = jnp.where(qi < lat_tiles, b * lat_tiles + qi, st.n_lat // tq + b * ctx_tiles + (qi - lat_tiles))
        return (row, h)

    ctx_blk0 = st.n_lat // st.ctx_len
    return pl.pallas_call(
        functools.partial(_attn_kernel, lat_tiles, group),
        out_shape=jax.ShapeDtypeStruct((n_rows, d), BF16),
        grid=(st.batch, n_kv, lat_tiles + ctx_tiles),
        in_specs=[pl.BlockSpec((tq, qw), q_map),
                  pl.BlockSpec((st.seq, HEAD_DIM), lambda b, h, qi: (b, k_col0 + h)),
                  pl.BlockSpec((st.seq, HEAD_DIM), lambda b, h, qi: (b, v_col0 + h)),
                  pl.BlockSpec((st.ctx_len, HEAD_DIM), lambda b, h, qi: (ctx_blk0 + b, k_col0 + h)),
                  pl.BlockSpec((st.ctx_len, HEAD_DIM), lambda b, h, qi: (ctx_blk0 + b, v_col0 + h))],
        out_specs=pl.BlockSpec((tq, qw), q_map),
        scratch_shapes=[pltpu.VMEM((st.seq + st.ctx_len, HEAD_DIM), qkv.dtype),
                        pltpu.VMEM((st.seq + st.ctx_len, 2 * HEAD_DIM), qkv.dtype)],
        compiler_params=_params("arbitrary", "arbitrary", "arbitrary"),
        name="gqa_attention",
    )(qkv, qkv, qkv, qkv, qkv)


def _sigmoid(x):
    return 0.5 * jnp.tanh(0.5 * x) + 0.5


def _scan8(a, b, reverse):
    row = lax.broadcasted_iota(jnp.int32, a.shape, 0)
    for s in (1, 2, 4):
        if reverse:
            keep = row < SUBLANES - s
            shift = SUBLANES - s
        else:
            keep = row >= s
            shift = s
        a_sh = jnp.where(keep, pltpu.roll(a, shift, 0), 1.0)
        b_sh = jnp.where(keep, pltpu.roll(b, shift, 0), 0.0)
        b = a * b_sh + b
        a = a * a_sh
    return a, b


def _lru_kernel(tm, lat_tiles, reverse, *refs):
    if reverse:
        xr_ref, gw_ref, gb_ref, lam_ref, recf_ref, gelu_ref, o_ref, a_ref, b_ref, carry_ref = refs
    else:
        (xp_ref, x_ref, xn_ref, cw_ref, cb_ref, gw_ref, gb_ref, lam_ref,
         o_ref, xr_ref, ext_ref, a_ref, b_ref, carry_ref) = refs
    s = pl.program_id(1)
    is_ctx = s == 0

    @pl.when(is_ctx)
    def _():
        carry_ref[...] = jnp.zeros_like(carry_ref)

    if not reverse:
        tile = s - 1
        first = jnp.logical_or(is_ctx, tile == 0)
        last = jnp.logical_or(is_ctx, tile == lat_tiles - 1)
        ext_ref[pl.ds(HALO, tm), :] = x_ref[...]
        ext_ref[pl.ds(0, HALO), :] = jnp.where(first, 0.0, xp_ref[...])
        ext_ref[pl.ds(HALO + tm, HALO), :] = jnp.where(last, 0.0, xn_ref[...])

    heads, bw, _ = gw_ref.shape
    lam = lam_ref[...]
    softplus = jnp.maximum(-lam, 0.0) + jnp.log1p(jnp.exp(-jnp.abs(lam)))
    for h in range(heads):
        cols = pl.ds(h * bw, bw)
        if reverse:
            xr = xr_ref[:, cols]
        else:
            xr = cb_ref[:, cols] + ext_ref[pl.ds(HALO - CONV_LEFT, tm), cols] * cw_ref[0:1, cols]
            for k in range(1, CONV_W):
                xr = xr + ext_ref[pl.ds(HALO - CONV_LEFT + k, tm), cols] * cw_ref[k:k + 1, cols]
            xr_ref[:, cols] = xr
        g = jnp.dot(xr.astype(BF16), gw_ref[h], preferred_element_type=F32)
        r = _sigmoid(g[:, :bw] + gb_ref[0:1, cols])
        i = _sigmoid(g[:, bw:] + gb_ref[1:2, cols])
        log_a = -LRU_C * r * softplus[:, h * bw:(h + 1) * bw]
        a = jnp.exp(log_a)
        a_ref[:, cols] = a
        om = 1.0 - a * a
        b_ref[:, cols] = jnp.where(om > 0.0, om * lax.rsqrt(om), 0.0) * (i * xr)

    n_chunks = tm // SUBLANES

    def body(c, carry):
        c = n_chunks - 1 - c if reverse else c
        rows = pl.ds(pl.multiple_of(c * SUBLANES, SUBLANES), SUBLANES)
        a_cum, b_cum = _scan8(a_ref[rows, :], b_ref[rows, :], reverse)
        hs = a_cum * carry + b_cum
        if reverse:
            o_ref[rows, :] = (gelu_ref[rows, :] * (recf_ref[rows, :] + hs)).astype(o_ref.dtype)
            edge = hs[0:1, :]
        else:
            o_ref[rows, :] = hs
            edge = hs[SUBLANES - 1:SUBLANES, :]
        return jnp.broadcast_to(edge, carry.shape)

    carry_ref[...] = lax.fori_loop(0, n_chunks, body, carry_ref[...])


def _lru_scan(st, yx, conv_w, conv_b, gate_w, gate_b, lam, reverse, fwd=None):
    n_rows = yx.shape[0]
    d = st.d
    tm = min(256, st.ctx_len)
    assert st.ctx_len == tm
    lat_tiles = st.seq // tm
    ctx_blk0 = st.n_lat // tm
    per = tm // HALO
    last_halo = n_rows // HALO - 1

    def blk(b, s):
        lat = b * lat_tiles + (lat_tiles - s if reverse else s - 1)
        return jnp.where(s == 0, ctx_blk0 + b, lat)

    heads = gate_w.shape[1]
    bw = d // heads
    gw = jnp.concatenate([gate_w[0], gate_w[1]], axis=-1).astype(BF16)
    tile = pl.BlockSpec((tm, d), lambda b, s: (blk(b, s), 0))
    gate_specs = [pl.BlockSpec((heads, bw, 2 * bw), lambda b, s: (0, 0, 0)),
                  pl.BlockSpec((2, d), lambda b, s: (0, 0)), _vec_spec(d)]
    gate_args = [gw, gate_b.reshape(2, d), lam.reshape(1, d)]
    scratch = [pltpu.VMEM((tm, d), F32), pltpu.VMEM((tm, d), F32), pltpu.VMEM((SUBLANES, d), F32)]
    if reverse:
        rec_f, xr = fwd
        in_specs = [tile] + gate_specs + [tile, tile]
        args = [xr] + gate_args + [rec_f, yx]
        out_shape = jax.ShapeDtypeStruct((n_rows, d), BF16)
        out_specs = tile
    else:
        in_specs = [pl.BlockSpec((HALO, d), lambda b, s: (jnp.maximum(blk(b, s) * per - 1, 0), 1)),
                    pl.BlockSpec((tm, d), lambda b, s: (blk(b, s), 1)),
                    pl.BlockSpec((HALO, d), lambda b, s: (jnp.minimum((blk(b, s) + 1) * per, last_halo), 1)),
                    pl.BlockSpec((CONV_W, d), lambda b, s: (0, 0)), _vec_spec(d)] + gate_specs
        args = [yx, yx, yx, conv_w, conv_b.reshape(1, d)] + gate_args
        out_shape = (jax.ShapeDtypeStruct((n_rows, d), F32), jax.ShapeDtypeStruct((n_rows, d), F32))
        out_specs = (tile, tile)
        scratch = [pltpu.VMEM((tm + 2 * HALO, d), F32)] + scratch
    return pl.pallas_call(
        functools.partial(_lru_kernel, tm, lat_tiles, reverse),
        out_shape=out_shape,
        grid=(st.batch, lat_tiles + 1),
        in_specs=in_specs,
        out_specs=out_specs,
        scratch_shapes=scratch,
        compiler_params=_params("arbitrary", "arbitrary"),
        name="rglru_reverse" if reverse else "rglru_forward",
    )(*args)


ROUTE_LANES = 8
MOE_ROW_TILE = 512


def _split_hi_lo(x):
    hi = lax.bitcast_convert_type(lax.bitcast_convert_type(x, jnp.uint32) & jnp.uint32(HI16), F32)
    return hi.astype(jnp.bfloat16), (x - hi).astype(jnp.bfloat16)


def _lane_pick(rec, k):
    lane = lax.broadcasted_iota(jnp.int32, rec.shape, 1)
    return jnp.sum(jnp.where(lane == k, rec, 0.0), axis=1, keepdims=True)


def _router_kernel(x_ref, sh_ref, sc_ref, w_ref, route_ref, cnt_ref, run_ref):
    @pl.when(pl.program_id(0) == 0)
    def _():
        run_ref[...] = jnp.zeros_like(run_ref)

    h = x_ref[...] * (1.0 + sc_ref[0]) + sh_ref[0]
    n_exp = w_ref.shape[1] // 2
    h_hi, h_lo = _split_hi_lo(h)
    t = jnp.dot(h_hi, w_ref[...], preferred_element_type=F32)
    u = jnp.dot(h_lo, w_ref[:, :n_exp], preferred_element_type=F32)
    logits = (t[:, n_exp:] + u) + t[:, :n_exp]
    tm = logits.shape[0]
    lane = lax.broadcasted_iota(jnp.int32, logits.shape, 1).astype(F32)
    m1 = jnp.max(logits, axis=-1, keepdims=True)
    i1 = jnp.min(jnp.where(logits == m1, lane, float(n_exp)), axis=-1, keepdims=True)
    pick1 = lane == i1
    rest = jnp.where(pick1, -jnp.inf, logits)
    m2 = jnp.max(rest, axis=-1, keepdims=True)
    i2 = jnp.min(jnp.where(rest == m2, lane, float(n_exp)), axis=-1, keepdims=True)
    pick2 = lane == i2
    e2 = jnp.exp(m2 - m1)
    den = 1.0 + e2
    onehot = jnp.where(pick1, 1.0, jnp.where(pick2, 1.0, 0.0))
    earlier = (lax.broadcasted_iota(jnp.int32, (tm, tm), 0) > lax.broadcasted_iota(jnp.int32, (tm, tm), 1))
    before = jnp.dot(jnp.where(earlier, 1.0, 0.0).astype(BF16), onehot.astype(BF16),
                     preferred_element_type=F32) + run_ref[...]
    rank1 = jnp.sum(jnp.where(pick1, before, 0.0), axis=-1, keepdims=True)
    rank2 = jnp.sum(jnp.where(pick2, before, 0.0), axis=-1, keepdims=True)
    rec = jnp.zeros((tm, ROUTE_LANES), F32)
    for k, v in enumerate((i1, i2, 1.0 / den, e2 / den, rank1, rank2)):
        rec = jnp.where(lax.broadcasted_iota(jnp.int32, rec.shape, 1) == k, v, rec)
    route_ref[...] = rec
    run_ref[...] += jnp.sum(onehot, axis=0, keepdims=True)
    cnt_ref[...] = run_ref[...]


def _router(st, n_rows, x, mods, layer, w):
    d, n_exp = w.shape
    tm = min(256, st.ctx_len)
    w_split = jnp.concatenate(_split_hi_lo(w), axis=1)
    return pl.pallas_call(
        _router_kernel,
        out_shape=(jax.ShapeDtypeStruct((n_rows, ROUTE_LANES), F32), jax.ShapeDtypeStruct((1, n_exp), F32)),
        grid=(n_rows // tm,),
        in_specs=[pl.BlockSpec((tm, d), lambda i: (i, 0)), st.mod_spec(layer, 3, tm), st.mod_spec(layer, 4, tm),
                  pl.BlockSpec((d, 2 * n_exp), lambda i: (0, 0))],
        out_specs=(pl.BlockSpec((tm, ROUTE_LANES), lambda i: (i, 0)), pl.BlockSpec((1, n_exp), lambda i: (0, 0))),
        scratch_shapes=[pltpu.VMEM((1, n_exp), F32)],
        compiler_params=_params("arbitrary"),
        name="moe_router",
    )(x, mods, mods, w_split)


def _dispatch_plan(route, counts, tm):
    n_tok = route.shape[0]
    n_exp = counts.shape[1]
    expert = route[:, 0:2].astype(jnp.int32)
    rank = route[:, 4:6].astype(jnp.int32)
    cnt = counts[0].astype(jnp.int32)
    padded = (cnt + tm - 1) // tm * tm
    ends = jnp.cumsum(padded)
    starts = ends - padded
    dest = (starts[expert] + rank).T.reshape(2 * n_tok)
    n_tiles = (2 * n_tok + n_exp * tm) // tm
    tile_row0 = jnp.arange(n_tiles, dtype=jnp.int32) * tm
    tile_expert = jnp.minimum(jnp.sum(tile_row0[:, None] >= ends[None, :], axis=1), n_exp - 1).astype(jnp.int32)
    n_used = (ends[-1:] // tm).astype(jnp.int32)
    return dest, tile_expert, n_used, n_tiles


HI16 = 0xFFFF0000
ROW_DMA_UNROLL = 8


def _pack_bf16_pairs(lo, hi):
    lo = lax.bitcast_convert_type(lo.astype(jnp.bfloat16).astype(F32), jnp.uint32)
    hi = lax.bitcast_convert_type(hi.astype(jnp.bfloat16).astype(F32), jnp.uint32)
    return (lo >> 16) | (hi & jnp.uint32(HI16))


def _unpack_bf16_pairs(p):
    return lax.bitcast_convert_type(p << 16, F32), lax.bitcast_convert_type(p & jnp.uint32(HI16), F32)


def _zeros_kernel(o_ref):
    o_ref[...] = jnp.zeros_like(o_ref)


def _zeros(rows, cols, dtype):
    tr = _pick(rows, (1024, 512, 256, 128, 64, 32, 16, 8))
    return pl.pallas_call(
        _zeros_kernel,
        out_shape=jax.ShapeDtypeStruct((rows, cols), dtype),
        grid=(rows // tr,),
        out_specs=pl.BlockSpec((tr, cols), lambda i: (i, 0)),
        compiler_params=_params("arbitrary"),
        name="zero_fill",
    )()


def _dispatch_kernel(tm, n_tok, dest_ref, x_ref, sh_ref, sc_ref, init_ref, o_ref, h_ref, sem):
    del init_ref
    base = pl.program_id(0) * tm
    h = x_ref[...] * (1.0 + sc_ref[0]) + sh_ref[0]
    half = h.shape[1] // 2
    h_ref[...] = _pack_bf16_pairs(h[:, :half], h[:, half:])

    def row_copy(r, d):
        return pltpu.make_async_copy(h_ref.at[pl.ds(r, 1), :], o_ref.at[pl.ds(d, 1), :], sem)

    def issue(r, carry):
        row_copy(r, dest_ref[base + r]).start(priority=0)
        row_copy(r, dest_ref[n_tok + base + r]).start(priority=1)
        return carry

    def drain(r, carry):
        row_copy(r, 0).wait()
        row_copy(r, 0).wait()
        return carry

    lax.fori_loop(0, tm, issue, 0, unroll=ROW_DMA_UNROLL)
    lax.fori_loop(0, tm, drain, 0, unroll=ROW_DMA_UNROLL)


def _dispatch(st, n_rows, x, mods, layer, dest, n_out):
    d = st.d
    tm = min(256, st.ctx_len)
    grid_spec = pltpu.PrefetchScalarGridSpec(
        num_scalar_prefetch=1,
        grid=(n_rows // tm,),
        in_specs=[pl.BlockSpec((tm, d), lambda i, dest: (i, 0)), st.mod_spec(layer, 3, tm), st.mod_spec(layer, 4, tm),
                  pl.BlockSpec(memory_space=pl.ANY)],
        out_specs=pl.BlockSpec(memory_space=pl.ANY),
        scratch_shapes=[pltpu.VMEM((tm, d // 2), jnp.uint32), pltpu.SemaphoreType.DMA],
    )
    return pl.pallas_call(
        functools.partial(_dispatch_kernel, tm, n_rows),
        out_shape=jax.ShapeDtypeStruct((n_out, d // 2), jnp.uint32),
        grid_spec=grid_spec,
        input_output_aliases={4: 0},
        compiler_params=_params("arbitrary"),
        name="moe_dispatch",
    )(dest, x, mods, mods, _zeros(n_out, d // 2, jnp.uint32))


def _moe_up_kernel(te_ref, nu_ref, a_ref, wg_ref, wu_ref, o_ref, wbf_ref):
    i = pl.program_id(1)

    @pl.when(jnp.logical_or(i == 0, te_ref[i] != te_ref[jnp.maximum(i - 1, 0)]))
    def _():
        wbf_ref[0] = wg_ref[...].astype(wbf_ref.dtype)
        wbf_ref[1] = wu_ref[...].astype(wbf_ref.dtype)

    @pl.when(i < nu_ref[0])
    def _():
        lo, hi = (t.astype(jnp.bfloat16) for t in _unpack_bf16_pairs(a_ref[...]))
        half = lo.shape[1]

        def proj(which):
            return (jnp.dot(lo, wbf_ref[which, :half, :], preferred_element_type=F32)
                    + jnp.dot(hi, wbf_ref[which, half:, :], preferred_element_type=F32))

        gate, up = proj(0), proj(1)
        o_ref[...] = (gate * jax.nn.sigmoid(gate) * up).astype(o_ref.dtype)

    @pl.when(i >= nu_ref[0])
    def _():
        o_ref[...] = jnp.zeros_like(o_ref)


def _moe_up(xg, w, layer, tile_expert, n_used, tm):
    n_out = xg.shape[0]
    d, f2 = w.shape[2:]
    f = f2 // 2
    tn = _pick(f, (512, 256, 128))
    nj = f // tn
    grid_spec = pltpu.PrefetchScalarGridSpec(
        num_scalar_prefetch=2,
        grid=(nj, n_out // tm),
        in_specs=[pl.BlockSpec((tm, d // 2), lambda j, i, te, nu: (i, 0)),
                  pl.BlockSpec((None, None, d, tn), lambda j, i, te, nu: (layer, te[i], 0, j)),
                  pl.BlockSpec((None, None, d, tn), lambda j, i, te, nu: (layer, te[i], 0, nj + j))],
        out_specs=pl.BlockSpec((tm, tn), lambda j, i, te, nu: (i, j)),
        scratch_shapes=[pltpu.VMEM((2, d, tn), BF16)],
    )
    return pl.pallas_call(
        _moe_up_kernel,
        out_shape=jax.ShapeDtypeStruct((n_out, f), BF16),
        grid_spec=grid_spec,
        compiler_params=_params("arbitrary", "arbitrary"),
        name="moe_up",
    )(tile_expert, n_used, xg, w, w)


def _moe_down_kernel(te_ref, nu_ref, a_ref, w_ref, o_ref, wbf_ref):
    i = pl.program_id(1)

    @pl.when(jnp.logical_or(i == 0, te_ref[i] != te_ref[jnp.maximum(i - 1, 0)]))
    def _():
        wbf_ref[...] = w_ref[...].astype(wbf_ref.dtype)

    @pl.when(i < nu_ref[0])
    def _():
        a = a_ref[...]
        half = wbf_ref.shape[1] // 2
        o_ref[...] = _pack_bf16_pairs(jnp.dot(a, wbf_ref[:, :half], preferred_element_type=F32),
                                      jnp.dot(a, wbf_ref[:, half:], preferred_element_type=F32))

    @pl.when(i >= nu_ref[0])
    def _():
        o_ref[...] = jnp.zeros_like(o_ref)


MOE_DOWN_COLS = 2048


def _moe_down(act, w, layer, tile_expert, n_used, tm):
    n_out, f = act.shape
    d = w.shape[3]
    tn = min(MOE_DOWN_COLS, d)
    grid_spec = pltpu.PrefetchScalarGridSpec(
        num_scalar_prefetch=2,
        grid=(d // tn, n_out // tm),
        in_specs=[pl.BlockSpec((tm, f), lambda j, i, te, nu: (i, 0)),
                  pl.BlockSpec((None, None, f, tn), lambda j, i, te, nu: (layer, te[i], 0, j))],
        out_specs=pl.BlockSpec((tm, tn // 2), lambda j, i, te, nu: (i, j)),
        scratch_shapes=[pltpu.VMEM((f, tn), BF16)],
    )
    return pl.pallas_call(
        _moe_down_kernel,
        out_shape=jax.ShapeDtypeStruct((n_out, d // 2), jnp.uint32),
        grid_spec=grid_spec,
        compiler_params=_params("arbitrary", "arbitrary"),
        name="moe_down",
    )(tile_expert, n_used, act, w)


def _unpack_expert_rows(p, d):
    tn = min(MOE_DOWN_COLS, d)
    lo, hi = _unpack_bf16_pairs(p)
    half = tn // 2
    parts = []
    for j in range(d // tn):
        parts += [lo[:, j * half:(j + 1) * half], hi[:, j * half:(j + 1) * half]]
    return parts


COMBINE_ROW_CHUNK = 128


def _combine_kernel(tm, n_tok, alpha, has_next, dest_ref, *refs):
    route_ref, x_ref, gate_ref, lng_ref, lnb_ref = refs[:5]
    next_refs = refs[5:7] if has_next else None
    refs = refs[7:] if has_next else refs[5:]
    y_ref, xo_ref = refs[:2]
    ho_ref = refs[2] if has_next else None
    buf_ref, sem = refs[-2:]
    i = pl.program_id(0)
    n_steps = pl.num_programs(0)
    slot = i % 2

    def row_copy(slot, which, r, src):
        return pltpu.make_async_copy(y_ref.at[pl.ds(src, 1), :], buf_ref.at[slot, which, pl.ds(r, 1), :],
                                     sem.at[slot])

    def gather(tile, slot):
        def issue(r, carry):
            row_copy(slot, 0, r, dest_ref[tile * tm + r]).start(priority=0)
            row_copy(slot, 1, r, dest_ref[n_tok + tile * tm + r]).start(priority=1)
            return carry
        lax.fori_loop(0, tm, issue, 0, unroll=ROW_DMA_UNROLL)

    @pl.when(i == 0)
    def _():
        gather(0, 0)

    @pl.when(i + 1 < n_steps)
    def _():
        gather(i + 1, 1 - slot)

    def drain(r, carry):
        row_copy(slot, 0, r, 0).wait()
        row_copy(slot, 1, r, 0).wait()
        return carry

    lax.fori_loop(0, tm, drain, 0, unroll=ROW_DMA_UNROLL)
    rc = min(COMBINE_ROW_CHUNK, tm)

    def body(c, carry):
        rows = pl.ds(pl.multiple_of(c * rc, rc), rc)
        rec = route_ref[rows, :]
        g1, g2 = _lane_pick(rec, 2), _lane_pick(rec, 3)
        d = x_ref.shape[1]
        y1 = _unpack_expert_rows(buf_ref[slot, 0, rows, :], d)
        y2 = _unpack_expert_rows(buf_ref[slot, 1, rows, :], d)
        y = jnp.concatenate([g1 * p1 + g2 * p2 for p1, p2 in zip(y1, y2)], axis=1)
        _deepnorm_epilogue(alpha, y, x_ref, gate_ref, lng_ref, lnb_ref, next_refs, xo_ref, ho_ref, rows)
        return carry

    lax.fori_loop(0, tm // rc, body, 0)


def _combine(st, n_rows, alpha, yg, route, dest, x, mods, layer, lng, lnb, next_mod=None):
    d = st.d
    tm = min(256, st.ctx_len)
    has_next = next_mod is not None
    row = lambda i, dest: (i, 0)
    in_specs = [pl.BlockSpec((tm, ROUTE_LANES), row), pl.BlockSpec((tm, d), row), st.mod_spec(layer, 5, tm),
                _vec_spec(d), _vec_spec(d)]
    args = [route, x, mods, lng.reshape(1, d), lnb.reshape(1, d)]
    out_shape = [jax.ShapeDtypeStruct((n_rows, d), F32)]
    out_specs = [pl.BlockSpec((tm, d), row)]
    if has_next:
        nl, n_sh, n_sc = next_mod
        in_specs += [st.mod_spec(nl, n_sh, tm), st.mod_spec(nl, n_sc, tm)]
        args += [mods, mods]
        out_shape.append(jax.ShapeDtypeStruct((n_rows, d), BF16))
        out_specs.append(pl.BlockSpec((tm, d), row))
    in_specs.append(pl.BlockSpec(memory_space=pl.ANY))
    args.append(yg)
    grid_spec = pltpu.PrefetchScalarGridSpec(
        num_scalar_prefetch=1,
        grid=(n_rows // tm,),
        in_specs=in_specs,
        out_specs=tuple(out_specs),
        scratch_shapes=[pltpu.VMEM((2, 2, tm, d // 2), jnp.uint32), pltpu.SemaphoreType.DMA((2,))],
    )
    res = pl.pallas_call(
        functools.partial(_combine_kernel, tm, n_rows, alpha, has_next),
        out_shape=tuple(out_shape),
        grid_spec=grid_spec,
        compiler_params=_params("arbitrary"),
        name="moe_combine",
    )(dest, *args)
    return res if has_next else (res[0], None)


def _moe_layer(st, n_rows, alpha, x, mods, layer, router_w, w_in, moe_index, w_out, lng, lnb, next_mod):
    route, counts = _router(st, n_rows, x, mods, layer, router_w)
    dest, tile_expert, n_used, n_tiles = _dispatch_plan(route, counts, MOE_ROW_TILE)
    xg = _dispatch(st, n_rows, x, mods, layer, dest, n_tiles * MOE_ROW_TILE)
    act = _moe_up(xg, w_in, moe_index, tile_expert, n_used, MOE_ROW_TILE)
    yg = _moe_down(act, w_out, moe_index, tile_expert, n_used, MOE_ROW_TILE)
    return _combine(st, n_rows, alpha, yg, route, dest, x, mods, layer, lng, lnb, next_mod)


def _cast_kernel(is_pad, x_ref, o_ref):
    pad = is_pad(pl.program_id(0), pl.program_id(1))

    @pl.when(jnp.logical_not(pad))
    def _():
        o_ref[...] = x_ref[...].astype(o_ref.dtype)

    @pl.when(pad)
    def _():
        o_ref[...] = jnp.zeros_like(o_ref)


def _cast_blocks(w, layer, block, out_blocks, src_of, n_src):
    def in_map(i, j):
        bi, bj = src_of(i, j)
        return (layer, jnp.minimum(bi, n_src[0] - 1), jnp.minimum(bj, n_src[1] - 1))

    def is_pad(i, j):
        bi, bj = src_of(i, j)
        return jnp.logical_or(bi >= n_src[0], bj >= n_src[1])

    return pl.pallas_call(
        functools.partial(_cast_kernel, is_pad),
        out_shape=jax.ShapeDtypeStruct((out_blocks[0] * block[0], out_blocks[1] * block[1]), BF16),
        grid=out_blocks,
        in_specs=[pl.BlockSpec((None,) + block, in_map)],
        out_specs=pl.BlockSpec(block, lambda i, j: (i, j)),
        compiler_params=_params("arbitrary", "arbitrary"),
        name="cast_weights",
    )(w)


def _cast(w, layer):
    w3 = w.reshape(w.shape[0], -1, w.shape[-1])
    rows, cols = w3.shape[1:]
    block = (_pick(rows, (512, 256, 128, 64, 32, 16)), _pick(cols, (4096, 2048, 1024, 512, 256, 128)))
    n = (rows // block[0], cols // block[1])
    return _cast_blocks(w3, layer, block, n, lambda i, j: (i, j), n).reshape(w.shape[1:])


FFN_PAD_BLOCK = 256


def _cast_pad_ffn(w_in, w_out, layer, mult):
    d, f2 = w_in.shape[1:]
    f = f2 // 2
    blk = _pick(f, (FFN_PAD_BLOCK, 128))
    fp = _round_up(f, mult)
    nb, nbp = f // blk, fp // blk
    tr = _pick(d, (2048, 1024, 512, 256, 128))

    def in_src(i, j):
        up = j - nbp
        return (i, jnp.where(j < nbp, jnp.where(j < nb, j, 2 * nb), jnp.where(up < nb, up + nb, 2 * nb)))

    w_in_p = _cast_blocks(w_in, layer, (tr, blk), (d // tr, 2 * nbp), in_src, (d // tr, 2 * nb))
    dc = _pick(d, (4096, 2048, 1024, 512, 256, 128))
    w_out_p = _cast_blocks(w_out, layer, (blk, dc), (nbp, d // dc), lambda i, j: (i, j), (nb, d // dc))
    return w_in_p.reshape(1, d, 2 * fp), w_out_p


def kernel(x, c, ctx, c_ctx, ada_down, ada_up, ada_b, ln_g, ln_b, pool_w, pool_scale, attn_wqkv, attn_q_gain,
           attn_k_gain, attn_wo, lru_w_in, lru_conv_w, lru_conv_b, lru_gate_w, lru_gate_b, lru_lambda, lru_w_out,
           ffn_w_in, ffn_w_out, moe_router, moe_w_in, moe_w_out):
    batch, seq, d = x.shape
    ctx_len = ctx.shape[1]
    depth = ada_down.shape[0]
    assert batch < MOD_ROWS and seq % ctx_len == 0 and ctx_len % SUBLANES == 0
    st = _Stream(batch, seq, ctx_len, d)
    alpha = (2.0 * depth) ** 0.25
    mixers = [i % N_MIXERS for i in range(depth)]
    is_moe = [i % 2 == 1 for i in range(depth)]
    ctx_needed_after = [any(mixers[j] != 0 for j in range(i + 1, depth)) for i in range(depth)]

    cvec = jnp.concatenate([c, c_ctx[None], jnp.zeros((MOD_ROWS - batch - 1, d), F32)], axis=0)
    mods = _adaln(cvec, ada_down, ada_up, ada_b)

    xs = None
    n_rows = st.n_all
    h = None
    for i in range(depth):
        mixer = mixers[i]
        mi = mixers[:i].count(mixer)
        fi = is_moe[:i].count(is_moe[i])
        lng, lnb = ln_g[i], ln_b[i]
        ffn_mod = None if is_moe[i] else (i, 3, 4)
        if not ctx_needed_after[i] and mixer == 0 and n_rows != st.n_lat:
            n_rows = st.n_lat
            xs = None if xs is None else xs[:n_rows]

        if mixer == 0 and xs is None:
            pool = functools.partial(_pool_layer, st, n_rows, i, alpha, mods=mods, w=_cast(pool_w, mi),
                                     pscale=pool_scale[mi], lng=lng[0], lnb=lnb[0])
            xs, hf = pool(x=x.reshape(st.n_lat, d))
            if n_rows > st.n_lat:
                xs, hf = pool(x=ctx.reshape(batch * ctx_len, d), row_offset=st.n_lat, filled=(xs, hf))
        elif mixer == 0:
            xs, hf = _pool_layer(st, n_rows, i, alpha, xs, mods, _cast(pool_w, mi), pool_scale[mi],
                                 lng[0], lnb[0], want_h=not is_moe[i])
        elif mixer == 1:
            qkv = _qkv_proj(st, h, _cast(attn_wqkv, mi), attn_q_gain[mi], attn_k_gain[mi])
            o = _attention(st, qkv, d)
            rows_out = n_rows if ctx_needed_after[i] else st.n_lat
            xs, hf = _mm_ln(st, rows_out, alpha, o, _cast(attn_wo, mi), xs, mods, i, 2, lng[0], lnb[0],
                            next_mod=ffn_mod)
            n_rows = rows_out
        else:
            yx = _lru_in_proj(h, _cast(lru_w_in, mi))
            fwd = _lru_scan(st, yx, lru_conv_w[mi], lru_conv_b[mi], lru_gate_w[mi, 0], lru_gate_b[mi, 0],
                            lru_lambda[mi, 0], False)
            m = _lru_scan(st, yx, lru_conv_w[mi], lru_conv_b[mi], lru_gate_w[mi, 1], lru_gate_b[mi, 1],
                          lru_lambda[mi, 1], True, fwd)
            rows_out = n_rows if ctx_needed_after[i] else st.n_lat
            xs, hf = _mm_ln(st, rows_out, alpha, m, _cast(lru_w_out, mi), xs, mods, i, 2, lng[0], lnb[0],
                            next_mod=ffn_mod)
            n_rows = rows_out

        next_mod = (i + 1, 0, 1) if i + 1 < depth and mixers[i + 1] != 0 else None
        if is_moe[i]:
            xs, h = _moe_layer(st, n_rows, alpha, xs, mods, i, moe_router[fi], moe_w_in, fi, moe_w_out,
                               lng[1], lnb[1], next_mod)
        else:
            w_in_p, w_out_p = _cast_pad_ffn(ffn_w_in, ffn_w_out, fi, 1024)
            act = _swiglu_up(hf, w_in_p, n_rows)
            xs, h = _mm_ln(st, n_rows, alpha, act, w_out_p, xs, mods, i, 5, lng[1], lnb[1], next_mod=next_mod)
    return xs[:st.n_lat].reshape(batch, seq, d)
```
